```python
import math
import numpy as np
import jax
import jax.numpy as jnp
from jax import lax

D_MODEL = 1024
BATCH = 16
SEQ = 2048
DEPTH = 2

HEAD_DIM = 64
N_HEADS_TOTAL = D_MODEL // HEAD_DIM
N_HEADS_B = N_HEADS_TOTAL // 4
N_HEADS_C = (N_HEADS_TOTAL - N_HEADS_B) // 2
N_HEADS_A = N_HEADS_TOTAL - N_HEADS_B - N_HEADS_C
A_WIDTH = N_HEADS_A * HEAD_DIM
B_WIDTH = N_HEADS_B * HEAD_DIM
C_WIDTH = N_HEADS_C * HEAD_DIM
MIX_WIDTH = A_WIDTH + B_WIDTH + C_WIDTH
COL_SIZES = (3 * A_WIDTH, A_WIDTH, N_HEADS_A, N_HEADS_A,
             B_WIDTH, B_WIDTH, B_WIDTH, B_WIDTH,
             C_WIDTH, C_WIDTH, C_WIDTH, C_WIDTH)
IN_COLS = sum(COL_SIZES)
CONV_WIDTH = 4
GDN_CHUNK = 64
BLOCK = 128
ROPE_DIM = HEAD_DIM // 4
ROPE_THETA = 500000.0
DILATED_PAIRS = ((128, 1), (512, 4), (2048, 16))
RMS_EPS = 1e-6

kernel_name = "hybrid_gdn_stickbreak_dilated"


def rmsnorm(x, w):
    x32 = x.astype(jnp.float32)
    y = x32 * lax.rsqrt(jnp.mean(x32 * x32, axis=-1, keepdims=True) + RMS_EPS)
    return (y * w.astype(jnp.float32)).astype(x.dtype)


def l2norm(x):
    return x * lax.rsqrt(jnp.sum(x * x, axis=-1, keepdims=True) + RMS_EPS)


def causal_depthwise_conv(x, w):
    kw, ch = w.shape
    return lax.conv_general_dilated(
        x, w.astype(x.dtype)[:, None, :], window_strides=(1,),
        padding=((kw - 1, 0),), dimension_numbers=('NWC', 'WIO', 'NWC'),
        feature_group_count=ch)


def partial_rope(x, positions):
    half = ROPE_DIM // 2
    inv_freq = ROPE_THETA ** (-jnp.arange(half, dtype=jnp.float32) / half)
    ang = positions.astype(jnp.float32)[:, None] * inv_freq[None, :]
    cos = jnp.cos(ang)[None, :, None, :]
    sin = jnp.sin(ang)[None, :, None, :]
    x32 = x.astype(jnp.float32)
    x1 = x32[..., :half]
    x2 = x32[..., half:ROPE_DIM]
    out = jnp.concatenate([x1 * cos - x2 * sin, x2 * cos + x1 * sin, x32[..., ROPE_DIM:]], axis=-1)
    return out.astype(x.dtype)


def gated_delta_rule(q, k, v, g, beta):
    b, t, h, dk = q.shape
    dv = v.shape[-1]
    c = GDN_CHUNK
    n = t // c
    f32 = jnp.float32
    q = l2norm(q.astype(f32)) * (dk ** -0.5)
    k = l2norm(k.astype(f32))
    v = v.astype(f32)

    def chunk4(z):
        return z.reshape(b, n, c, h, z.shape[-1]).transpose(0, 1, 3, 2, 4)

    def chunk3(z):
        return z.astype(f32).reshape(b, n, c, h).transpose(0, 1, 3, 2)

    q, k, v = chunk4(q), chunk4(k), chunk4(v)
    g, beta = chunk3(g), chunk3(beta)
    gc = jnp.cumsum(g, axis=-1)
    idx = jnp.arange(c)
    incl = idx[:, None] >= idx[None, :]
    strict = idx[:, None] > idx[None, :]
    decay = jnp.exp(jnp.where(incl, gc[..., :, None] - gc[..., None, :], -jnp.inf))
    kb = k * beta[..., None]
    a = jnp.where(strict, jnp.einsum('bnhid,bnhjd->bnhij', kb, k) * decay, 0.0)
    eye = jnp.eye(c, dtype=f32)
    tmat = lax.linalg.triangular_solve(eye + a, jnp.broadcast_to(eye, a.shape),
                                       left_side=True, lower=True, unit_diagonal=True)
    u = tmat @ (v * beta[..., None])
    w = tmat @ (kb * jnp.exp(gc)[..., None])
    qk = jnp.einsum('bnhid,bnhjd->bnhij', q, k) * decay
    qg = q * jnp.exp(gc)[..., None]
    kg = k * jnp.exp(gc[..., -1:] - gc)[..., None]
    g_last = jnp.exp(gc[..., -1])

    def step(state, inp):
        qg_i, kg_i, u_i, w_i, qk_i, gl_i = inp
        v_new = u_i - w_i @ state
        o_i = qg_i @ state + qk_i @ v_new
        state = state * gl_i[..., None, None] + jnp.einsum('bhcd,bhce->bhde', kg_i, v_new)
        return state, o_i

    xs = tuple(jnp.moveaxis(z, 1, 0) for z in (qg, kg, u, w, qk, g_last))
    s0 = jnp.zeros((b, h, dk, dv), f32)
    _, o = lax.scan(step, s0, xs)
    return o.transpose(1, 0, 3, 2, 4).reshape(b, t, h, dv)


def stick_breaking_attention(q, k, v):
    b, t, h, dh = q.shape
    nb = t // BLOCK
    f32 = jnp.float32
    qb = (q.astype(f32) * (dh ** -0.5)).reshape(b, nb, BLOCK, h, dh).transpose(1, 0, 3, 2, 4)
    kt = k.astype(f32).transpose(0, 2, 1, 3)
    vt = v.astype(f32).transpose(0, 2, 1, 3)
    key_pos = jnp.arange(t)

    def one_block(args):
        q_blk, blk = args
        z = jnp.einsum('bhqd,bhkd->bhqk', q_blk, kt)
        q_pos = blk * BLOCK + jnp.arange(BLOCK)
        earlier = key_pos[None, :] < q_pos[:, None]
        log_beta = jnp.where(earlier, jax.nn.log_sigmoid(z), -jnp.inf)
        log_keep = jnp.where(earlier, jax.nn.log_sigmoid(-z), 0.0)
        log_keep_between = lax.cumsum(log_keep, axis=3, reverse=True) - log_keep
        wts = jnp.exp(log_beta + log_keep_between)
        return jnp.einsum('bhqk,bhkd->bhqd', wts, vt)

    o = lax.map(one_block, (qb, jnp.arange(nb)))
    return o.transpose(1, 0, 3, 2, 4).reshape(b, t, h, dh)


def dilated_window_attention(q, k, v, window, dilation):
    b, t, h, dh = q.shape
    steps = window // dilation
    length = t // dilation
    nb = -(-length // BLOCK)
    lp = nb * BLOCK
    f32 = jnp.float32

    def to_blocks(z):
        z = z.astype(f32).reshape(b, length, dilation, h, dh).transpose(0, 2, 3, 1, 4)
        z = jnp.pad(z, ((0, 0), (0, 0), (0, 0), (0, lp - length), (0, 0)))
        return z.reshape(b, dilation, h, nb, BLOCK, dh)

    def with_prev(z):
        prev = jnp.pad(z, ((0, 0), (0, 0), (0, 0), (1, 0), (0, 0), (0, 0)))[:, :, :, :-1]
        return jnp.concatenate([prev, z], axis=-2)

    qb = to_blocks(q) * (dh ** -0.5)
    kk = with_prev(to_blocks(k))
    vv = with_prev(to_blocks(v))
    s = jnp.einsum('brhnqe,brhnke->brhnqk', qb, kk)
    qi = jnp.arange(BLOCK)[:, None]
    kj = jnp.arange(2 * BLOCK)[None, :]
    rel = qi - kj + BLOCK
    key_idx = jnp.arange(nb)[:, None, None] * BLOCK + kj[None] - BLOCK
    mask = (rel >= 0) & (rel <= steps) & (key_idx >= 0)
    s = jnp.where(mask, s, -jnp.inf)
    m = jnp.max(s, axis=-1, keepdims=True)
    p = jnp.exp(s - m)
    denom = jnp.sum(p, axis=-1, keepdims=True)
    o = jnp.einsum('brhnqk,brhnke->brhnqe', p, vv) / denom
    lse = (m + jnp.log(denom))[..., 0]
    o = o.reshape(b, dilation, h, lp, dh)[:, :, :, :length].transpose(0, 3, 1, 2, 4).reshape(b, t, h, dh)
    lse = lse.reshape(b, dilation, h, lp)[..., :length].transpose(0, 3, 1, 2).reshape(b, t, h)
    return o, lse


def hybrid_layer(x, norm_w, w_in, conv_w, a_log, dt_bias, gdn_norm_w, q_norm_w, k_norm_w, w_out, positions):
    b, t, _ = x.shape
    f32 = jnp.float32
    hdn = rmsnorm(x, norm_w)
    proj = hdn @ w_in
    split_points = np.cumsum(COL_SIZES)[:-1].tolist()
    (qkv_a, z_a, beta_a, alpha_a, q_b, k_b, v_b, z_b,
     q_c, k_c, v_c, z_c) = jnp.split(proj, split_points, axis=-1)

    def heads(z, n_heads):
        return z.reshape(b, t, n_heads, HEAD_DIM)

    qkv_a = jax.nn.silu(causal_depthwise_conv(qkv_a, conv_w))
    q_a, k_a, v_a = jnp.split(qkv_a, 3, axis=-1)
    beta = jax.nn.sigmoid(beta_a.astype(f32))
    g = -jnp.exp(a_log.astype(f32)) * jax.nn.softplus(alpha_a.astype(f32) + dt_bias.astype(f32))
    o_a = gated_delta_rule(heads(q_a, N_HEADS_A), heads(k_a, N_HEADS_A), heads(v_a, N_HEADS_A), g, beta)
    o_a = rmsnorm(o_a, gdn_norm_w).reshape(b, t, A_WIDTH).astype(x.dtype) * jax.nn.silu(z_a)

    o_b = stick_breaking_attention(heads(q_b, N_HEADS_B), heads(k_b, N_HEADS_B), heads(v_b, N_HEADS_B))
    o_b = o_b.reshape(b, t, B_WIDTH).astype(x.dtype) * jax.nn.silu(z_b)

    qc = partial_rope(rmsnorm(heads(q_c, N_HEADS_C), q_norm_w), positions)
    kc = partial_rope(rmsnorm(heads(k_c, N_HEADS_C), k_norm_w), positions)
    vc = heads(v_c, N_HEADS_C)
    outs = []
    lses = []
    for window, dilation in DILATED_PAIRS:
        o_g, lse_g = dilated_window_attention(qc, kc, vc, window, dilation)
        outs.append(o_g)
        lses.append(lse_g)
    mix_w = jax.nn.softmax(jnp.stack(lses, axis=0), axis=0)
    o_c = jnp.einsum('gbth,gbthd->bthd', mix_w, jnp.stack(outs, axis=0))
    o_c = o_c.reshape(b, t, C_WIDTH).astype(x.dtype) * jax.nn.silu(z_c)

    mixed = jnp.concatenate([o_a, o_b, o_c], axis=-1)
    return x + mixed @ w_out


def setup_inputs(seed: int = 0) -> dict:
    key = jax.random.key(seed)
    ks = jax.random.split(key, 10)
    f32 = jnp.float32
    x = jax.random.normal(ks[0], (BATCH, SEQ, D_MODEL), f32)
    norm_w = 1.0 + 0.02 * jax.random.normal(ks[1], (DEPTH, D_MODEL), f32)
    w_in = jax.random.normal(ks[2], (DEPTH, D_MODEL, IN_COLS), f32) * (D_MODEL ** -0.5)
    conv_w = jax.random.normal(ks[3], (DEPTH, CONV_WIDTH, 3 * A_WIDTH), f32) * (CONV_WIDTH ** -0.5)
    a_log = jnp.log(jax.random.uniform(ks[4], (DEPTH, N_HEADS_A), f32, minval=1.0, maxval=16.0))
    dt = jnp.exp(jax.random.uniform(ks[5], (DEPTH, N_HEADS_A), f32,
                                    minval=math.log(1e-3), maxval=math.log(1e-1)))
    dt_bias = dt + jnp.log(-jnp.expm1(-dt))
    gdn_norm_w = 1.0 + 0.02 * jax.random.normal(ks[6], (DEPTH, HEAD_DIM), f32)
    q_norm_w = 1.0 + 0.02 * jax.random.normal(ks[7], (DEPTH, HEAD_DIM), f32)
    k_norm_w = 1.0 + 0.02 * jax.random.normal(ks[8], (DEPTH, HEAD_DIM), f32)
    w_out = jax.random.normal(ks[9], (DEPTH, MIX_WIDTH, D_MODEL), f32) * (MIX_WIDTH ** -0.5)
    return {"x": x, "norm_w": norm_w, "w_in": w_in, "conv_w": conv_w, "a_log": a_log,
            "dt_bias": dt_bias, "gdn_norm_w": gdn_norm_w, "q_norm_w": q_norm_w,
            "k_norm_w": k_norm_w, "w_out": w_out}


def reference(x, norm_w, w_in, conv_w, a_log, dt_bias, gdn_norm_w, q_norm_w, k_norm_w, w_out):
    positions = jnp.arange(x.shape[1], dtype=jnp.int32)
    for layer in range(DEPTH):
        x = hybrid_layer(x, norm_w[layer], w_in[layer], conv_w[layer], a_log[layer], dt_bias[layer],
                         gdn_norm_w[layer], q_norm_w[layer], k_norm_w[layer], w_out[layer], positions)
    return x
```

```python
import functools
import math

import numpy as np
import jax
import jax.numpy as jnp
from jax import lax
from jax.experimental import pallas as pl
from jax.experimental.pallas import tpu as pltpu

F32 = jnp.float32
BF16 = jnp.bfloat16

D_MODEL = 1024
HEAD_DIM = 64
N_HEADS_A, N_HEADS_B, N_HEADS_C = 6, 4, 6
A_WIDTH, B_WIDTH, C_WIDTH = N_HEADS_A * HEAD_DIM, N_HEADS_B * HEAD_DIM, N_HEADS_C * HEAD_DIM
CONV_WIDTH = 4
GDN_CHUNK = 64
ROPE_DIM = HEAD_DIM // 4
ROPE_THETA = 500000.0
DILATIONS = (1, 4, 16)
RMS_EPS = 1e-6

LANES = 128
PAIR = LANES // HEAD_DIM
NEG = -1e30

PA_COLS = 4 * A_WIDTH
BA_COLS = LANES
PB_COLS = 4 * B_WIDTH
PC_COLS = 4 * C_WIDTH
W1_COLS = PA_COLS + BA_COLS + PB_COLS + PC_COLS

VMEM_LIMIT = 56 * 1024 * 1024


def _dot(a, b):
    return jnp.dot(a, b, preferred_element_type=F32)


def _dot_nt(a, b):
    return lax.dot_general(a, b, (((1,), (1,)), ((), ())), preferred_element_type=F32)


def _dot_tn(a, b):
    return lax.dot_general(a, b, (((0,), (0,)), ((), ())), preferred_element_type=F32)


def _aligned(x, m):
    return x if isinstance(x, int) else pl.multiple_of(x, m)


def _sigmoid(x):
    return 1.0 / (1.0 + jnp.exp(-x))


def _softplus(x):
    return jnp.maximum(x, 0.0) + jnp.log(1.0 + jnp.exp(-jnp.abs(x)))


def _split_bf16(x):
    hi = x.astype(BF16)
    lo = (x - hi.astype(F32)).astype(BF16)
    return hi, lo


def _head_lo(shape):
    return lax.broadcasted_iota(jnp.int32, shape, len(shape) - 1) < HEAD_DIM


def _block_diag(m, head_lo):
    z = jnp.zeros_like(m)
    return jnp.concatenate([jnp.where(head_lo, m, z), jnp.where(head_lo, z, m)], axis=0).astype(BF16)


def _head_sums(x2, ones_bd):
    return _dot(x2.astype(BF16), ones_bd)


def _ones_bd():
    r = lax.broadcasted_iota(jnp.int32, (LANES, LANES), 0) // HEAD_DIM
    c = lax.broadcasted_iota(jnp.int32, (LANES, LANES), 1) // HEAD_DIM
    return (r == c).astype(BF16)


def _in_proj_kernel(x_ref, nw_ref, w_ref, pa_ref, ba_ref, pb_ref, pc_ref):
    x = x_ref[...]
    ms = jnp.mean(x * x, axis=-1, keepdims=True)
    h = (x * lax.rsqrt(ms + RMS_EPS) * nw_ref[...]).astype(BF16)
    c0 = 0
    pa_ref[...] = _dot(h, w_ref[:, c0:c0 + PA_COLS]).astype(BF16)
    c0 += PA_COLS
    ba_ref[...] = _dot(h, w_ref[:, c0:c0 + BA_COLS])
    c0 += BA_COLS
    pb_ref[...] = _dot(h, w_ref[:, c0:c0 + PB_COLS]).astype(BF16)
    c0 += PB_COLS
    pc_ref[...] = _dot(h, w_ref[:, c0:c0 + PC_COLS]).astype(BF16)


def _in_proj(x2d, norm_w, w1, tm):
    n = x2d.shape[0]
    row = lambda i: (i, 0)
    fixed = lambda i: (0, 0)
    return pl.pallas_call(
        _in_proj_kernel,
        grid=(n // tm,),
        in_specs=[pl.BlockSpec((tm, D_MODEL), row),
                  pl.BlockSpec((1, D_MODEL), fixed),
                  pl.BlockSpec((D_MODEL, W1_COLS), fixed)],
        out_specs=[pl.BlockSpec((tm, PA_COLS), row), pl.BlockSpec((tm, BA_COLS), row),
                   pl.BlockSpec((tm, PB_COLS), row), pl.BlockSpec((tm, PC_COLS), row)],
        out_shape=[jax.ShapeDtypeStruct((n, PA_COLS), BF16), jax.ShapeDtypeStruct((n, BA_COLS), F32),
                   jax.ShapeDtypeStruct((n, PB_COLS), BF16), jax.ShapeDtypeStruct((n, PC_COLS), BF16)],
        compiler_params=pltpu.CompilerParams(dimension_semantics=("arbitrary",), vmem_limit_bytes=VMEM_LIMIT),
        name="in_proj",
    )(x2d, norm_w, w1)


def _out_proj_kernel(x_ref, oa_ref, ob_ref, oc_ref, w_ref, o_ref):
    acc = _dot(oa_ref[...], w_ref[0:A_WIDTH, :])
    acc = acc + _dot(ob_ref[...], w_ref[A_WIDTH:A_WIDTH + B_WIDTH, :])
    acc = acc + _dot(oc_ref[...], w_ref[A_WIDTH + B_WIDTH:, :])
    o_ref[...] = x_ref[...] + acc


def _out_proj(x2d, oa, ob, oc, w_out, tm):
    n = x2d.shape[0]
    row = lambda i: (i, 0)
    fixed = lambda i: (0, 0)
    return pl.pallas_call(
        _out_proj_kernel,
        grid=(n // tm,),
        in_specs=[pl.BlockSpec((tm, D_MODEL), row), pl.BlockSpec((tm, A_WIDTH), row),
                  pl.BlockSpec((tm, B_WIDTH), row), pl.BlockSpec((tm, C_WIDTH), row),
                  pl.BlockSpec((D_MODEL, D_MODEL), fixed)],
        out_specs=pl.BlockSpec((tm, D_MODEL), row),
        out_shape=jax.ShapeDtypeStruct((n, D_MODEL), F32),
        compiler_params=pltpu.CompilerParams(dimension_semantics=("arbitrary",), vmem_limit_bytes=VMEM_LIMIT),
        name="out_proj",
    )(x2d, oa, ob, oc, w_out)


SB_BLOCK = 256


def _sb_kernel(q_ref, k_ref, v_ref, z_ref, o_ref):
    seq = q_ref.shape[1]
    nb = seq // SB_BLOCK
    head_lo = _head_lo((1, LANES))
    row = lax.broadcasted_iota(jnp.int32, (SB_BLOCK, SB_BLOCK), 0)
    col = lax.broadcasted_iota(jnp.int32, (SB_BLOCK, SB_BLOCK), 1)
    earlier = col < row
    jj = lax.broadcasted_iota(jnp.int32, (2 * SB_BLOCK, SB_BLOCK), 0) % SB_BLOCK
    ss = lax.broadcasted_iota(jnp.int32, (2 * SB_BLOCK, SB_BLOCK), 1)
    suffix = (jj > ss).astype(BF16)

    def tile(qh, k0, carry, acc, diag):
        k = k_ref[0, pl.ds(k0, SB_BLOCK), :]
        v = v_ref[0, pl.ds(k0, SB_BLOCK), :]
        z = _dot_nt(qh, k)
        lk = -_softplus(z)
        if diag:
            lk = jnp.where(earlier, lk, 0.0)
        hi, lo = _split_bf16(lk)
        later = _dot(jnp.concatenate([hi, lo], axis=1), suffix)
        logw = z + lk + later + carry
        if diag:
            logw = jnp.where(earlier, logw, NEG)
        acc = acc + _dot(jnp.exp(logw).astype(BF16), v)
        carry = carry + jnp.sum(lk, axis=1, keepdims=True)
        return carry, acc

    def q_block(i, _):
        q0 = pl.multiple_of(i * SB_BLOCK, SB_BLOCK)
        q = q_ref[0, pl.ds(q0, SB_BLOCK), :] * jnp.asarray(HEAD_DIM ** -0.5, BF16)
        zero = jnp.zeros_like(q)
        qs = (jnp.where(head_lo, q, zero), jnp.where(head_lo, zero, q))
        c0 = jnp.zeros((SB_BLOCK, 1), F32)
        a0 = jnp.zeros((SB_BLOCK, LANES), F32)
        state = tuple(tile(qh, q0, c0, a0, True) for qh in qs)

        def k_block(j, st):
            k0 = pl.multiple_of((i - 1 - j) * SB_BLOCK, SB_BLOCK)
            return tuple(tile(qh, k0, c, a, False) for qh, (c, a) in zip(qs, st))

        state = lax.fori_loop(0, i, k_block, state)
        o = jnp.where(head_lo, state[0][1], state[1][1])
        zg = z_ref[0, pl.ds(q0, SB_BLOCK), :].astype(F32)
        o_ref[0, pl.ds(q0, SB_BLOCK), :] = (o * (zg * _sigmoid(zg))).astype(BF16)
        return 0

    lax.fori_loop(0, nb, q_block, 0)


def _stick_breaking(pb):
    bsz, seq, _ = pb.shape
    npair = N_HEADS_B // PAIR
    spec = lambda g: pl.BlockSpec((1, seq, LANES), lambda b, p, g=g: (b, 0, g * npair + p))
    return pl.pallas_call(
        _sb_kernel,
        grid=(bsz, npair),
        in_specs=[spec(0), spec(1), spec(2), spec(3)],
        out_specs=pl.BlockSpec((1, seq, LANES), lambda b, p: (b, 0, p)),
        out_shape=jax.ShapeDtypeStruct((bsz, seq, B_WIDTH), BF16),
        compiler_params=pltpu.CompilerParams(dimension_semantics=("arbitrary", "arbitrary"),
                                             vmem_limit_bytes=VMEM_LIMIT),
        name="stick_breaking",
    )(pb, pb, pb, pb)


DW_BLOCK = 128


def _dw_kernel(q_ref, k_ref, v_ref, z_ref, qw_ref, kw_ref, cos_ref, sina_ref, sinb_ref, o_ref,
               qf, kf, vf, qg, kg, vg, m_st, l_st, acc_st):
    seq = q_ref.shape[1]
    head_lo = _head_lo((1, LANES))
    ones_bd = _ones_bd()
    n_rows = 256

    def prep(i, _):
        r0 = pl.multiple_of(i * n_rows, n_rows)
        rows = pl.ds(r0, n_rows)
        cos, sina, sinb = cos_ref[rows, :], sina_ref[rows, :], sinb_ref[rows, :]

        def norm_rope(x, w):
            ms = _head_sums(x * x, ones_bd) * (1.0 / HEAD_DIM)
            y = x * lax.rsqrt(ms + RMS_EPS) * w
            return y * cos + pltpu.roll(y, LANES - ROPE_DIM // 2, 1) * sina + pltpu.roll(y, ROPE_DIM // 2, 1) * sinb

        qf[rows, :] = norm_rope(q_ref[0, rows, :].astype(F32), qw_ref[...]) * (HEAD_DIM ** -0.5)
        kf[rows, :] = norm_rope(k_ref[0, rows, :].astype(F32), kw_ref[...])
        vf[rows, :] = v_ref[0, rows, :].astype(F32)
        return 0

    lax.fori_loop(0, seq // n_rows, prep, 0)

    qi = lax.broadcasted_iota(jnp.int32, (DW_BLOCK, DW_BLOCK), 0)
    kj = lax.broadcasted_iota(jnp.int32, (DW_BLOCK, DW_BLOCK), 1)
    own_ok = kj <= qi
    prev_ok = kj >= qi
    both_ok = jnp.concatenate([prev_ok, own_ok], axis=1)

    def block(c0, nat, has_prev, first, last):
        q = qg[pl.ds(c0, DW_BLOCK), :]
        zero = jnp.zeros_like(q)
        if has_prev:
            keys = pl.ds(c0 - DW_BLOCK, 2 * DW_BLOCK)
            ok = both_ok
        else:
            keys = pl.ds(c0, DW_BLOCK)
            ok = own_ok
        k = kg[keys, :]
        v = vg[keys, :]
        s = [jnp.where(ok, _dot_nt(qh, k), NEG) for qh in (jnp.where(head_lo, q, zero), jnp.where(head_lo, zero, q))]
        m_blk = jnp.where(head_lo, jnp.max(s[0], axis=1, keepdims=True), jnp.max(s[1], axis=1, keepdims=True))
        if first:
            m_new = m_blk
        else:
            m_old = m_st[nat, :]
            m_new = jnp.maximum(m_old, m_blk)
            alpha = jnp.exp(m_old - m_new)
        m_other = pltpu.roll(m_new, HEAD_DIM, 1)
        m_h = (jnp.where(head_lo, m_new, m_other), jnp.where(head_lo, m_other, m_new))
        width = 2 if has_prev else 1
        p = [jnp.exp(sh - jnp.concatenate([mh] * width, axis=1)) for sh, mh in zip(s, m_h)]
        l_blk = jnp.where(head_lo, jnp.sum(p[0], axis=1, keepdims=True), jnp.sum(p[1], axis=1, keepdims=True))
        pv = jnp.where(head_lo, _dot(p[0].astype(BF16), v), _dot(p[1].astype(BF16), v))
        if first:
            l_new, acc_new = l_blk, pv
        else:
            l_new = alpha * l_st[nat, :] + l_blk
            acc_new = alpha * acc_st[nat, :] + pv
        if last:
            zg = z_ref[0, nat, :].astype(F32)
            o_ref[0, nat, :] = (acc_new / l_new * (zg * _sigmoid(zg))).astype(BF16)
        else:
            m_st[nat, :] = m_new
            l_st[nat, :] = l_new
            acc_st[nat, :] = acc_new

    order = tuple(reversed(DILATIONS))
    for d in order:
        first, last = d == order[0], d == order[-1]
        cls_len = seq // d
        nblk = cls_len // DW_BLOCK
        for r in range(d):
            src = pl.ds(r, cls_len, stride=d) if d > 1 else pl.ds(0, seq)
            dst = pl.ds(r * cls_len, cls_len)
            qg[dst, :] = qf[src, :].astype(BF16)
            kg[dst, :] = kf[src, :].astype(BF16)
            vg[dst, :] = vf[src, :].astype(BF16)

        def nat_rows(r, j, d=d):
            start = r + j * (DW_BLOCK * d)
            return pl.ds(start, DW_BLOCK, stride=d) if d > 1 else pl.ds(_aligned(start, DW_BLOCK), DW_BLOCK)

        def per_class(r, _, d=d, cls_len=cls_len, nblk=nblk, first=first, last=last, nat_rows=nat_rows):
            base = _aligned(r * cls_len, DW_BLOCK)
            block(base, nat_rows(r, 0), False, first, last)

            def inner(j, _):
                block(pl.multiple_of(base + j * DW_BLOCK, DW_BLOCK), nat_rows(r, j), True, first, last)
                return 0

            if nblk > 1:
                lax.fori_loop(1, nblk, inner, 0)
            return 0

        if d > 1:
            lax.fori_loop(0, d, per_class, 0)
        else:
            per_class(0, 0)


def _rope_tables(seq):
    half = ROPE_DIM // 2
    inv_freq = ROPE_THETA ** (-jnp.arange(half, dtype=F32) / half)
    ang = jnp.arange(seq, dtype=jnp.int32).astype(F32)[:, None] * inv_freq[None, :]
    cos, sin = jnp.cos(ang), jnp.sin(ang)
    ones = jnp.ones((seq, HEAD_DIM - ROPE_DIM), F32)
    zeros_h = jnp.zeros((seq, half), F32)
    zeros_t = jnp.zeros((seq, HEAD_DIM - ROPE_DIM), F32)
    cos_h = jnp.concatenate([cos, cos, ones], axis=1)
    sina_h = jnp.concatenate([-sin, zeros_h, zeros_t], axis=1)
    sinb_h = jnp.concatenate([zeros_h, sin, zeros_t], axis=1)
    tile = lambda t: jnp.tile(t, (1, PAIR))
    return tile(cos_h), tile(sina_h), tile(sinb_h)


def _dilated(pc, q_norm_w, k_norm_w):
    bsz, seq, _ = pc.shape
    npair = N_HEADS_C // PAIR
    cos, sina, sinb = _rope_tables(seq)
    qw = jnp.tile(q_norm_w.astype(F32), PAIR)[None, :]
    kw = jnp.tile(k_norm_w.astype(F32), PAIR)[None, :]
    spec = lambda g: pl.BlockSpec((1, seq, LANES), lambda b, p, g=g: (b, 0, g * npair + p))
    fixed = lambda shape: pl.BlockSpec(shape, lambda b, p: (0, 0))
    return pl.pallas_call(
        _dw_kernel,
        grid=(bsz, npair),
        in_specs=[spec(0), spec(1), spec(2), spec(3), fixed((1, LANES)), fixed((1, LANES)),
                  fixed((seq, LANES)), fixed((seq, LANES)), fixed((seq, LANES))],
        out_specs=pl.BlockSpec((1, seq, LANES), lambda b, p: (b, 0, p)),
        out_shape=jax.ShapeDtypeStruct((bsz, seq, C_WIDTH), BF16),
        scratch_shapes=[pltpu.VMEM((seq, LANES), F32)] * 3 + [pltpu.VMEM((seq, LANES), BF16)] * 3
        + [pltpu.VMEM((seq, LANES), F32)] * 3,
        compiler_params=pltpu.CompilerParams(dimension_semantics=("arbitrary", "arbitrary"),
                                             vmem_limit_bytes=VMEM_LIMIT),
        name="dilated_window",
    )(pc, pc, pc, pc, qw, kw, cos, sina, sinb)


GDN_ROWS = 256
GDN_GROUP = 4


def _gdn_kernel(pa_ref, ba_ref, cw_ref, alog_ref, dtb_ref, gnw_ref, o_ref,
                xf, qn_s, kn_s, v_s, be_s, gc_s, qt_s, o0_s, nf_s, kw_s, gl_s, oc_s):
    seq = pa_ref.shape[1]
    n_chunks = seq // GDN_CHUNK
    npair = N_HEADS_A // PAIR
    lane = lax.broadcasted_iota(jnp.int32, (1, LANES), 1)
    head_lo = lane < HEAD_DIM
    ones_bd = _ones_bd()
    bd_mask = (lax.broadcasted_iota(jnp.int32, (LANES, LANES), 0) // HEAD_DIM
               == lax.broadcasted_iota(jnp.int32, (LANES, LANES), 1) // HEAD_DIM)

    ri = lax.broadcasted_iota(jnp.int32, (GDN_ROWS, 2 * GDN_ROWS), 0)
    ci = lax.broadcasted_iota(jnp.int32, (GDN_ROWS, 2 * GDN_ROWS), 1) % GDN_ROWS
    cum_mat = ((ri // GDN_CHUNK == ci // GDN_CHUNK) & (ci <= ri)).astype(BF16)

    ii = lax.broadcasted_iota(jnp.int32, (GDN_CHUNK, LANES), 0)
    jl = lax.broadcasted_iota(jnp.int32, (GDN_CHUNK, LANES), 1) % HEAD_DIM
    incl = ii >= jl
    strict = ii > jl
    eye_pair = ii == jl
    eye_f = eye_pair.astype(F32)

    xf[0:8, :] = jnp.zeros((8, LANES), F32)

    for p in range(npair):
        er = lax.broadcasted_iota(jnp.int32, (2 * LANES, 2 * LANES), 0) % LANES
        ec = lax.broadcasted_iota(jnp.int32, (2 * LANES, 2 * LANES), 1)
        want = jnp.where(ec < LANES, 2 * p + ec // HEAD_DIM, N_HEADS_A + 2 * p + (ec - LANES) // HEAD_DIM)
        expand = (er == want).astype(BF16)

        def conv_silu(col, r0):
            w = cw_ref[:, col:col + LANES]
            x = xf[pl.ds(r0, GDN_ROWS + 8), :]
            y = w[3:4, :] * x[8:, :]
            for kk in range(CONV_WIDTH - 1):
                y = y + w[kk:kk + 1, :] * pltpu.roll(x, CONV_WIDTH - 1 - kk, 0)[8:, :]
            return y * _sigmoid(y)

        for t, dst in enumerate((qn_s, kn_s, v_s)):
            col = t * A_WIDTH + p * LANES

            def to_f32(i, _, col=col):
                r0 = pl.multiple_of(i * GDN_ROWS, GDN_ROWS)
                xf[pl.ds(r0 + 8, GDN_ROWS), :] = pa_ref[0, pl.ds(r0, GDN_ROWS), col:col + LANES].astype(F32)
                return 0

            lax.fori_loop(0, seq // GDN_ROWS, to_f32, 0)

            def conv_rows(i, _, col=col, dst=dst, t=t):
                r0 = pl.multiple_of(i * GDN_ROWS, GDN_ROWS)
                y = conv_silu(col, r0)
                if t < 2:
                    y = y * lax.rsqrt(_head_sums(y * y, ones_bd) + RMS_EPS)
                    if t == 0:
                        y = y * (HEAD_DIM ** -0.5)
                dst[pl.ds(r0, GDN_ROWS), :] = y
                return 0

            lax.fori_loop(0, seq // GDN_ROWS, conv_rows, 0)

        def gates(i, _, expand=expand):
            r0 = pl.multiple_of(i * GDN_ROWS, GDN_ROWS)
            ba = ba_ref[0, pl.ds(r0, GDN_ROWS), :]
            g = -jnp.exp(alog_ref[...]) * _softplus(ba + dtb_ref[...])
            g = jnp.where((lane >= N_HEADS_A) & (lane < 2 * N_HEADS_A), g, 0.0)
            hi, lo = _split_bf16(g)
            gc = _dot(cum_mat, jnp.concatenate([hi, lo], axis=0))
            nar = jnp.where(lane < N_HEADS_A, _sigmoid(ba), gc)
            hi, lo = _split_bf16(nar)
            wide = _dot(jnp.concatenate([hi, lo], axis=1), expand)
            be_s[pl.ds(r0, GDN_ROWS), :] = wide[:, :LANES]
            gc_s[pl.ds(r0, GDN_ROWS), :] = wide[:, LANES:]
            return 0

        lax.fori_loop(0, seq // GDN_ROWS, gates, 0)

        def chunk(c, p=p):
            r0 = pl.multiple_of(c * GDN_CHUNK, GDN_CHUNK)
            rows = pl.ds(r0, GDN_CHUNK)
            qn, kn, v, be, gc = qn_s[rows, :], kn_s[rows, :], v_s[rows, :], be_s[rows, :], gc_s[rows, :]
            g_row = jnp.sum(jnp.where(eye_pair, gc, 0.0), axis=0, keepdims=True)
            decay = jnp.exp(jnp.where(incl, gc - g_row, NEG))
            eg = jnp.exp(gc)
            g_last = gc[GDN_CHUNK - 1:GDN_CHUNK, :]
            kg = kn * jnp.exp(g_last - gc)
            kb = kn * be
            ap = _dot_nt(jnp.concatenate([kb, qn], axis=0).astype(BF16), _block_diag(kn, head_lo))
            x = jnp.where(strict, -ap[:GDN_CHUNK] * decay, 0.0)
            pm = ap[GDN_CHUNK:] * decay
            tm = eye_f + x
            xr = _dot(x.astype(BF16), _block_diag(x, head_lo))
            for r in range(1, 6):
                rhs = _block_diag(xr, head_lo)
                if r < 5:
                    y = _dot(jnp.concatenate([xr, tm], axis=0).astype(BF16), rhs)
                    xr = y[:GDN_CHUNK]
                    tm = tm + y[GDN_CHUNK:]
                else:
                    tm = tm + _dot(tm.astype(BF16), rhs)
            uw = _dot(tm.astype(BF16), jnp.concatenate([_block_diag(v * be, head_lo),
                                                         _block_diag(kb * eg, head_lo)], axis=1))
            u, w = uw[:, :LANES], uw[:, LANES:]
            puw = _dot(pm.astype(BF16), jnp.concatenate([_block_diag(u, head_lo), _block_diag(w, head_lo)], axis=1))
            kuw = _dot_tn(kg.astype(BF16), uw.astype(BF16))
            qt_s[p, rows, :] = (qn * eg - puw[:, LANES:]).astype(BF16)
            o0_s[p, rows, :] = puw[:, :LANES]
            nf_s[p, c] = jnp.where(bd_mask, kuw[:, :LANES], 0.0)
            kw_s[p, c] = jnp.where(bd_mask, kuw[:, LANES:], 0.0).astype(BF16)
            gl_s[p, c] = jnp.broadcast_to(jnp.exp(g_last), (8, LANES))

        def chunk_group(i, _, chunk=chunk):
            for gi in range(GDN_GROUP):
                chunk(i * GDN_GROUP + gi)
            return 0

        lax.fori_loop(0, n_chunks // GDN_GROUP, chunk_group, 0)

    def scan(c, states):
        r0 = pl.multiple_of(c * GDN_CHUNK, GDN_CHUNK)
        rows = pl.ds(r0, GDN_CHUNK)
        new = []
        for p in range(npair):
            s = states[p]
            sb = s.astype(BF16)
            oc_s[p, rows, :] = _dot(qt_s[p, rows, :], sb) + o0_s[p, rows, :]
            new.append(gl_s[p, c][0:1, :] * s + nf_s[p, c] - _dot(kw_s[p, c], sb))
        return tuple(new)

    lax.fori_loop(0, n_chunks, scan, tuple(jnp.zeros((LANES, LANES), F32) for _ in range(npair)))

    def finish(i, _):
        r0 = pl.multiple_of(i * GDN_ROWS, GDN_ROWS)
        rows = pl.ds(r0, GDN_ROWS)
        for p in range(npair):
            o = oc_s[p, rows, :]
            ms = _head_sums(o * o, ones_bd) * (1.0 / HEAD_DIM)
            y = o * lax.rsqrt(ms + RMS_EPS) * gnw_ref[...]
            zg = pa_ref[0, rows, 3 * A_WIDTH + p * LANES:3 * A_WIDTH + (p + 1) * LANES].astype(F32)
            o_ref[0, rows, p * LANES:(p + 1) * LANES] = (y * (zg * _sigmoid(zg))).astype(BF16)
        return 0

    lax.fori_loop(0, seq // GDN_ROWS, finish, 0)


def _gdn(pa, ba, conv_w, a_log, dt_bias, gdn_norm_w):
    bsz, seq, _ = pa.shape
    npair = N_HEADS_A // PAIR
    n_chunks = seq // GDN_CHUNK
    pad = lambda vec: jnp.zeros((1, LANES), F32).at[0, N_HEADS_A:2 * N_HEADS_A].set(vec.astype(F32))
    gnw = jnp.tile(gdn_norm_w.astype(F32), PAIR)[None, :]
    fixed = lambda shape: pl.BlockSpec(shape, lambda b: (0,) * len(shape))
    return pl.pallas_call(
        _gdn_kernel,
        grid=(bsz,),
        in_specs=[pl.BlockSpec((1, seq, PA_COLS), lambda b: (b, 0, 0)),
                  pl.BlockSpec((1, seq, BA_COLS), lambda b: (b, 0, 0)),
                  fixed((CONV_WIDTH, 3 * A_WIDTH)), fixed((1, LANES)), fixed((1, LANES)), fixed((1, LANES))],
        out_specs=pl.BlockSpec((1, seq, A_WIDTH), lambda b: (b, 0, 0)),
        out_shape=jax.ShapeDtypeStruct((bsz, seq, A_WIDTH), BF16),
        scratch_shapes=[
            pltpu.VMEM((seq + 8, LANES), F32),
            pltpu.VMEM((seq, LANES), F32), pltpu.VMEM((seq, LANES), F32), pltpu.VMEM((seq, LANES), F32),
            pltpu.VMEM((seq, LANES), F32), pltpu.VMEM((seq, LANES), F32),
            pltpu.VMEM((npair, seq, LANES), BF16),
            pltpu.VMEM((npair, seq, LANES), F32),
            pltpu.VMEM((npair, n_chunks, LANES, LANES), F32),
            pltpu.VMEM((npair, n_chunks, LANES, LANES), BF16),
            pltpu.VMEM((npair, n_chunks, 8, LANES), F32),
            pltpu.VMEM((npair, seq, LANES), F32),
        ],
        compiler_params=pltpu.CompilerParams(dimension_semantics=("arbitrary",), vmem_limit_bytes=VMEM_LIMIT),
        name="gated_delta",
    )(pa, ba, conv_w.astype(F32), pad(a_log), pad(dt_bias), gnw)


def _pack_w_in(w_in):
    n_ba = 2 * N_HEADS_A
    a_end = PA_COLS
    ba_end = a_end + n_ba
    w_a, w_ba, w_rest = w_in[:, :a_end], w_in[:, a_end:ba_end], w_in[:, ba_end:]
    w_ba = jnp.pad(w_ba, ((0, 0), (0, BA_COLS - n_ba)))
    return jnp.concatenate([w_a, w_ba, w_rest], axis=1).astype(BF16)


def _layer(x, norm_w, w_in, conv_w, a_log, dt_bias, gdn_norm_w, q_norm_w, k_norm_w, w_out):
    bsz, seq, _ = x.shape
    x2d = x.reshape(bsz * seq, D_MODEL)
    pa, ba, pb, pc = _in_proj(x2d, norm_w.astype(F32)[None, :], _pack_w_in(w_in), 512)
    shape3 = lambda t: t.reshape(bsz, seq, t.shape[-1])
    oa = _gdn(shape3(pa), shape3(ba), conv_w, a_log, dt_bias, gdn_norm_w)
    ob = _stick_breaking(shape3(pb))
    oc = _dilated(shape3(pc), q_norm_w, k_norm_w)
    flat = lambda t: t.reshape(bsz * seq, t.shape[-1])
    out = _out_proj(x2d, flat(oa), flat(ob), flat(oc), w_out.astype(BF16), 512)
    return out.reshape(bsz, seq, D_MODEL)


def kernel(x, norm_w, w_in, conv_w, a_log, dt_bias, gdn_norm_w, q_norm_w, k_norm_w, w_out):
    for layer in range(norm_w.shape[0]):
        x = _layer(x, norm_w[layer], w_in[layer], conv_w[layer], a_log[layer], dt_bias[layer],
                   gdn_norm_w[layer], q_norm_w[layer], k_norm_w[layer], w_out[layer])
    return x
```

```python
import functools

import jax
import jax.numpy as jnp
from jax import lax
from jax.experimental import pallas as pl
from jax.experimental.pallas import tpu as pltpu

F32 = jnp.float32
BF16 = jnp.bfloat16

D_MODEL = 1024
HEAD_DIM = 64
N_HEADS_A, N_HEADS_B, N_HEADS_C = 6, 4, 6
A_WIDTH, B_WIDTH, C_WIDTH = N_HEADS_A * HEAD_DIM, N_HEADS_B * HEAD_DIM, N_HEADS_C * HEAD_DIM
CONV_WIDTH = 4
GDN_CHUNK = 64
ROPE_DIM = HEAD_DIM // 4
ROPE_THETA = 500000.0
DILATIONS = (1, 4, 16)
RMS_EPS = 1e-6

LANES = 128
PAIR = LANES // HEAD_DIM
NEG = -1e30

PA_COLS = 4 * A_WIDTH
BA_COLS = LANES
PB_COLS = 4 * B_WIDTH
PC_COLS = 4 * C_WIDTH
IN_COLS = PA_COLS + 2 * N_HEADS_A + PB_COLS + PC_COLS
W1_COLS = PA_COLS + BA_COLS + PB_COLS + PC_COLS

VMEM_LIMIT = 56 * 1024 * 1024


def _dot(a, b):
    return jnp.dot(a, b, preferred_element_type=F32)


def _dot_nt(a, b):
    return lax.dot_general(a, b, (((1,), (1,)), ((), ())), preferred_element_type=F32)


def _dot_tn(a, b):
    return lax.dot_general(a, b, (((0,), (0,)), ((), ())), preferred_element_type=F32)


def _aligned(x, m):
    return x if isinstance(x, int) else pl.multiple_of(x, m)


def _sigmoid(x):
    return 1.0 / (1.0 + jnp.exp(-x))


def _softplus(x):
    return jnp.maximum(x, 0.0) + jnp.log(1.0 + jnp.exp(-jnp.abs(x)))


def _split_bf16(x):
    hi = x.astype(BF16)
    lo = (x - hi.astype(F32)).astype(BF16)
    return hi, lo


def _head_lo(shape):
    return lax.broadcasted_iota(jnp.int32, shape, len(shape) - 1) < HEAD_DIM


def _block_diag(m, head_lo):
    z = jnp.zeros_like(m)
    return jnp.concatenate([jnp.where(head_lo, m, z), jnp.where(head_lo, z, m)], axis=0).astype(BF16)


def _head_sums(x2, ones_bd):
    return _dot(x2.astype(BF16), ones_bd)


def _ones_bd():
    r = lax.broadcasted_iota(jnp.int32, (LANES, LANES), 0) // HEAD_DIM
    c = lax.broadcasted_iota(jnp.int32, (LANES, LANES), 1) // HEAD_DIM
    return (r == c).astype(BF16)


W_ROWS = 128


def _in_proj_kernel(x_ref, nw_ref, w_ref, pa_ref, ba_ref, pb_ref, pc_ref, wb):
    @pl.when(pl.program_id(0) == 0)
    def _():
        def rows(i, _):
            r = pl.ds(pl.multiple_of(i * W_ROWS, W_ROWS), W_ROWS)
            wb[r, 0:PA_COLS + BA_COLS] = w_ref[0, r, 0:PA_COLS + BA_COLS].astype(BF16)
            wb[r, PA_COLS + BA_COLS:W1_COLS] = w_ref[0, r, PA_COLS + 2 * N_HEADS_A:IN_COLS].astype(BF16)
            return 0

        lax.fori_loop(0, D_MODEL // W_ROWS, rows, 0)

    x = x_ref[...]
    ms = jnp.mean(x * x, axis=-1, keepdims=True)
    h = (x * lax.rsqrt(ms + RMS_EPS) * nw_ref[...]).astype(BF16)
    c0 = 0
    pa_ref[...] = _dot(h, wb[:, c0:c0 + PA_COLS]).astype(BF16)
    c0 += PA_COLS
    ba_ref[...] = _dot(h, wb[:, c0:c0 + BA_COLS])
    c0 += BA_COLS
    pb_ref[...] = _dot(h, wb[:, c0:c0 + PB_COLS]).astype(BF16)
    c0 += PB_COLS
    pc_ref[...] = _dot(h, wb[:, c0:c0 + PC_COLS]).astype(BF16)


def _in_proj(x2d, norm_w, w_in, layer, tm):
    n = x2d.shape[0]
    row = lambda i: (i, 0)
    return pl.pallas_call(
        _in_proj_kernel,
        grid=(n // tm,),
        in_specs=[pl.BlockSpec((tm, D_MODEL), row),
                  pl.BlockSpec((1, D_MODEL), lambda i: (0, 0)),
                  pl.BlockSpec((1, D_MODEL, IN_COLS), lambda i: (layer, 0, 0), pipeline_mode=pl.Buffered(1))],
        out_specs=[pl.BlockSpec((tm, PA_COLS), row), pl.BlockSpec((tm, BA_COLS), row),
                   pl.BlockSpec((tm, PB_COLS), row), pl.BlockSpec((tm, PC_COLS), row)],
        out_shape=[jax.ShapeDtypeStruct((n, PA_COLS), BF16), jax.ShapeDtypeStruct((n, BA_COLS), F32),
                   jax.ShapeDtypeStruct((n, PB_COLS), BF16), jax.ShapeDtypeStruct((n, PC_COLS), BF16)],
        scratch_shapes=[pltpu.VMEM((D_MODEL, W1_COLS), BF16)],
        compiler_params=pltpu.CompilerParams(dimension_semantics=("arbitrary",), vmem_limit_bytes=VMEM_LIMIT),
        name="in_proj",
    )(x2d, norm_w, w_in)


def _out_proj_kernel(x_ref, oa_ref, ob_ref, oc_ref, w_ref, o_ref):
    acc = _dot(oa_ref[...], w_ref[0:A_WIDTH, :])
    acc = acc + _dot(ob_ref[...], w_ref[A_WIDTH:A_WIDTH + B_WIDTH, :])
    acc = acc + _dot(oc_ref[...], w_ref[A_WIDTH + B_WIDTH:, :])
    o_ref[...] = x_ref[...] + acc


def _out_proj(x2d, oa, ob, oc, w_out, tm):
    n = x2d.shape[0]
    row = lambda i: (i, 0)
    fixed = lambda i: (0, 0)
    return pl.pallas_call(
        _out_proj_kernel,
        grid=(n // tm,),
        in_specs=[pl.BlockSpec((tm, D_MODEL), row), pl.BlockSpec((tm, A_WIDTH), row),
                  pl.BlockSpec((tm, B_WIDTH), row), pl.BlockSpec((tm, C_WIDTH), row),
                  pl.BlockSpec((D_MODEL, D_MODEL), fixed)],
        out_specs=pl.BlockSpec((tm, D_MODEL), row),
        out_shape=jax.ShapeDtypeStruct((n, D_MODEL), F32),
        compiler_params=pltpu.CompilerParams(dimension_semantics=("arbitrary",), vmem_limit_bytes=VMEM_LIMIT),
        name="out_proj",
    )(x2d, oa, ob, oc, w_out)


SB_BLOCK = 256


def _sb_kernel(q_ref, k_ref, v_ref, z_ref, o_ref):
    seq = q_ref.shape[1]
    nb = seq // SB_BLOCK
    head_lo = _head_lo((1, LANES))
    row = lax.broadcasted_iota(jnp.int32, (SB_BLOCK, SB_BLOCK), 0)
    col = lax.broadcasted_iota(jnp.int32, (SB_BLOCK, SB_BLOCK), 1)
    earlier = col < row
    jj = lax.broadcasted_iota(jnp.int32, (2 * SB_BLOCK, SB_BLOCK), 0) % SB_BLOCK
    ss = lax.broadcasted_iota(jnp.int32, (2 * SB_BLOCK, SB_BLOCK), 1)
    suffix = (jj > ss).astype(BF16)

    def logits(qs, k0, diag):
        k = k_ref[0, pl.ds(k0, SB_BLOCK), :]
        z = [_dot_nt(qh, k) for qh in qs]
        lk = [-_softplus(zh) for zh in z]
        if diag:
            lk = [jnp.where(earlier, x, 0.0) for x in lk]
        parts = [_split_bf16(x) for x in lk]
        later = [_dot(jnp.concatenate([hi, lo], axis=1), suffix) for hi, lo in parts]
        pre = [zh + x + y for zh, x, y in zip(z, lk, later)]
        if diag:
            pre = [jnp.where(earlier, x, NEG) for x in pre]
        return pre, [jnp.sum(x, axis=1, keepdims=True) for x in lk]

    def accumulate(pre, carry, acc, k0):
        v = v_ref[0, pl.ds(k0, SB_BLOCK), :]
        w = [jnp.exp(x + c).astype(BF16) for x, c in zip(pre, carry)]
        return [a + _dot(x, v) for a, x in zip(acc, w)]

    def q_block(i, _):
        q0 = pl.multiple_of(i * SB_BLOCK, SB_BLOCK)
        q = q_ref[0, pl.ds(q0, SB_BLOCK), :] * jnp.asarray(HEAD_DIM ** -0.5, BF16)
        zero = jnp.zeros_like(q)
        qs = (jnp.where(head_lo, q, zero), jnp.where(head_lo, zero, q))
        pre, rs = logits(qs, q0, True)
        carry = [jnp.zeros((SB_BLOCK, 1), F32)] * PAIR
        acc = [jnp.zeros((SB_BLOCK, LANES), F32)] * PAIR

        def k_block(j, st):
            pre, rs, carry, acc = st
            k0 = pl.multiple_of((i - 1 - j) * SB_BLOCK, SB_BLOCK)
            nxt, nrs = logits(qs, k0, False)
            acc = accumulate(pre, carry, acc, pl.multiple_of((i - j) * SB_BLOCK, SB_BLOCK))
            carry = [c + r for c, r in zip(carry, rs)]
            return nxt, nrs, carry, acc

        pre, rs, carry, acc = lax.fori_loop(0, i, k_block, (pre, rs, carry, acc))
        acc = accumulate(pre, carry, acc, 0)
        o = jnp.where(head_lo, acc[0], acc[1])
        zg = z_ref[0, pl.ds(q0, SB_BLOCK), :].astype(F32)
        o_ref[0, pl.ds(q0, SB_BLOCK), :] = (o * (zg * _sigmoid(zg))).astype(BF16)
        return 0

    lax.fori_loop(0, nb, q_block, 0)


def _stick_breaking(pb):
    bsz, seq, _ = pb.shape
    npair = N_HEADS_B // PAIR
    spec = lambda g: pl.BlockSpec((1, seq, LANES), lambda b, p, g=g: (b, 0, g * npair + p))
    return pl.pallas_call(
        _sb_kernel,
        grid=(bsz, npair),
        in_specs=[spec(0), spec(1), spec(2), spec(3)],
        out_specs=pl.BlockSpec((1, seq, LANES), lambda b, p: (b, 0, p)),
        out_shape=jax.ShapeDtypeStruct((bsz, seq, B_WIDTH), BF16),
        compiler_params=pltpu.CompilerParams(dimension_semantics=("arbitrary", "arbitrary"),
                                             vmem_limit_bytes=VMEM_LIMIT),
        name="stick_breaking",
    )(pb, pb, pb, pb)


DW_BLOCK = 128
DW_GROUP = {16: 8, 4: 4, 1: 5}


def _dw_kernel(q_ref, k_ref, v_ref, z_ref, qw_ref, kw_ref, cos_ref, sina_ref, sinb_ref, o_ref,
               qf, kf, vf, qg, kg, vg, m_st, l_st, acc_st):
    seq = q_ref.shape[1]
    head_lo = _head_lo((1, LANES))
    ones_bd = _ones_bd()
    n_rows = 256

    def prep(i, _):
        r0 = pl.multiple_of(i * n_rows, n_rows)
        rows = pl.ds(r0, n_rows)
        cos, sina, sinb = cos_ref[rows, :], sina_ref[rows, :], sinb_ref[rows, :]

        def norm_rope(x, w):
            ms = _head_sums(x * x, ones_bd) * (1.0 / HEAD_DIM)
            y = x * lax.rsqrt(ms + RMS_EPS) * w
            return y * cos + pltpu.roll(y, LANES - ROPE_DIM // 2, 1) * sina + pltpu.roll(y, ROPE_DIM // 2, 1) * sinb

        qf[rows, :] = norm_rope(q_ref[0, rows, :].astype(F32), qw_ref[...]) * (HEAD_DIM ** -0.5)
        kf[rows, :] = norm_rope(k_ref[0, rows, :].astype(F32), kw_ref[...])
        vf[rows, :] = v_ref[0, rows, :].astype(F32)
        return 0

    lax.fori_loop(0, seq // n_rows, prep, 0)

    qi = lax.broadcasted_iota(jnp.int32, (DW_BLOCK, DW_BLOCK), 0)
    kj = lax.broadcasted_iota(jnp.int32, (DW_BLOCK, DW_BLOCK), 1)
    own_ok = kj <= qi
    prev_ok = kj >= qi
    both_ok = jnp.concatenate([prev_ok, own_ok], axis=1)

    def pair_of(a, b):
        return jnp.where(head_lo, a, b)

    def blocks(items, has_prev, first, last):
        width = 2 if has_prev else 1
        ok = both_ok if has_prev else own_ok
        q = [qg[pl.ds(c0, DW_BLOCK), :] for c0, _ in items]
        keys = [pl.ds(c0 - DW_BLOCK, 2 * DW_BLOCK) if has_prev else pl.ds(c0, DW_BLOCK) for c0, _ in items]
        k = [kg[ks, :] for ks in keys]
        s = [[jnp.where(ok, _dot_nt(qh, kk), NEG)
              for qh in (jnp.where(head_lo, x, jnp.zeros_like(x)), jnp.where(head_lo, jnp.zeros_like(x), x))]
             for x, kk in zip(q, k)]
        m_blk = [[jnp.max(sh, axis=1, keepdims=True) for sh in sb] for sb in s]
        if first:
            m_h = [[jnp.broadcast_to(mh, (DW_BLOCK, LANES)) for mh in mb] for mb in m_blk]
            m_new = [pair_of(mb[0], mb[1]) for mb in m_blk]
        else:
            m_old = [m_st[nat, :] for _, nat in items]
            m_swap = [pltpu.roll(x, HEAD_DIM, 1) for x in m_old]
            m_old_h = [(pair_of(x, y), pair_of(y, x)) for x, y in zip(m_old, m_swap)]
            m_h = [[jnp.maximum(o, b) for o, b in zip(oh, mb)] for oh, mb in zip(m_old_h, m_blk)]
            m_new = [pair_of(mh[0], mh[1]) for mh in m_h]
            alpha = [jnp.exp(o - n) for o, n in zip(m_old, m_new)]
        p = [[jnp.exp(sh - jnp.concatenate([mh] * width, axis=1)) for sh, mh in zip(sb, mb)] for sb, mb in zip(s, m_h)]
        v = [vg[ks, :] for ks in keys]
        pv = [pair_of(_dot(pb[0].astype(BF16), vv), _dot(pb[1].astype(BF16), vv)) for pb, vv in zip(p, v)]
        l_blk = [pair_of(jnp.sum(pb[0], axis=1, keepdims=True), jnp.sum(pb[1], axis=1, keepdims=True)) for pb in p]
        if first:
            l_new, acc_new = l_blk, pv
        else:
            l_new = [a * l_st[nat, :] + x for a, (_, nat), x in zip(alpha, items, l_blk)]
            acc_new = [a * acc_st[nat, :] + x for a, (_, nat), x in zip(alpha, items, pv)]
        for (_, nat), m, l, acc in zip(items, m_new, l_new, acc_new):
            if last:
                zg = z_ref[0, nat, :].astype(F32)
                o_ref[0, nat, :] = (acc / l * (zg * _sigmoid(zg))).astype(BF16)
            else:
                m_st[nat, :] = m
                l_st[nat, :] = l
                acc_st[nat, :] = acc

    order = tuple(reversed(DILATIONS))
    for d in order:
        first, last = d == order[0], d == order[-1]
        cls_len = seq // d
        nblk = cls_len // DW_BLOCK
        grp = DW_GROUP[d]
        for r in range(d):
            src = pl.ds(r, cls_len, stride=d) if d > 1 else pl.ds(0, seq)
            dst = pl.ds(r * cls_len, cls_len)
            qg[dst, :] = qf[src, :].astype(BF16)
            kg[dst, :] = kf[src, :].astype(BF16)
            vg[dst, :] = vf[src, :].astype(BF16)

        def item(r, j, d=d, cls_len=cls_len):
            c0 = _aligned(r * cls_len + j * DW_BLOCK, DW_BLOCK)
            start = r + j * (DW_BLOCK * d)
            nat = pl.ds(start, DW_BLOCK, stride=d) if d > 1 else pl.ds(_aligned(start, DW_BLOCK), DW_BLOCK)
            return c0, nat

        if nblk == 1:
            def body(i, _, grp=grp, item=item, first=first, last=last):
                blocks([item(i * grp + g, 0) for g in range(grp)], False, first, last)
                return 0

            lax.fori_loop(0, d // grp, body, 0)
        elif d > 1:
            blocks([item(r, 0) for r in range(d)], False, first, last)

            def body(j, _, d=d, item=item, first=first, last=last):
                blocks([item(r, j) for r in range(d)], True, first, last)
                return 0

            lax.fori_loop(1, nblk, body, 0)
        else:
            blocks([item(0, 0)], False, first, last)

            def body(i, _, grp=grp, item=item, first=first, last=last):
                blocks([item(0, 1 + i * grp + g) for g in range(grp)], True, first, last)
                return 0

            lax.fori_loop(0, (nblk - 1) // grp, body, 0)


def _rope_tables(seq):
    half = ROPE_DIM // 2
    inv_freq = ROPE_THETA ** (-jnp.arange(half, dtype=F32) / half)
    ang = jnp.arange(seq, dtype=jnp.int32).astype(F32)[:, None] * inv_freq[None, :]
    cos, sin = jnp.cos(ang), jnp.sin(ang)
    ones = jnp.ones((seq, HEAD_DIM - ROPE_DIM), F32)
    zeros_h = jnp.zeros((seq, half), F32)
    zeros_t = jnp.zeros((seq, HEAD_DIM - ROPE_DIM), F32)
    cos_h = jnp.concatenate([cos, cos, ones], axis=1)
    sina_h = jnp.concatenate([-sin, zeros_h, zeros_t], axis=1)
    sinb_h = jnp.concatenate([zeros_h, sin, zeros_t], axis=1)
    tile = lambda t: jnp.tile(t, (1, PAIR))
    return tile(cos_h), tile(sina_h), tile(sinb_h)


def _dilated(pc, q_norm_w, k_norm_w):
    bsz, seq, _ = pc.shape
    npair = N_HEADS_C // PAIR
    assert (seq // DW_BLOCK - 1) % DW_GROUP[1] == 0 and 16 % DW_GROUP[16] == 0
    cos, sina, sinb = _rope_tables(seq)
    qw = jnp.tile(q_norm_w.astype(F32), PAIR)[None, :]
    kw = jnp.tile(k_norm_w.astype(F32), PAIR)[None, :]
    spec = lambda g: pl.BlockSpec((1, seq, LANES), lambda b, p, g=g: (b, 0, g * npair + p))
    fixed = lambda shape: pl.BlockSpec(shape, lambda b, p: (0, 0))
    return pl.pallas_call(
        _dw_kernel,
        grid=(bsz, npair),
        in_specs=[spec(0), spec(1), spec(2), spec(3), fixed((1, LANES)), fixed((1, LANES)),
                  fixed((seq, LANES)), fixed((seq, LANES)), fixed((seq, LANES))],
        out_specs=pl.BlockSpec((1, seq, LANES), lambda b, p: (b, 0, p)),
        out_shape=jax.ShapeDtypeStruct((bsz, seq, C_WIDTH), BF16),
        scratch_shapes=[pltpu.VMEM((seq, LANES), F32)] * 3 + [pltpu.VMEM((seq, LANES), BF16)] * 3
        + [pltpu.VMEM((seq, LANES), F32)] * 3,
        compiler_params=pltpu.CompilerParams(dimension_semantics=("arbitrary", "arbitrary"),
                                             vmem_limit_bytes=VMEM_LIMIT),
        name="dilated_window",
    )(pc, pc, pc, pc, qw, kw, cos, sina, sinb)


GDN_ROWS = 256
GDN_GROUP = 8


def _gdn_kernel(pa_ref, ba_ref, cw_ref, alog_ref, dtb_ref, gnw_ref, o_ref,
                xf, qn_s, kn_s, v_s, be_s, gc_s, qt_s, o0_s, nf_s, kw_s, gl_s, oc_s):
    seq = pa_ref.shape[1]
    n_chunks = seq // GDN_CHUNK
    npair = N_HEADS_A // PAIR
    lane = lax.broadcasted_iota(jnp.int32, (1, LANES), 1)
    head_lo = lane < HEAD_DIM
    ones_bd = _ones_bd()
    bd_mask = (lax.broadcasted_iota(jnp.int32, (LANES, LANES), 0) // HEAD_DIM
               == lax.broadcasted_iota(jnp.int32, (LANES, LANES), 1) // HEAD_DIM)

    ri = lax.broadcasted_iota(jnp.int32, (GDN_ROWS, 2 * GDN_ROWS), 0)
    ci = lax.broadcasted_iota(jnp.int32, (GDN_ROWS, 2 * GDN_ROWS), 1) % GDN_ROWS
    cum_mat = ((ri // GDN_CHUNK == ci // GDN_CHUNK) & (ci <= ri)).astype(BF16)

    ii = lax.broadcasted_iota(jnp.int32, (GDN_CHUNK, LANES), 0)
    jl = lax.broadcasted_iota(jnp.int32, (GDN_CHUNK, LANES), 1) % HEAD_DIM
    incl = ii >= jl
    strict = ii > jl
    eye_pair = ii == jl
    eye_f = eye_pair.astype(F32)

    xf[0:8, :] = jnp.zeros((8, LANES), F32)

    for p in range(npair):
        er = lax.broadcasted_iota(jnp.int32, (2 * LANES, 2 * LANES), 0) % LANES
        ec = lax.broadcasted_iota(jnp.int32, (2 * LANES, 2 * LANES), 1)
        want = jnp.where(ec < LANES, 2 * p + ec // HEAD_DIM, N_HEADS_A + 2 * p + (ec - LANES) // HEAD_DIM)
        expand = (er == want).astype(BF16)

        def conv_silu(col, r0):
            w = cw_ref[:, col:col + LANES]
            x = xf[pl.ds(r0, GDN_ROWS + 8), :]
            y = w[3:4, :] * x[8:, :]
            for kk in range(CONV_WIDTH - 1):
                y = y + w[kk:kk + 1, :] * pltpu.roll(x, CONV_WIDTH - 1 - kk, 0)[8:, :]
            return y * _sigmoid(y)

        for t, dst in enumerate((qn_s, kn_s, v_s)):
            col = t * A_WIDTH + p * LANES

            def to_f32(i, _, col=col):
                r0 = pl.multiple_of(i * GDN_ROWS, GDN_ROWS)
                xf[pl.ds(r0 + 8, GDN_ROWS), :] = pa_ref[0, pl.ds(r0, GDN_ROWS), col:col + LANES].astype(F32)
                return 0

            lax.fori_loop(0, seq // GDN_ROWS, to_f32, 0)

            def conv_rows(i, _, col=col, dst=dst, t=t, p=p):
                r0 = pl.multiple_of(i * GDN_ROWS, GDN_ROWS)
                y = conv_silu(col, r0)
                if t < 2:
                    y = y * lax.rsqrt(_head_sums(y * y, ones_bd) + RMS_EPS)
                    if t == 0:
                        y = y * (HEAD_DIM ** -0.5)
                dst[p, pl.ds(r0, GDN_ROWS), :] = y
                return 0

            lax.fori_loop(0, seq // GDN_ROWS, conv_rows, 0)

        def gates(i, _, expand=expand, p=p):
            r0 = pl.multiple_of(i * GDN_ROWS, GDN_ROWS)
            ba = ba_ref[0, pl.ds(r0, GDN_ROWS), :]
            g = -jnp.exp(alog_ref[...]) * _softplus(ba + dtb_ref[...])
            g = jnp.where((lane >= N_HEADS_A) & (lane < 2 * N_HEADS_A), g, 0.0)
            hi, lo = _split_bf16(g)
            gc = _dot(cum_mat, jnp.concatenate([hi, lo], axis=0))
            nar = jnp.where(lane < N_HEADS_A, _sigmoid(ba), gc)
            hi, lo = _split_bf16(nar)
            wide = _dot(jnp.concatenate([hi, lo], axis=1), expand)
            be_s[p, pl.ds(r0, GDN_ROWS), :] = wide[:, :LANES]
            gc_s[p, pl.ds(r0, GDN_ROWS), :] = wide[:, LANES:]
            return 0

        lax.fori_loop(0, seq // GDN_ROWS, gates, 0)

    def bd(ms):
        return [_block_diag(m, head_lo) for m in ms]

    def chunk_group(p, i, _):
        cs = [i * GDN_GROUP + gi for gi in range(GDN_GROUP)]
        rows = [pl.ds(pl.multiple_of(c * GDN_CHUNK, GDN_CHUNK), GDN_CHUNK) for c in cs]
        qn = [qn_s[p, r, :] for r in rows]
        kn = [kn_s[p, r, :] for r in rows]
        v = [v_s[p, r, :] for r in rows]
        be = [be_s[p, r, :] for r in rows]
        gc = [gc_s[p, r, :] for r in rows]
        kb = [a * b for a, b in zip(kn, be)]
        ap = [_dot_nt(jnp.concatenate([a, b], axis=0).astype(BF16), m) for a, b, m in zip(kb, qn, bd(kn))]
        g_row = [jnp.sum(jnp.where(eye_pair, x, 0.0), axis=0, keepdims=True) for x in gc]
        decay = [jnp.exp(jnp.where(incl, x - y, NEG)) for x, y in zip(gc, g_row)]
        x = [jnp.where(strict, -a[:GDN_CHUNK] * d, 0.0) for a, d in zip(ap, decay)]
        pm = [a[GDN_CHUNK:] * d for a, d in zip(ap, decay)]
        tm = [eye_f + a for a in x]
        xr = [_dot(a.astype(BF16), m) for a, m in zip(x, bd(x))]
        for r in range(1, 6):
            rhs = bd(xr)
            if r < 5:
                y = [_dot(jnp.concatenate([a, b], axis=0).astype(BF16), m) for a, b, m in zip(xr, tm, rhs)]
                xr = [a[:GDN_CHUNK] for a in y]
                tm = [a + b[GDN_CHUNK:] for a, b in zip(tm, y)]
            else:
                tm = [a + _dot(a.astype(BF16), m) for a, m in zip(tm, rhs)]
        eg = [jnp.exp(a) for a in gc]
        vb = bd([a * b for a, b in zip(v, be)])
        kbg = bd([a * b for a, b in zip(kb, eg)])
        uw = [_dot(a.astype(BF16), jnp.concatenate([b, c], axis=1)) for a, b, c in zip(tm, vb, kbg)]
        u_bd = bd([a[:, :LANES] for a in uw])
        w_bd = bd([a[:, LANES:] for a in uw])
        puw = [_dot(a.astype(BF16), jnp.concatenate([b, c], axis=1)) for a, b, c in zip(pm, u_bd, w_bd)]
        g_last = [a[GDN_CHUNK - 1:GDN_CHUNK, :] for a in gc]
        kg = [a * jnp.exp(b - c) for a, b, c in zip(kn, g_last, gc)]
        kuw = [_dot_tn(a.astype(BF16), b.astype(BF16)) for a, b in zip(kg, uw)]
        for c, r, q, e, a, b, gl in zip(cs, rows, qn, eg, puw, kuw, g_last):
            qt_s[p, r, :] = (q * e - a[:, LANES:]).astype(BF16)
            o0_s[p, r, :] = a[:, :LANES]
            nf_s[p, c] = jnp.where(bd_mask, b[:, :LANES], 0.0)
            kw_s[p, c] = jnp.where(bd_mask, b[:, LANES:], 0.0).astype(BF16)
            gl_s[p, c] = jnp.broadcast_to(jnp.exp(gl), (8, LANES))
        return 0

    def per_pair(p, _):
        lax.fori_loop(0, n_chunks // GDN_GROUP, functools.partial(chunk_group, p), 0)
        return 0

    lax.fori_loop(0, npair, per_pair, 0)

    def scan(c, states):
        r0 = pl.multiple_of(c * GDN_CHUNK, GDN_CHUNK)
        rows = pl.ds(r0, GDN_CHUNK)
        sb = [s.astype(BF16) for s in states]
        ks = [_dot(kw_s[p, c], sb[p]) for p in range(npair)]
        for p in range(npair):
            oc_s[p, rows, :] = _dot(qt_s[p, rows, :], sb[p]) + o0_s[p, rows, :]
        return tuple(gl_s[p, c][0:1, :] * states[p] + nf_s[p, c] - ks[p] for p in range(npair))

    lax.fori_loop(0, n_chunks, scan, tuple(jnp.zeros((LANES, LANES), F32) for _ in range(npair)))

    def finish(i, _):
        r0 = pl.multiple_of(i * GDN_ROWS, GDN_ROWS)
        rows = pl.ds(r0, GDN_ROWS)
        for p in range(npair):
            o = oc_s[p, rows, :]
            ms = _head_sums(o * o, ones_bd) * (1.0 / HEAD_DIM)
            y = o * lax.rsqrt(ms + RMS_EPS) * gnw_ref[...]
            zg = pa_ref[0, rows, 3 * A_WIDTH + p * LANES:3 * A_WIDTH + (p + 1) * LANES].astype(F32)
            o_ref[0, rows, p * LANES:(p + 1) * LANES] = (y * (zg * _sigmoid(zg))).astype(BF16)
        return 0

    lax.fori_loop(0, seq // GDN_ROWS, finish, 0)


def _gdn(pa, ba, conv_w, a_log, dt_bias, gdn_norm_w):
    bsz, seq, _ = pa.shape
    npair = N_HEADS_A // PAIR
    n_chunks = seq // GDN_CHUNK
    pad = lambda vec: jnp.zeros((1, LANES), F32).at[0, N_HEADS_A:2 * N_HEADS_A].set(vec.astype(F32))
    gnw = jnp.tile(gdn_norm_w.astype(F32), PAIR)[None, :]
    fixed = lambda shape: pl.BlockSpec(shape, lambda b: (0,) * len(shape))
    per_pair_f32 = pltpu.VMEM((npair, seq, LANES), F32)
    return pl.pallas_call(
        _gdn_kernel,
        grid=(bsz,),
        in_specs=[pl.BlockSpec((1, seq, PA_COLS), lambda b: (b, 0, 0)),
                  pl.BlockSpec((1, seq, BA_COLS), lambda b: (b, 0, 0)),
                  fixed((CONV_WIDTH, 3 * A_WIDTH)), fixed((1, LANES)), fixed((1, LANES)), fixed((1, LANES))],
        out_specs=pl.BlockSpec((1, seq, A_WIDTH), lambda b: (b, 0, 0)),
        out_shape=jax.ShapeDtypeStruct((bsz, seq, A_WIDTH), BF16),
        scratch_shapes=[
            pltpu.VMEM((seq + 8, LANES), F32),
            per_pair_f32, per_pair_f32, per_pair_f32,
            per_pair_f32, per_pair_f32,
            pltpu.VMEM((npair, seq, LANES), BF16),
            per_pair_f32,
            pltpu.VMEM((npair, n_chunks, LANES, LANES), F32),
            pltpu.VMEM((npair, n_chunks, LANES, LANES), BF16),
            pltpu.VMEM((npair, n_chunks, 8, LANES), F32),
            per_pair_f32,
        ],
        compiler_params=pltpu.CompilerParams(dimension_semantics=("arbitrary",), vmem_limit_bytes=VMEM_LIMIT),
        name="gated_delta",
    )(pa, ba, conv_w.astype(F32), pad(a_log), pad(dt_bias), gnw)


ROW_TILE = 512


def kernel(x, norm_w, w_in, conv_w, a_log, dt_bias, gdn_norm_w, q_norm_w, k_norm_w, w_out):
    bsz, seq, _ = x.shape
    assert w_in.shape[1:] == (D_MODEL, IN_COLS)
    shape3 = lambda t: t.reshape(bsz, seq, t.shape[-1])
    flat = lambda t: t.reshape(bsz * seq, t.shape[-1])
    x2d = flat(x)
    w_in = w_in.astype(F32)
    norm_w = norm_w.astype(F32)
    for layer in range(norm_w.shape[0]):
        pa, ba, pb, pc = _in_proj(x2d, norm_w[layer][None, :], w_in, layer, ROW_TILE)
        oa = _gdn(shape3(pa), shape3(ba), conv_w[layer], a_log[layer], dt_bias[layer], gdn_norm_w[layer])
        ob = _stick_breaking(shape3(pb))
        oc = _dilated(shape3(pc), q_norm_w[layer], k_norm_w[layer])
        x2d = _out_proj(x2d, flat(oa), flat(ob), flat(oc), w_out[layer].astype(BF16), ROW_TILE)
    return shape3(x2d)
```

```python
import functools

import jax
import jax.numpy as jnp
from jax import lax
from jax.experimental import pallas as pl
from jax.experimental.pallas import tpu as pltpu

F32 = jnp.float32
BF16 = jnp.bfloat16

D_MODEL = 1024
HEAD_DIM = 64
N_HEADS_A, N_HEADS_B, N_HEADS_C = 6, 4, 6
A_WIDTH, B_WIDTH, C_WIDTH = N_HEADS_A * HEAD_DIM, N_HEADS_B * HEAD_DIM, N_HEADS_C * HEAD_DIM
CONV_WIDTH = 4
GDN_CHUNK = 64
ROPE_DIM = HEAD_DIM // 4
ROPE_THETA = 500000.0
DILATIONS = (1, 4, 16)
RMS_EPS = 1e-6

LANES = 128
PAIR = LANES // HEAD_DIM
NEG = -1e30

PA_COLS = 4 * A_WIDTH
BA_COLS = LANES
PB_COLS = 4 * B_WIDTH
PC_COLS = 4 * C_WIDTH
IN_COLS = PA_COLS + 2 * N_HEADS_A + PB_COLS + PC_COLS
W1_COLS = PA_COLS + BA_COLS + PB_COLS + PC_COLS

VMEM_LIMIT = 56 * 1024 * 1024

LOG2E = 1.4426950408889634
QSCALE2 = HEAD_DIM ** -0.5 * LOG2E


def _dot(a, b):
    return jnp.dot(a, b, preferred_element_type=F32)


def _dot_nt(a, b):
    return lax.dot_general(a, b, (((1,), (1,)), ((), ())), preferred_element_type=F32)


def _dot_tn(a, b):
    return lax.dot_general(a, b, (((0,), (0,)), ((), ())), preferred_element_type=F32)


def _aligned(x, m):
    return x if isinstance(x, int) else pl.multiple_of(x, m)


def _sigmoid(x):
    return 1.0 / (1.0 + jnp.exp(-x))


def _softplus(x):
    return jnp.maximum(x, 0.0) + jnp.log(1.0 + jnp.exp(-jnp.abs(x)))


def _split_bf16(x):
    hi = x.astype(BF16)
    lo = (x - hi.astype(F32)).astype(BF16)
    return hi, lo


def _head_lo(shape):
    return lax.broadcasted_iota(jnp.int32, shape, len(shape) - 1) < HEAD_DIM


def _block_diag(m, head_lo):
    z = jnp.zeros_like(m)
    return jnp.concatenate([jnp.where(head_lo, m, z), jnp.where(head_lo, z, m)], axis=0).astype(BF16)


def _head_sums(x2, ones_bd):
    return _dot(x2.astype(BF16), ones_bd)


def _ones_bd():
    r = lax.broadcasted_iota(jnp.int32, (LANES, LANES), 0) // HEAD_DIM
    c = lax.broadcasted_iota(jnp.int32, (LANES, LANES), 1) // HEAD_DIM
    return (r == c).astype(BF16)


W_ROWS = 128


def _in_proj_kernel(x_ref, nw_ref, w_ref, pa_ref, ba_ref, pb_ref, pc_ref, wb):
    @pl.when(pl.program_id(0) == 0)
    def _():
        def rows(i, _):
            r = pl.ds(pl.multiple_of(i * W_ROWS, W_ROWS), W_ROWS)
            wb[r, 0:PA_COLS + BA_COLS] = w_ref[0, r, 0:PA_COLS + BA_COLS].astype(BF16)
            wb[r, PA_COLS + BA_COLS:W1_COLS] = w_ref[0, r, PA_COLS + 2 * N_HEADS_A:IN_COLS].astype(BF16)
            return 0

        lax.fori_loop(0, D_MODEL // W_ROWS, rows, 0)

    x = x_ref[...]
    ms = jnp.mean(x * x, axis=-1, keepdims=True)
    h = (x * lax.rsqrt(ms + RMS_EPS) * nw_ref[...]).astype(BF16)
    c0 = 0
    pa_ref[...] = _dot(h, wb[:, c0:c0 + PA_COLS]).astype(BF16)
    c0 += PA_COLS
    ba_ref[...] = _dot(h, wb[:, c0:c0 + BA_COLS])
    c0 += BA_COLS
    col_b = lax.broadcasted_iota(jnp.int32, (1, PB_COLS), 1)
    pb_ref[...] = (_dot(h, wb[:, c0:c0 + PB_COLS]) * jnp.where(col_b < B_WIDTH, QSCALE2, 1.0)).astype(BF16)
    c0 += PB_COLS
    pc_ref[...] = _dot(h, wb[:, c0:c0 + PC_COLS]).astype(BF16)


def _in_proj(x2d, norm_w, w_in, layer, tm):
    n = x2d.shape[0]
    row = lambda i: (i, 0)
    return pl.pallas_call(
        _in_proj_kernel,
        grid=(n // tm,),
        in_specs=[pl.BlockSpec((tm, D_MODEL), row),
                  pl.BlockSpec((1, D_MODEL), lambda i: (0, 0)),
                  pl.BlockSpec((1, D_MODEL, IN_COLS), lambda i: (layer, 0, 0), pipeline_mode=pl.Buffered(1))],
        out_specs=[pl.BlockSpec((tm, PA_COLS), row), pl.BlockSpec((tm, BA_COLS), row),
                   pl.BlockSpec((tm, PB_COLS), row), pl.BlockSpec((tm, PC_COLS), row)],
        out_shape=[jax.ShapeDtypeStruct((n, PA_COLS), BF16), jax.ShapeDtypeStruct((n, BA_COLS), F32),
                   jax.ShapeDtypeStruct((n, PB_COLS), BF16), jax.ShapeDtypeStruct((n, PC_COLS), BF16)],
        scratch_shapes=[pltpu.VMEM((D_MODEL, W1_COLS), BF16)],
        compiler_params=pltpu.CompilerParams(dimension_semantics=("arbitrary",), vmem_limit_bytes=VMEM_LIMIT),
        name="in_proj",
    )(x2d, norm_w, w_in)


def _out_proj_kernel(x_ref, oa_ref, ob_ref, oc_ref, w_ref, o_ref):
    acc = _dot(oa_ref[...], w_ref[0:A_WIDTH, :])
    acc = acc + _dot(ob_ref[...], w_ref[A_WIDTH:A_WIDTH + B_WIDTH, :])
    acc = acc + _dot(oc_ref[...], w_ref[A_WIDTH + B_WIDTH:, :])
    o_ref[...] = x_ref[...] + acc


def _out_proj(x2d, oa, ob, oc, w_out, tm):
    n = x2d.shape[0]
    row = lambda i: (i, 0)
    fixed = lambda i: (0, 0)
    return pl.pallas_call(
        _out_proj_kernel,
        grid=(n // tm,),
        in_specs=[pl.BlockSpec((tm, D_MODEL), row), pl.BlockSpec((tm, A_WIDTH), row),
                  pl.BlockSpec((tm, B_WIDTH), row), pl.BlockSpec((tm, C_WIDTH), row),
                  pl.BlockSpec((D_MODEL, D_MODEL), fixed)],
        out_specs=pl.BlockSpec((tm, D_MODEL), row),
        out_shape=jax.ShapeDtypeStruct((n, D_MODEL), F32),
        compiler_params=pltpu.CompilerParams(dimension_semantics=("arbitrary",), vmem_limit_bytes=VMEM_LIMIT),
        name="out_proj",
    )(x2d, oa, ob, oc, w_out)


SB_BLOCK = 256


def _sb_kernel(q_ref, k_ref, v_ref, z_ref, o_ref):
    seq = q_ref.shape[1]
    nb = seq // SB_BLOCK
    head_lo = _head_lo((1, LANES))
    row = lax.broadcasted_iota(jnp.int32, (SB_BLOCK, SB_BLOCK), 0)
    col = lax.broadcasted_iota(jnp.int32, (SB_BLOCK, SB_BLOCK), 1)
    earlier = col < row
    jj = lax.broadcasted_iota(jnp.int32, (2 * SB_BLOCK, SB_BLOCK), 0) % SB_BLOCK
    ss = lax.broadcasted_iota(jnp.int32, (2 * SB_BLOCK, SB_BLOCK), 1)
    neg_suffix = jnp.where(jj >= ss, -1.0, 0.0).astype(BF16)

    def rows(b):
        return pl.ds(b * SB_BLOCK, SB_BLOCK)

    def scores(i, kt):
        q = q_ref[0, rows(i), :]
        zero = jnp.zeros_like(q)
        k = k_ref[0, rows(kt), :]
        return [_dot_nt(qh, k) for qh in (jnp.where(head_lo, q, zero), jnp.where(head_lo, zero, q))]

    def log_weights(z2, diag):
        nlk = [jnp.maximum(x, 0.0) + jnp.log2(1.0 + jnp.exp2(-jnp.abs(x))) for x in z2]
        if diag:
            nlk = [jnp.where(earlier, x, 0.0) for x in nlk]
        parts = [_split_bf16(x) for x in nlk]
        here_on = [_dot(jnp.concatenate([hi, lo], axis=1), neg_suffix) for hi, lo in parts]
        pre = [x + w for x, w in zip(z2, here_on)]
        if diag:
            pre = [jnp.where(earlier, x, NEG) for x in pre]
        return pre, [w[:, 0:1] for w in here_on]

    def accumulate(pre, carry, acc, kt):
        v = v_ref[0, rows(kt), :]
        w = [jnp.exp2(x + c).astype(BF16) for x, c in zip(pre, carry)]
        return [a + _dot(x, v) for a, x in zip(acc, w)]

    tiles = [(i, kt) for i in range(nb) for kt in range(i, -1, -1)]
    z2, lw = {}, {}
    carry = acc = None
    for step in range(len(tiles) + 2):
        if step < len(tiles):
            z2[step] = scores(*tiles[step])
        if 0 <= step - 1 < len(tiles):
            i, kt = tiles[step - 1]
            lw[step - 1] = log_weights(z2.pop(step - 1), i == kt)
        if 0 <= step - 2 < len(tiles):
            i, kt = tiles[step - 2]
            pre, tot = lw.pop(step - 2)
            if i == kt:
                carry = [jnp.zeros((SB_BLOCK, 1), F32)] * PAIR
                acc = [jnp.zeros((SB_BLOCK, LANES), F32)] * PAIR
            acc = accumulate(pre, carry, acc, kt)
            carry = [c + t for c, t in zip(carry, tot)]
            if kt == 0:
                o = jnp.where(head_lo, acc[0], acc[1])
                zg = z_ref[0, rows(i), :].astype(F32)
                o_ref[0, rows(i), :] = (o * (zg * _sigmoid(zg))).astype(BF16)


def _stick_breaking(pb):
    bsz, seq, _ = pb.shape
    npair = N_HEADS_B // PAIR
    spec = lambda g: pl.BlockSpec((1, seq, LANES), lambda b, p, g=g: (b, 0, g * npair + p))
    return pl.pallas_call(
        _sb_kernel,
        grid=(bsz, npair),
        in_specs=[spec(0), spec(1), spec(2), spec(3)],
        out_specs=pl.BlockSpec((1, seq, LANES), lambda b, p: (b, 0, p)),
        out_shape=jax.ShapeDtypeStruct((bsz, seq, B_WIDTH), BF16),
        compiler_params=pltpu.CompilerParams(dimension_semantics=("arbitrary", "arbitrary"),
                                             vmem_limit_bytes=VMEM_LIMIT),
        name="stick_breaking",
    )(pb, pb, pb, pb)


DW_BLOCK = 128
DW_GROUP = {16: 8, 4: 4, 1: 5}


def _dw_kernel(q_ref, k_ref, v_ref, z_ref, qw_ref, kw_ref, cos_ref, sina_ref, sinb_ref, o_ref,
               qf, kf, vf, qn, kn, qg, kg, vg, m1_st, m2_st, l_st, acc_st):
    seq = q_ref.shape[1]
    head_lo = _head_lo((1, LANES))
    ones_bd = _ones_bd()
    n_rows = 256

    def prep(i, _):
        r0 = pl.multiple_of(i * n_rows, n_rows)
        rows = pl.ds(r0, n_rows)
        cos, sina, sinb = cos_ref[rows, :], sina_ref[rows, :], sinb_ref[rows, :]

        def norm_rope(x, w):
            ms = _head_sums(x * x, ones_bd) * (1.0 / HEAD_DIM)
            y = x * lax.rsqrt(ms + RMS_EPS) * w
            return y * cos + pltpu.roll(y, LANES - ROPE_DIM // 2, 1) * sina + pltpu.roll(y, ROPE_DIM // 2, 1) * sinb

        q = norm_rope(q_ref[0, rows, :].astype(F32), qw_ref[...]) * QSCALE2
        k = norm_rope(k_ref[0, rows, :].astype(F32), kw_ref[...])
        qf[rows, :] = q
        kf[rows, :] = k
        qn[rows, :] = q.astype(BF16)
        kn[rows, :] = k.astype(BF16)
        vf[rows, :] = v_ref[0, rows, :].astype(F32)
        return 0

    lax.fori_loop(0, seq // n_rows, prep, 0)

    qi = lax.broadcasted_iota(jnp.int32, (DW_BLOCK, DW_BLOCK), 0)
    kj = lax.broadcasted_iota(jnp.int32, (DW_BLOCK, DW_BLOCK), 1)
    own_ok = kj <= qi
    prev_ok = kj >= qi
    both_ok = jnp.concatenate([prev_ok, own_ok], axis=1)

    def pair_of(a, b):
        return jnp.where(head_lo, a, b)

    def blocks(items, src, has_prev, first, last):
        q_src, k_src, v_of = src
        width = 2 if has_prev else 1
        ok = both_ok if has_prev else own_ok
        ones = jnp.ones((width * DW_BLOCK, LANES), BF16)
        q = [q_src[pl.ds(c0, DW_BLOCK), :] for c0, _ in items]
        keys = [pl.ds(c0 - DW_BLOCK, 2 * DW_BLOCK) if has_prev else pl.ds(c0, DW_BLOCK) for c0, _ in items]
        k = [k_src[ks, :] for ks in keys]
        s = [[jnp.where(ok, _dot_nt(qh, kk), NEG)
              for qh in (jnp.where(head_lo, x, jnp.zeros_like(x)), jnp.where(head_lo, jnp.zeros_like(x), x))]
             for x, kk in zip(q, k)]
        m_blk = [[jnp.max(sh, axis=1, keepdims=True) for sh in sb] for sb in s]
        if first:
            m_h = [[jnp.broadcast_to(mh, (DW_BLOCK, LANES)) for mh in mb] for mb in m_blk]
        else:
            m_old = [(m1_st[nat, :], m2_st[nat, :]) for _, nat in items]
            m_h = [[jnp.maximum(o, b) for o, b in zip(oh, mb)] for oh, mb in zip(m_old, m_blk)]
            alpha = [jnp.exp2(pair_of(o[0], o[1]) - pair_of(n[0], n[1])) for o, n in zip(m_old, m_h)]
        p = [[jnp.exp2(sh - jnp.concatenate([mh] * width, axis=1)).astype(BF16) for sh, mh in zip(sb, mb)]
             for sb, mb in zip(s, m_h)]
        vo = [jnp.concatenate([v_of(ks), ones], axis=1) for ks in keys]
        pvl = [[_dot(ph, vv) for ph in pb] for pb, vv in zip(p, vo)]
        pv = [pair_of(a[:, :LANES], b[:, :LANES]) for a, b in pvl]
        l_blk = [pair_of(a[:, LANES:], b[:, LANES:]) for a, b in pvl]
        if first:
            l_new, acc_new = l_blk, pv
        else:
            l_new = [a * l_st[nat, :] + x for a, (_, nat), x in zip(alpha, items, l_blk)]
            acc_new = [a * acc_st[nat, :] + x for a, (_, nat), x in zip(alpha, items, pv)]
        for (_, nat), mh, l, acc in zip(items, m_h, l_new, acc_new):
            if last:
                zg = z_ref[0, nat, :].astype(F32)
                o_ref[0, nat, :] = (acc / l * (zg * _sigmoid(zg))).astype(BF16)
            else:
                m1_st[nat, :] = mh[0]
                m2_st[nat, :] = mh[1]
                l_st[nat, :] = l
                acc_st[nat, :] = acc

    order = tuple(reversed(DILATIONS))
    for d in order:
        first, last = d == order[0], d == order[-1]
        cls_len = seq // d
        nblk = cls_len // DW_BLOCK
        grp = DW_GROUP[d]
        if d > 1:
            for r in range(d):
                src_rows = pl.ds(r, cls_len, stride=d)
                dst = pl.ds(r * cls_len, cls_len)
                qg[dst, :] = qf[src_rows, :].astype(BF16)
                kg[dst, :] = kf[src_rows, :].astype(BF16)
                vg[dst, :] = vf[src_rows, :].astype(BF16)
            src = (qg, kg, lambda ks: vg[ks, :])
        else:
            src = (qn, kn, lambda ks: v_ref[0, ks, :])

        def item(r, j, d=d, cls_len=cls_len):
            c0 = _aligned(r * cls_len + j * DW_BLOCK, DW_BLOCK)
            start = r + j * (DW_BLOCK * d)
            nat = pl.ds(start, DW_BLOCK, stride=d) if d > 1 else pl.ds(_aligned(start, DW_BLOCK), DW_BLOCK)
            return c0, nat

        if nblk == 1:
            def body(i, _, grp=grp, item=item, src=src, first=first, last=last):
                blocks([item(i * grp + g, 0) for g in range(grp)], src, False, first, last)
                return 0

            lax.fori_loop(0, d // grp, body, 0)
        elif d > 1:
            blocks([item(r, 0) for r in range(d)], src, False, first, last)

            def body(j, _, d=d, item=item, src=src, first=first, last=last):
                blocks([item(r, j) for r in range(d)], src, True, first, last)
                return 0

            lax.fori_loop(1, nblk, body, 0)
        else:
            blocks([item(0, 0)], src, False, first, last)

            def body(i, _, grp=grp, item=item, src=src, first=first, last=last):
                blocks([item(0, 1 + i * grp + g) for g in range(grp)], src, True, first, last)
                return 0

            lax.fori_loop(0, (nblk - 1) // grp, body, 0)


def _rope_tables(seq):
    half = ROPE_DIM // 2
    inv_freq = ROPE_THETA ** (-jnp.arange(half, dtype=F32) / half)
    ang = jnp.arange(seq, dtype=jnp.int32).astype(F32)[:, None] * inv_freq[None, :]
    cos, sin = jnp.cos(ang), jnp.sin(ang)
    ones = jnp.ones((seq, HEAD_DIM - ROPE_DIM), F32)
    zeros_h = jnp.zeros((seq, half), F32)
    zeros_t = jnp.zeros((seq, HEAD_DIM - ROPE_DIM), F32)
    cos_h = jnp.concatenate([cos, cos, ones], axis=1)
    sina_h = jnp.concatenate([-sin, zeros_h, zeros_t], axis=1)
    sinb_h = jnp.concatenate([zeros_h, sin, zeros_t], axis=1)
    tile = lambda t: jnp.tile(t, (1, PAIR))
    return tile(cos_h), tile(sina_h), tile(sinb_h)


def _dilated(pc, q_norm_w, k_norm_w):
    bsz, seq, _ = pc.shape
    npair = N_HEADS_C // PAIR
    assert (seq // DW_BLOCK - 1) % DW_GROUP[1] == 0 and 16 % DW_GROUP[16] == 0
    cos, sina, sinb = _rope_tables(seq)
    qw = jnp.tile(q_norm_w.astype(F32), PAIR)[None, :]
    kw = jnp.tile(k_norm_w.astype(F32), PAIR)[None, :]
    spec = lambda g: pl.BlockSpec((1, seq, LANES), lambda b, p, g=g: (b, 0, g * npair + p))
    fixed = lambda shape: pl.BlockSpec(shape, lambda b, p: (0, 0))
    f32_rows, bf16_rows = pltpu.VMEM((seq, LANES), F32), pltpu.VMEM((seq, LANES), BF16)
    return pl.pallas_call(
        _dw_kernel,
        grid=(bsz, npair),
        in_specs=[spec(0), spec(1), spec(2), spec(3), fixed((1, LANES)), fixed((1, LANES)),
                  fixed((seq, LANES)), fixed((seq, LANES)), fixed((seq, LANES))],
        out_specs=pl.BlockSpec((1, seq, LANES), lambda b, p: (b, 0, p)),
        out_shape=jax.ShapeDtypeStruct((bsz, seq, C_WIDTH), BF16),
        scratch_shapes=[f32_rows] * 3 + [bf16_rows] * 5 + [f32_rows] * 4,
        compiler_params=pltpu.CompilerParams(dimension_semantics=("arbitrary", "arbitrary"),
                                             vmem_limit_bytes=VMEM_LIMIT),
        name="dilated_window",
    )(pc, pc, pc, pc, qw, kw, cos, sina, sinb)


GDN_ROWS = 256
GDN_HALO = 16
GDN_GROUP = 16


def _gdn_kernel(pa_ref, ba_ref, cw_ref, alog_ref, dtb_ref, gnw_ref, o_ref,
                qn_s, kn_s, v_s, be_s, gc_s, qt_s, o0_s, nf_s, kw_s, gl_s, oc_s):
    seq = pa_ref.shape[1]
    n_chunks = seq // GDN_CHUNK
    npair = N_HEADS_A // PAIR
    lane = lax.broadcasted_iota(jnp.int32, (1, LANES), 1)
    head_lo = lane < HEAD_DIM
    ones_bd = _ones_bd()
    bd_mask = (lax.broadcasted_iota(jnp.int32, (LANES, LANES), 0) // HEAD_DIM
               == lax.broadcasted_iota(jnp.int32, (LANES, LANES), 1) // HEAD_DIM)

    ri = lax.broadcasted_iota(jnp.int32, (GDN_ROWS, 2 * GDN_ROWS), 0)
    ci = lax.broadcasted_iota(jnp.int32, (GDN_ROWS, 2 * GDN_ROWS), 1) % GDN_ROWS
    cum_mat = ((ri // GDN_CHUNK == ci // GDN_CHUNK) & (ci <= ri)).astype(BF16)

    n_shift = CONV_WIDTH - 1
    sr = lax.broadcasted_iota(jnp.int32, (n_shift * GDN_ROWS, GDN_ROWS), 0)
    sc = lax.broadcasted_iota(jnp.int32, (n_shift * GDN_ROWS, GDN_ROWS), 1)
    shift_mat = (sc == sr % GDN_ROWS - (CONV_WIDTH - 1) + sr // GDN_ROWS).astype(BF16)
    hr = lax.broadcasted_iota(jnp.int32, (n_shift * 8, GDN_HALO), 0)
    hc = lax.broadcasted_iota(jnp.int32, (n_shift * 8, GDN_HALO), 1)
    halo_mat = (hc == GDN_HALO + hr % 8 - (CONV_WIDTH - 1) + hr // 8).astype(BF16)

    ii = lax.broadcasted_iota(jnp.int32, (GDN_CHUNK, LANES), 0)
    jl = lax.broadcasted_iota(jnp.int32, (GDN_CHUNK, LANES), 1) % HEAD_DIM
    incl = ii >= jl
    strict = ii > jl
    eye_pair = ii == jl
    eye_f = eye_pair.astype(F32)

    er = lax.broadcasted_iota(jnp.int32, (2 * LANES, 2 * npair * LANES), 0) % LANES
    ec = lax.broadcasted_iota(jnp.int32, (2 * LANES, 2 * npair * LANES), 1)
    want = (ec // LANES % 2) * N_HEADS_A + 2 * (ec // (2 * LANES)) + ec % LANES // HEAD_DIM
    expand = (er == want).astype(BF16)
    rq = lax.broadcasted_iota(jnp.int32, (2 * LANES, 2 * LANES), 0) // HEAD_DIM
    cq = lax.broadcasted_iota(jnp.int32, (2 * LANES, 2 * LANES), 1) // HEAD_DIM
    ones_bd4 = (rq == cq).astype(BF16)

    def prep(i, _):
        r0 = pl.multiple_of(i * GDN_ROWS, GDN_ROWS)
        rows = pl.ds(r0, GDN_ROWS)
        halo_rows = pl.ds(pl.multiple_of(jnp.maximum(r0 - GDN_HALO, 0), GDN_HALO), GDN_HALO)
        conv_cols = slice(0, 3 * A_WIDTH)
        halo = pa_ref[0, halo_rows, conv_cols]
        halo = jnp.where(i > 0, halo, jnp.zeros_like(halo))
        taps = _dot(shift_mat, pa_ref[0, rows, conv_cols])
        edge = _dot(halo_mat, halo)

        ba = ba_ref[0, rows, :]
        g = -jnp.exp(alog_ref[...]) * _softplus(ba + dtb_ref[...])
        g = jnp.where((lane >= N_HEADS_A) & (lane < 2 * N_HEADS_A), g, 0.0)
        hi, lo = _split_bf16(g)
        gc = _dot(cum_mat, jnp.concatenate([hi, lo], axis=0))
        nar = jnp.where(lane < N_HEADS_A, _sigmoid(ba), gc)
        hi, lo = _split_bf16(nar)
        wide = _dot(jnp.concatenate([hi, lo], axis=1), expand)

        def conv_silu(c):
            w = cw_ref[:, c]
            y = w[CONV_WIDTH - 1:CONV_WIDTH, :] * pa_ref[0, rows, c].astype(F32)
            y0 = 0.0
            for kk in range(CONV_WIDTH - 1):
                y = y + w[kk:kk + 1, :] * taps[kk * GDN_ROWS:(kk + 1) * GDN_ROWS, c]
                y0 = y0 + w[kk:kk + 1, :] * edge[kk * 8:(kk + 1) * 8, c]
            y = jnp.concatenate([y[:8, :] + y0, y[8:, :]], axis=0)
            return y * _sigmoid(y)

        ys = [[conv_silu(slice(t * A_WIDTH + p * LANES, t * A_WIDTH + (p + 1) * LANES)) for t in range(3)]
              for p in range(npair)]
        ss = [_dot(jnp.concatenate([y[0] * y[0], y[1] * y[1]], axis=1).astype(BF16), ones_bd4) for y in ys]
        for p in range(npair):
            qn_s[p, rows, :] = ys[p][0] * lax.rsqrt(ss[p][:, :LANES] + RMS_EPS) * (HEAD_DIM ** -0.5)
            kn_s[p, rows, :] = ys[p][1] * lax.rsqrt(ss[p][:, LANES:] + RMS_EPS)
            v_s[p, rows, :] = ys[p][2]
            be_s[p, rows, :] = wide[:, 2 * p * LANES:(2 * p + 1) * LANES]
            gc_s[p, rows, :] = wide[:, (2 * p + 1) * LANES:(2 * p + 2) * LANES]
        return 0

    lax.fori_loop(0, seq // GDN_ROWS, prep, 0)

    def bd(ms):
        return [_block_diag(m, head_lo) for m in ms]

    def chunk_group(p, i, _):
        cs = [i * GDN_GROUP + gi for gi in range(GDN_GROUP)]
        rows = [pl.ds(pl.multiple_of(c * GDN_CHUNK, GDN_CHUNK), GDN_CHUNK) for c in cs]
        qn = [qn_s[p, r, :] for r in rows]
        kn = [kn_s[p, r, :] for r in rows]
        v = [v_s[p, r, :] for r in rows]
        be = [be_s[p, r, :] for r in rows]
        gc = [gc_s[p, r, :] for r in rows]
        kb = [a * b for a, b in zip(kn, be)]
        ap = [_dot_nt(jnp.concatenate([a, b], axis=0).astype(BF16), m) for a, b, m in zip(kb, qn, bd(kn))]
        g_row = [jnp.sum(jnp.where(eye_pair, x, 0.0), axis=0, keepdims=True) for x in gc]
        decay = [jnp.exp(jnp.where(incl, x - y, NEG)) for x, y in zip(gc, g_row)]
        x = [jnp.where(strict, -a[:GDN_CHUNK] * d, 0.0) for a, d in zip(ap, decay)]
        pm = [a[GDN_CHUNK:] * d for a, d in zip(ap, decay)]
        tm = [eye_f + a for a in x]
        xr = [_dot(a.astype(BF16), m) for a, m in zip(x, bd(x))]
        for r in range(1, 6):
            rhs = bd(xr)
            if r < 5:
                y = [_dot(jnp.concatenate([a, b], axis=0).astype(BF16), m) for a, b, m in zip(xr, tm, rhs)]
                xr = [a[:GDN_CHUNK] for a in y]
                tm = [a + b[GDN_CHUNK:] for a, b in zip(tm, y)]
            else:
                tm = [a + _dot(a.astype(BF16), m) for a, m in zip(tm, rhs)]
        eg = [jnp.exp(a) for a in gc]
        vb = bd([a * b for a, b in zip(v, be)])
        kbg = bd([a * b for a, b in zip(kb, eg)])
        uw = [_dot(a.astype(BF16), jnp.concatenate([b, c], axis=1)) for a, b, c in zip(tm, vb, kbg)]
        u_bd = bd([a[:, :LANES] for a in uw])
        w_bd = bd([a[:, LANES:] for a in uw])
        puw = [_dot(a.astype(BF16), jnp.concatenate([b, c], axis=1)) for a, b, c in zip(pm, u_bd, w_bd)]
        g_last = [a[GDN_CHUNK - 1:GDN_CHUNK, :] for a in gc]
        kg = [a * jnp.exp(b - c) for a, b, c in zip(kn, g_last, gc)]
        kuw = [_dot_tn(a.astype(BF16), b.astype(BF16)) for a, b in zip(kg, uw)]
        for c, r, q, e, a, b, gl in zip(cs, rows, qn, eg, puw, kuw, g_last):
            qt_s[p, r, :] = (q * e - a[:, LANES:]).astype(BF16)
            o0_s[p, r, :] = a[:, :LANES]
            nf_s[p, c] = jnp.where(bd_mask, b[:, :LANES], 0.0)
            kw_s[p, c] = jnp.where(bd_mask, b[:, LANES:], 0.0).astype(BF16)
            gl_s[p, c] = jnp.broadcast_to(jnp.exp(gl), (8, LANES))
        return 0

    def per_pair(p, _):
        lax.fori_loop(0, n_chunks // GDN_GROUP, functools.partial(chunk_group, p), 0)
        return 0

    lax.fori_loop(0, npair, per_pair, 0)

    def scan(c, states):
        r0 = pl.multiple_of(c * GDN_CHUNK, GDN_CHUNK)
        rows = pl.ds(r0, GDN_CHUNK)
        sb = [s.astype(BF16) for s in states]
        ks = [_dot(kw_s[p, c], sb[p]) for p in range(npair)]
        for p in range(npair):
            oc_s[p, rows, :] = _dot(qt_s[p, rows, :], sb[p]) + o0_s[p, rows, :]
        return tuple(gl_s[p, c][0:1, :] * states[p] + nf_s[p, c] - ks[p] for p in range(npair))

    lax.fori_loop(0, n_chunks, scan, tuple(jnp.zeros((LANES, LANES), F32) for _ in range(npair)))

    def finish(i, _):
        r0 = pl.multiple_of(i * GDN_ROWS, GDN_ROWS)
        rows = pl.ds(r0, GDN_ROWS)
        for p in range(npair):
            o = oc_s[p, rows, :]
            ms = _head_sums(o * o, ones_bd) * (1.0 / HEAD_DIM)
            y = o * lax.rsqrt(ms + RMS_EPS) * gnw_ref[...]
            zg = pa_ref[0, rows, 3 * A_WIDTH + p * LANES:3 * A_WIDTH + (p + 1) * LANES].astype(F32)
            o_ref[0, rows, p * LANES:(p + 1) * LANES] = (y * (zg * _sigmoid(zg))).astype(BF16)
        return 0

    lax.fori_loop(0, seq // GDN_ROWS, finish, 0)


def _gdn(pa, ba, conv_w, a_log, dt_bias, gdn_norm_w):
    bsz, seq, _ = pa.shape
    npair = N_HEADS_A // PAIR
    n_chunks = seq // GDN_CHUNK
    pad = lambda vec: jnp.zeros((1, LANES), F32).at[0, N_HEADS_A:2 * N_HEADS_A].set(vec.astype(F32))
    gnw = jnp.tile(gdn_norm_w.astype(F32), PAIR)[None, :]
    fixed = lambda shape: pl.BlockSpec(shape, lambda b: (0,) * len(shape))
    per_pair_f32 = pltpu.VMEM((npair, seq, LANES), F32)
    return pl.pallas_call(
        _gdn_kernel,
        grid=(bsz,),
        in_specs=[pl.BlockSpec((1, seq, PA_COLS), lambda b: (b, 0, 0)),
                  pl.BlockSpec((1, seq, BA_COLS), lambda b: (b, 0, 0)),
                  fixed((CONV_WIDTH, 3 * A_WIDTH)), fixed((1, LANES)), fixed((1, LANES)), fixed((1, LANES))],
        out_specs=pl.BlockSpec((1, seq, A_WIDTH), lambda b: (b, 0, 0)),
        out_shape=jax.ShapeDtypeStruct((bsz, seq, A_WIDTH), BF16),
        scratch_shapes=[
            per_pair_f32, per_pair_f32, per_pair_f32,
            per_pair_f32, per_pair_f32,
            pltpu.VMEM((npair, seq, LANES), BF16),
            per_pair_f32,
            pltpu.VMEM((npair, n_chunks, LANES, LANES), F32),
            pltpu.VMEM((npair, n_chunks, LANES, LANES), BF16),
            pltpu.VMEM((npair, n_chunks, 8, LANES), F32),
            per_pair_f32,
        ],
        compiler_params=pltpu.CompilerParams(dimension_semantics=("arbitrary",), vmem_limit_bytes=VMEM_LIMIT),
        name="gated_delta",
    )(pa, ba, conv_w.astype(F32), pad(a_log), pad(dt_bias), gnw)


ROW_TILE = 512


def kernel(x, norm_w, w_in, conv_w, a_log, dt_bias, gdn_norm_w, q_norm_w, k_norm_w, w_out):
    bsz, seq, _ = x.shape
    assert w_in.shape[1:] == (D_MODEL, IN_COLS)
    shape3 = lambda t: t.reshape(bsz, seq, t.shape[-1])
    flat = lambda t: t.reshape(bsz * seq, t.shape[-1])
    x2d = flat(x)
    w_in = w_in.astype(F32)
    norm_w = norm_w.astype(F32)
    for layer in range(norm_w.shape[0]):
        pa, ba, pb, pc = _in_proj(x2d, norm_w[layer][None, :], w_in, layer, ROW_TILE)
        oa = _gdn(shape3(pa), shape3(ba), conv_w[layer], a_log[layer], dt_bias[layer], gdn_norm_w[layer])
        ob = _stick_breaking(shape3(pb))
        oc = _dilated(shape3(pc), q_norm_w[layer], k_norm_w[layer])
        x2d = _out_proj(x2d, flat(oa), flat(ob), flat(oc), w_out[layer].astype(BF16), ROW_TILE)
    return shape3(x2d)
```

```python
import functools

import jax
import jax.numpy as jnp
from jax import lax
from jax.experimental import pallas as pl
from jax.experimental.pallas import tpu as pltpu

F32 = jnp.float32
BF16 = jnp.bfloat16

D_MODEL = 1024
HEAD_DIM = 64
N_HEADS_A, N_HEADS_B, N_HEADS_C = 6, 4, 6
A_WIDTH, B_WIDTH, C_WIDTH = N_HEADS_A * HEAD_DIM, N_HEADS_B * HEAD_DIM, N_HEADS_C * HEAD_DIM
CONV_WIDTH = 4
GDN_CHUNK = 64
ROPE_DIM = HEAD_DIM // 4
ROPE_THETA = 500000.0
DILATIONS = (1, 4, 16)
RMS_EPS = 1e-6

LANES = 128
PAIR = LANES // HEAD_DIM
NEG = -1e30

PA_COLS = 4 * A_WIDTH
BA_COLS = LANES
PB_COLS = 4 * B_WIDTH
PC_COLS = 4 * C_WIDTH
IN_COLS = PA_COLS + 2 * N_HEADS_A + PB_COLS + PC_COLS
W1_COLS = PA_COLS + BA_COLS + PB_COLS + PC_COLS

VMEM_LIMIT = 56 * 1024 * 1024

LOG2E = 1.4426950408889634
QSCALE2 = HEAD_DIM ** -0.5 * LOG2E


def _dot(a, b):
    return jnp.dot(a, b, preferred_element_type=F32)


def _dot_nt(a, b):
    return lax.dot_general(a, b, (((1,), (1,)), ((), ())), preferred_element_type=F32)


def _dot_tn(a, b):
    return lax.dot_general(a, b, (((0,), (0,)), ((), ())), preferred_element_type=F32)


def _aligned(x, m):
    return x if isinstance(x, int) else pl.multiple_of(x, m)


def _sigmoid(x):
    return 1.0 / (1.0 + jnp.exp(-x))


def _softplus(x):
    return jnp.maximum(x, 0.0) + jnp.log(1.0 + jnp.exp(-jnp.abs(x)))


def _split_bf16(x):
    hi = x.astype(BF16)
    lo = (x - hi.astype(F32)).astype(BF16)
    return hi, lo


def _head_lo(shape):
    return lax.broadcasted_iota(jnp.int32, shape, len(shape) - 1) < HEAD_DIM


def _block_diag(m, head_lo):
    z = jnp.zeros_like(m)
    return jnp.concatenate([jnp.where(head_lo, m, z), jnp.where(head_lo, z, m)], axis=0).astype(BF16)


def _head_sums(x2, ones_bd):
    return _dot(x2.astype(BF16), ones_bd)


def _ones_bd():
    r = lax.broadcasted_iota(jnp.int32, (LANES, LANES), 0) // HEAD_DIM
    c = lax.broadcasted_iota(jnp.int32, (LANES, LANES), 1) // HEAD_DIM
    return (r == c).astype(BF16)


W_ROWS = 128


def _in_proj_kernel(x_ref, nw_ref, w_ref, pa_ref, ba_ref, pb_ref, pc_ref, wb):
    @pl.when(pl.program_id(0) == 0)
    def _():
        def rows(i, _):
            r = pl.ds(pl.multiple_of(i * W_ROWS, W_ROWS), W_ROWS)
            wb[r, 0:PA_COLS + BA_COLS] = w_ref[0, r, 0:PA_COLS + BA_COLS].astype(BF16)
            wb[r, PA_COLS + BA_COLS:W1_COLS] = w_ref[0, r, PA_COLS + 2 * N_HEADS_A:IN_COLS].astype(BF16)
            return 0

        lax.fori_loop(0, D_MODEL // W_ROWS, rows, 0)

    x = x_ref[...]
    ms = jnp.mean(x * x, axis=-1, keepdims=True)
    h = (x * lax.rsqrt(ms + RMS_EPS) * nw_ref[...]).astype(BF16)
    c0 = 0
    pa_ref[...] = _dot(h, wb[:, c0:c0 + PA_COLS]).astype(BF16)
    c0 += PA_COLS
    ba_ref[...] = _dot(h, wb[:, c0:c0 + BA_COLS])
    c0 += BA_COLS
    col_b = lax.broadcasted_iota(jnp.int32, (1, PB_COLS), 1)
    pb_ref[...] = (_dot(h, wb[:, c0:c0 + PB_COLS]) * jnp.where(col_b < B_WIDTH, QSCALE2, 1.0)).astype(BF16)
    c0 += PB_COLS
    pc_ref[...] = _dot(h, wb[:, c0:c0 + PC_COLS]).astype(BF16)


def _in_proj(x2d, norm_w, w_in, layer, tm):
    n = x2d.shape[0]
    row = lambda i: (i, 0)
    return pl.pallas_call(
        _in_proj_kernel,
        grid=(n // tm,),
        in_specs=[pl.BlockSpec((tm, D_MODEL), row),
                  pl.BlockSpec((1, D_MODEL), lambda i: (0, 0)),
                  pl.BlockSpec((1, D_MODEL, IN_COLS), lambda i: (layer, 0, 0), pipeline_mode=pl.Buffered(1))],
        out_specs=[pl.BlockSpec((tm, PA_COLS), row), pl.BlockSpec((tm, BA_COLS), row),
                   pl.BlockSpec((tm, PB_COLS), row), pl.BlockSpec((tm, PC_COLS), row)],
        out_shape=[jax.ShapeDtypeStruct((n, PA_COLS), BF16), jax.ShapeDtypeStruct((n, BA_COLS), F32),
                   jax.ShapeDtypeStruct((n, PB_COLS), BF16), jax.ShapeDtypeStruct((n, PC_COLS), BF16)],
        scratch_shapes=[pltpu.VMEM((D_MODEL, W1_COLS), BF16)],
        compiler_params=pltpu.CompilerParams(dimension_semantics=("arbitrary",), vmem_limit_bytes=VMEM_LIMIT),
        name="in_proj",
    )(x2d, norm_w, w_in)


def _out_proj_kernel(x_ref, oa_ref, ob_ref, oc_ref, w_ref, o_ref):
    acc = _dot(oa_ref[...], w_ref[0:A_WIDTH, :])
    acc = acc + _dot(ob_ref[...], w_ref[A_WIDTH:A_WIDTH + B_WIDTH, :])
    acc = acc + _dot(oc_ref[...], w_ref[A_WIDTH + B_WIDTH:, :])
    o_ref[...] = x_ref[...] + acc


def _out_proj(x2d, oa, ob, oc, w_out, tm):
    n = x2d.shape[0]
    row = lambda i: (i, 0)
    fixed = lambda i: (0, 0)
    return pl.pallas_call(
        _out_proj_kernel,
        grid=(n // tm,),
        in_specs=[pl.BlockSpec((tm, D_MODEL), row), pl.BlockSpec((tm, A_WIDTH), row),
                  pl.BlockSpec((tm, B_WIDTH), row), pl.BlockSpec((tm, C_WIDTH), row),
                  pl.BlockSpec((D_MODEL, D_MODEL), fixed)],
        out_specs=pl.BlockSpec((tm, D_MODEL), row),
        out_shape=jax.ShapeDtypeStruct((n, D_MODEL), F32),
        compiler_params=pltpu.CompilerParams(dimension_semantics=("arbitrary",), vmem_limit_bytes=VMEM_LIMIT),
        name="out_proj",
    )(x2d, oa, ob, oc, w_out)


SB_BLOCK = 256


def _sb_kernel(q_ref, k_ref, v_ref, z_ref, o_ref):
    seq = q_ref.shape[1]
    nb = seq // SB_BLOCK
    head_lo = _head_lo((1, LANES))
    row = lax.broadcasted_iota(jnp.int32, (SB_BLOCK, SB_BLOCK), 0)
    col = lax.broadcasted_iota(jnp.int32, (SB_BLOCK, SB_BLOCK), 1)
    earlier = col < row
    jj = lax.broadcasted_iota(jnp.int32, (2 * SB_BLOCK, SB_BLOCK), 0) % SB_BLOCK
    ss = lax.broadcasted_iota(jnp.int32, (2 * SB_BLOCK, SB_BLOCK), 1)
    neg_suffix = jnp.where(jj >= ss, -1.0, 0.0).astype(BF16)

    def rows(b):
        return pl.ds(b * SB_BLOCK, SB_BLOCK)

    def scores(i, kt):
        q = q_ref[0, rows(i), :]
        zero = jnp.zeros_like(q)
        k = k_ref[0, rows(kt), :]
        return [_dot_nt(qh, k) for qh in (jnp.where(head_lo, q, zero), jnp.where(head_lo, zero, q))]

    def log_weights(z2, diag):
        nlk = [jnp.maximum(x, 0.0) + jnp.log2(1.0 + jnp.exp2(-jnp.abs(x))) for x in z2]
        if diag:
            nlk = [jnp.where(earlier, x, 0.0) for x in nlk]
        parts = [_split_bf16(x) for x in nlk]
        here_on = [_dot(jnp.concatenate([hi, lo], axis=1), neg_suffix) for hi, lo in parts]
        pre = [x + w for x, w in zip(z2, here_on)]
        if diag:
            pre = [jnp.where(earlier, x, NEG) for x in pre]
        return pre, [w[:, 0:1] for w in here_on]

    def accumulate(pre, carry, acc, kt):
        v = v_ref[0, rows(kt), :]
        w = [jnp.exp2(x + c).astype(BF16) for x, c in zip(pre, carry)]
        return [a + _dot(x, v) for a, x in zip(acc, w)]

    tiles = [(i, kt) for i in range(nb) for kt in range(i, -1, -1)]
    z2, lw = {}, {}
    carry = acc = None
    for step in range(len(tiles) + 2):
        if step < len(tiles):
            z2[step] = scores(*tiles[step])
        if 0 <= step - 1 < len(tiles):
            i, kt = tiles[step - 1]
            lw[step - 1] = log_weights(z2.pop(step - 1), i == kt)
        if 0 <= step - 2 < len(tiles):
            i, kt = tiles[step - 2]
            pre, tot = lw.pop(step - 2)
            if i == kt:
                carry = [jnp.zeros((SB_BLOCK, 1), F32)] * PAIR
                acc = [jnp.zeros((SB_BLOCK, LANES), F32)] * PAIR
            acc = accumulate(pre, carry, acc, kt)
            carry = [c + t for c, t in zip(carry, tot)]
            if kt == 0:
                o = jnp.where(head_lo, acc[0], acc[1])
                zg = z_ref[0, rows(i), :].astype(F32)
                o_ref[0, rows(i), :] = (o * (zg * _sigmoid(zg))).astype(BF16)


def _stick_breaking(pb):
    bsz, seq, _ = pb.shape
    npair = N_HEADS_B // PAIR
    spec = lambda g: pl.BlockSpec((1, seq, LANES), lambda b, p, g=g: (b, 0, g * npair + p))
    return pl.pallas_call(
        _sb_kernel,
        grid=(bsz, npair),
        in_specs=[spec(0), spec(1), spec(2), spec(3)],
        out_specs=pl.BlockSpec((1, seq, LANES), lambda b, p: (b, 0, p)),
        out_shape=jax.ShapeDtypeStruct((bsz, seq, B_WIDTH), BF16),
        compiler_params=pltpu.CompilerParams(dimension_semantics=("arbitrary", "arbitrary"),
                                             vmem_limit_bytes=VMEM_LIMIT),
        name="stick_breaking",
    )(pb, pb, pb, pb)


DW_BLOCK = 128
DW_SLOT = 2


def _dw_kernel(q_ref, k_ref, v_ref, z_ref, qw_ref, kw_ref, cos_ref, sina_ref, sinb_ref, o_ref,
               qf, kf, vf, q4f, k4f, v4f, qn, kn, qg4, kg4, vg4, qg16, kg16, vg16, m1_st, m2_st, l_st, acc_st):
    seq = q_ref.shape[1]
    head_lo = _head_lo((1, LANES))
    ones_bd = _ones_bd()
    n_rows = 256

    def prep(i, _):
        blocks = [pl.ds(pl.multiple_of((2 * i + b) * n_rows, n_rows), n_rows) for b in range(2)]
        xs = [(ref[0, rows, :].astype(F32), w_ref[...], rows)
              for rows in blocks for ref, w_ref in ((q_ref, qw_ref), (k_ref, kw_ref))]
        ms = [_head_sums(x * x, ones_bd) * (1.0 / HEAD_DIM) for x, _, _ in xs]
        ys = [x * lax.rsqrt(m + RMS_EPS) * w for (x, w, _), m in zip(xs, ms)]
        ys = [y * cos_ref[rows, :] + pltpu.roll(y, LANES - ROPE_DIM // 2, 1) * sina_ref[rows, :]
              + pltpu.roll(y, ROPE_DIM // 2, 1) * sinb_ref[rows, :] for y, (_, _, rows) in zip(ys, xs)]
        for b, rows in enumerate(blocks):
            q, k = ys[2 * b] * QSCALE2, ys[2 * b + 1]
            qf[rows, :] = q
            kf[rows, :] = k
            qn[rows, :] = q.astype(BF16)
            kn[rows, :] = k.astype(BF16)
            vf[rows, :] = v_ref[0, rows, :].astype(F32)
        return 0

    lax.fori_loop(0, seq // (2 * n_rows), prep, 0)

    qi = lax.broadcasted_iota(jnp.int32, (DW_BLOCK, DW_BLOCK), 0)
    kj = lax.broadcasted_iota(jnp.int32, (DW_BLOCK, DW_BLOCK), 1)
    own_ok = kj <= qi
    prev_ok = kj >= qi
    both_ok = jnp.concatenate([prev_ok, own_ok], axis=1)

    def pair_of(a, b):
        return jnp.where(head_lo, a, b)

    d_mid, d_far = DILATIONS[1], DILATIONS[2]
    len_mid, len_far = seq // d_mid, seq // d_far
    for src, mid_f, mid_b, far_b in ((qf, q4f, qg4, qg16), (kf, k4f, kg4, kg16), (vf, v4f, vg4, vg16)):
        for r in range(d_mid):
            x = src[pl.ds(r, len_mid, stride=d_mid), :]
            mid_f[pl.ds(r * len_mid, len_mid), :] = x
            mid_b[pl.ds(r * len_mid, len_mid), :] = x.astype(BF16)
        for r in range(d_far):
            x = mid_f[pl.ds((r % d_mid) * len_mid + r // d_mid, len_far, stride=d_far // d_mid), :]
            far_b[pl.ds(r * len_far, len_far), :] = x.astype(BF16)
    sources = {d_far: (qg16, kg16, lambda ks: vg16[ks, :]), d_mid: (qg4, kg4, lambda ks: vg4[ks, :]),
               1: (qn, kn, lambda ks: v_ref[0, ks, :])}

    def nat_rows(it):
        return pl.ds(it["start"], DW_BLOCK, stride=it["d"]) if it["d"] > 1 else pl.ds(it["start"], DW_BLOCK)

    def stage_scores(it):
        q_src, k_src, _ = sources[it["d"]]
        c0 = it["c0"]
        keys = pl.ds(c0 - DW_BLOCK, 2 * DW_BLOCK) if it["has_prev"] else pl.ds(c0, DW_BLOCK)
        ok = both_ok if it["has_prev"] else own_ok
        q = q_src[pl.ds(c0, DW_BLOCK), :]
        zero = jnp.zeros_like(q)
        k = k_src[keys, :]
        s = [jnp.where(ok, _dot_nt(qh, k), NEG) for qh in (jnp.where(head_lo, q, zero), jnp.where(head_lo, zero, q))]
        return dict(s=s, m_blk=[jnp.max(sh, axis=1, keepdims=True) for sh in s], keys=keys)

    def stage_softmax(it, st):
        width = 2 if it["has_prev"] else 1
        if it["d"] == d_far:
            m_h = [jnp.broadcast_to(m, (DW_BLOCK, LANES)) for m in st["m_blk"]]
            alpha = None
        else:
            nat = nat_rows(it)
            m_old = (m1_st[nat, :], m2_st[nat, :])
            m_h = [jnp.maximum(o, b) for o, b in zip(m_old, st["m_blk"])]
            alpha = jnp.exp2(pair_of(m_old[0], m_old[1]) - pair_of(m_h[0], m_h[1]))
        p = [jnp.exp2(sh - jnp.concatenate([mh] * width, axis=1)).astype(BF16) for sh, mh in zip(st["s"], m_h)]
        return dict(p=p, m_h=m_h, alpha=alpha, keys=st["keys"])

    def stage_values(it, st):
        _, _, v_of = sources[it["d"]]
        v = v_of(st["keys"])
        vo = jnp.concatenate([v, jnp.ones_like(v)], axis=1)
        pvl = [_dot(ph, vo) for ph in st["p"]]
        pv = pair_of(pvl[0][:, :LANES], pvl[1][:, :LANES])
        l_new = pair_of(pvl[0][:, LANES:], pvl[1][:, LANES:])
        nat = nat_rows(it)
        if st["alpha"] is not None:
            l_new = st["alpha"] * l_st[nat, :] + l_new
            pv = st["alpha"] * acc_st[nat, :] + pv
        if it["d"] == 1:
            zg = z_ref[0, nat, :].astype(F32)
            o_ref[0, nat, :] = (pv / l_new * (zg * _sigmoid(zg))).astype(BF16)
        else:
            m1_st[nat, :] = st["m_h"][0]
            m2_st[nat, :] = st["m_h"][1]
            l_st[nat, :] = l_new
            acc_st[nat, :] = pv

    groups = [[dict(d=d_far, c0=r * len_far, start=r, has_prev=False) for r in range(d_far)],
              [dict(d=d_mid, c0=r * len_mid + j * DW_BLOCK, start=r + j * DW_BLOCK * d_mid, has_prev=j > 0)
               for j in range(len_mid // DW_BLOCK) for r in range(d_mid)],
              [dict(d=1, c0=j * DW_BLOCK, start=j * DW_BLOCK, has_prev=j > 0) for j in range(seq // DW_BLOCK)]]
    slots = []
    for g in groups:
        slots += [g[i:i + DW_SLOT] for i in range(0, len(g), DW_SLOT)] + [[]]
    slots += [[]]
    scored, soft = {}, {}
    for step, slot in enumerate(slots):
        if step >= 2:
            for n, it in enumerate(slots[step - 2]):
                stage_values(it, soft.pop((step - 2, n)))
        if step >= 1:
            for n, it in enumerate(slots[step - 1]):
                soft[step - 1, n] = stage_softmax(it, scored.pop((step - 1, n)))
        for n, it in enumerate(slot):
            scored[step, n] = stage_scores(it)


def _rope_tables(seq):
    half = ROPE_DIM // 2
    inv_freq = ROPE_THETA ** (-jnp.arange(half, dtype=F32) / half)
    ang = jnp.arange(seq, dtype=jnp.int32).astype(F32)[:, None] * inv_freq[None, :]
    cos, sin = jnp.cos(ang), jnp.sin(ang)
    ones = jnp.ones((seq, HEAD_DIM - ROPE_DIM), F32)
    zeros_h = jnp.zeros((seq, half), F32)
    zeros_t = jnp.zeros((seq, HEAD_DIM - ROPE_DIM), F32)
    cos_h = jnp.concatenate([cos, cos, ones], axis=1)
    sina_h = jnp.concatenate([-sin, zeros_h, zeros_t], axis=1)
    sinb_h = jnp.concatenate([zeros_h, sin, zeros_t], axis=1)
    tile = lambda t: jnp.tile(t, (1, PAIR))
    return tile(cos_h), tile(sina_h), tile(sinb_h)


def _dilated(pc, q_norm_w, k_norm_w):
    bsz, seq, _ = pc.shape
    npair = N_HEADS_C // PAIR
    assert seq % (DW_BLOCK * DILATIONS[-1]) == 0 and DILATIONS[-1] % DILATIONS[1] == 0
    cos, sina, sinb = _rope_tables(seq)
    qw = jnp.tile(q_norm_w.astype(F32), PAIR)[None, :]
    kw = jnp.tile(k_norm_w.astype(F32), PAIR)[None, :]
    spec = lambda g: pl.BlockSpec((1, seq, LANES), lambda b, p, g=g: (b, 0, g * npair + p))
    fixed = lambda shape: pl.BlockSpec(shape, lambda b, p: (0, 0))
    f32_rows, bf16_rows = pltpu.VMEM((seq, LANES), F32), pltpu.VMEM((seq, LANES), BF16)
    return pl.pallas_call(
        _dw_kernel,
        grid=(bsz, npair),
        in_specs=[spec(0), spec(1), spec(2), spec(3), fixed((1, LANES)), fixed((1, LANES)),
                  fixed((seq, LANES)), fixed((seq, LANES)), fixed((seq, LANES))],
        out_specs=pl.BlockSpec((1, seq, LANES), lambda b, p: (b, 0, p)),
        out_shape=jax.ShapeDtypeStruct((bsz, seq, C_WIDTH), BF16),
        scratch_shapes=[f32_rows] * 6 + [bf16_rows] * 8 + [f32_rows] * 4,
        compiler_params=pltpu.CompilerParams(dimension_semantics=("arbitrary", "arbitrary"),
                                             vmem_limit_bytes=VMEM_LIMIT),
        name="dilated_window",
    )(pc, pc, pc, pc, qw, kw, cos, sina, sinb)


GDN_ROWS = 256
GDN_HALO = 16
GDN_GROUP = 16


def _gdn_kernel(pa_ref, ba_ref, cw_ref, alog_ref, dtb_ref, gnw_ref, o_ref,
                qn_s, kn_s, v_s, be_s, gc_s, qt_s, o0_s, nf_s, kw_s, gl_s, oc_s):
    seq = pa_ref.shape[1]
    n_chunks = seq // GDN_CHUNK
    npair = N_HEADS_A // PAIR
    lane = lax.broadcasted_iota(jnp.int32, (1, LANES), 1)
    head_lo = lane < HEAD_DIM
    ones_bd = _ones_bd()
    bd_mask = (lax.broadcasted_iota(jnp.int32, (LANES, LANES), 0) // HEAD_DIM
               == lax.broadcasted_iota(jnp.int32, (LANES, LANES), 1) // HEAD_DIM)

    ri = lax.broadcasted_iota(jnp.int32, (GDN_ROWS, 2 * GDN_ROWS), 0)
    ci = lax.broadcasted_iota(jnp.int32, (GDN_ROWS, 2 * GDN_ROWS), 1) % GDN_ROWS
    cum_mat = ((ri // GDN_CHUNK == ci // GDN_CHUNK) & (ci <= ri)).astype(BF16)

    n_shift = CONV_WIDTH - 1
    sr = lax.broadcasted_iota(jnp.int32, (n_shift * GDN_ROWS, GDN_ROWS), 0)
    sc = lax.broadcasted_iota(jnp.int32, (n_shift * GDN_ROWS, GDN_ROWS), 1)
    shift_mat = (sc == sr % GDN_ROWS - (CONV_WIDTH - 1) + sr // GDN_ROWS).astype(BF16)
    hr = lax.broadcasted_iota(jnp.int32, (n_shift * 8, GDN_HALO), 0)
    hc = lax.broadcasted_iota(jnp.int32, (n_shift * 8, GDN_HALO), 1)
    halo_mat = (hc == GDN_HALO + hr % 8 - (CONV_WIDTH - 1) + hr // 8).astype(BF16)

    ii = lax.broadcasted_iota(jnp.int32, (GDN_CHUNK, LANES), 0)
    jl = lax.broadcasted_iota(jnp.int32, (GDN_CHUNK, LANES), 1) % HEAD_DIM
    incl = ii >= jl
    strict = ii > jl
    eye_pair = ii == jl
    eye_f = eye_pair.astype(F32)

    er = lax.broadcasted_iota(jnp.int32, (2 * LANES, 2 * npair * LANES), 0) % LANES
    ec = lax.broadcasted_iota(jnp.int32, (2 * LANES, 2 * npair * LANES), 1)
    want = (ec // LANES % 2) * N_HEADS_A + 2 * (ec // (2 * LANES)) + ec % LANES // HEAD_DIM
    expand = (er == want).astype(BF16)
    rq = lax.broadcasted_iota(jnp.int32, (2 * LANES, 2 * LANES), 0) // HEAD_DIM
    cq = lax.broadcasted_iota(jnp.int32, (2 * LANES, 2 * LANES), 1) // HEAD_DIM
    ones_bd4 = (rq == cq).astype(BF16)

    def prep(i, _):
        r0 = pl.multiple_of(i * GDN_ROWS, GDN_ROWS)
        rows = pl.ds(r0, GDN_ROWS)
        halo_rows = pl.ds(pl.multiple_of(jnp.maximum(r0 - GDN_HALO, 0), GDN_HALO), GDN_HALO)
        conv_cols = slice(0, 3 * A_WIDTH)
        halo = pa_ref[0, halo_rows, conv_cols]
        halo = jnp.where(i > 0, halo, jnp.zeros_like(halo))
        taps = _dot(shift_mat, pa_ref[0, rows, conv_cols])
        edge = _dot(halo_mat, halo)

        ba = ba_ref[0, rows, :]
        g = -jnp.exp(alog_ref[...]) * _softplus(ba + dtb_ref[...])
        g = jnp.where((lane >= N_HEADS_A) & (lane < 2 * N_HEADS_A), g, 0.0)
        hi, lo = _split_bf16(g)
        gc = _dot(cum_mat, jnp.concatenate([hi, lo], axis=0))
        nar = jnp.where(lane < N_HEADS_A, _sigmoid(ba), gc)
        hi, lo = _split_bf16(nar)
        wide = _dot(jnp.concatenate([hi, lo], axis=1), expand)

        def conv_silu(c):
            w = cw_ref[:, c]
            y = w[CONV_WIDTH - 1:CONV_WIDTH, :] * pa_ref[0, rows, c].astype(F32)
            y0 = 0.0
            for kk in range(CONV_WIDTH - 1):
                y = y + w[kk:kk + 1, :] * taps[kk * GDN_ROWS:(kk + 1) * GDN_ROWS, c]
                y0 = y0 + w[kk:kk + 1, :] * edge[kk * 8:(kk + 1) * 8, c]
            y = jnp.concatenate([y[:8, :] + y0, y[8:, :]], axis=0)
            return y * _sigmoid(y)

        ys = [[conv_silu(slice(t * A_WIDTH + p * LANES, t * A_WIDTH + (p + 1) * LANES)) for t in range(3)]
              for p in range(npair)]
        ss = [_dot(jnp.concatenate([y[0] * y[0], y[1] * y[1]], axis=1).astype(BF16), ones_bd4) for y in ys]
        for p in range(npair):
            qn_s[p, rows, :] = ys[p][0] * lax.rsqrt(ss[p][:, :LANES] + RMS_EPS) * (HEAD_DIM ** -0.5)
            kn_s[p, rows, :] = ys[p][1] * lax.rsqrt(ss[p][:, LANES:] + RMS_EPS)
            v_s[p, rows, :] = ys[p][2]
            be_s[p, rows, :] = wide[:, 2 * p * LANES:(2 * p + 1) * LANES]
            gc_s[p, rows, :] = wide[:, (2 * p + 1) * LANES:(2 * p + 2) * LANES]
        return 0

    lax.fori_loop(0, seq // GDN_ROWS, prep, 0)

    def bd(ms):
        return [_block_diag(m, head_lo) for m in ms]

    def chunk_group(p, i, _):
        cs = [i * GDN_GROUP + gi for gi in range(GDN_GROUP)]
        rows = [pl.ds(pl.multiple_of(c * GDN_CHUNK, GDN_CHUNK), GDN_CHUNK) for c in cs]
        qn = [qn_s[p, r, :] for r in rows]
        kn = [kn_s[p, r, :] for r in rows]
        v = [v_s[p, r, :] for r in rows]
        be = [be_s[p, r, :] for r in rows]
        gc = [gc_s[p, r, :] for r in rows]
        kb = [a * b for a, b in zip(kn, be)]
        ap = [_dot_nt(jnp.concatenate([a, b], axis=0).astype(BF16), m) for a, b, m in zip(kb, qn, bd(kn))]
        g_row = [jnp.sum(jnp.where(eye_pair, x, 0.0), axis=0, keepdims=True) for x in gc]
        decay = [jnp.exp(jnp.where(incl, x - y, NEG)) for x, y in zip(gc, g_row)]
        x = [jnp.where(strict, -a[:GDN_CHUNK] * d, 0.0) for a, d in zip(ap, decay)]
        pm = [a[GDN_CHUNK:] * d for a, d in zip(ap, decay)]
        tm = [eye_f + a for a in x]
        xr = [_dot(a.astype(BF16), m) for a, m in zip(x, bd(x))]
        for r in range(1, 6):
            rhs = bd(xr)
            if r < 5:
                y = [_dot(jnp.concatenate([a, b], axis=0).astype(BF16), m) for a, b, m in zip(xr, tm, rhs)]
                xr = [a[:GDN_CHUNK] for a in y]
                tm = [a + b[GDN_CHUNK:] for a, b in zip(tm, y)]
            else:
                tm = [a + _dot(a.astype(BF16), m) for a, m in zip(tm, rhs)]
        eg = [jnp.exp(a) for a in gc]
        vb = bd([a * b for a, b in zip(v, be)])
        kbg = bd([a * b for a, b in zip(kb, eg)])
        uw = [_dot(a.astype(BF16), jnp.concatenate([b, c], axis=1)) for a, b, c in zip(tm, vb, kbg)]
        u_bd = bd([a[:, :LANES] for a in uw])
        w_bd = bd([a[:, LANES:] for a in uw])
        puw = [_dot(a.astype(BF16), jnp.concatenate([b, c], axis=1)) for a, b, c in zip(pm, u_bd, w_bd)]
        g_last = [a[GDN_CHUNK - 1:GDN_CHUNK, :] for a in gc]
        kg = [a * jnp.exp(b - c) for a, b, c in zip(kn, g_last, gc)]
        kuw = [_dot_tn(a.astype(BF16), b.astype(BF16)) for a, b in zip(kg, uw)]
        for c, r, q, e, a, b, gl in zip(cs, rows, qn, eg, puw, kuw, g_last):
            qt_s[p, r, :] = (q * e - a[:, LANES:]).astype(BF16)
            o0_s[p, r, :] = a[:, :LANES]
            nf_s[p, c] = jnp.where(bd_mask, b[:, :LANES], 0.0)
            kw_s[p, c] = jnp.where(bd_mask, b[:, LANES:], 0.0).astype(BF16)
            gl_s[p, c] = jnp.broadcast_to(jnp.exp(gl), (8, LANES))
        return 0

    def per_pair(p, _):
        lax.fori_loop(0, n_chunks // GDN_GROUP, functools.partial(chunk_group, p), 0)
        return 0

    lax.fori_loop(0, npair, per_pair, 0)

    def scan(c, states):
        r0 = pl.multiple_of(c * GDN_CHUNK, GDN_CHUNK)
        rows = pl.ds(r0, GDN_CHUNK)
        sb = [s.astype(BF16) for s in states]
        ks = [_dot(kw_s[p, c], sb[p]) for p in range(npair)]
        for p in range(npair):
            oc_s[p, rows, :] = _dot(qt_s[p, rows, :], sb[p]) + o0_s[p, rows, :]
        return tuple(gl_s[p, c][0:1, :] * states[p] + nf_s[p, c] - ks[p] for p in range(npair))

    lax.fori_loop(0, n_chunks, scan, tuple(jnp.zeros((LANES, LANES), F32) for _ in range(npair)))

    def finish(i, _):
        r0 = pl.multiple_of(i * GDN_ROWS, GDN_ROWS)
        rows = pl.ds(r0, GDN_ROWS)
        for p in range(npair):
            o = oc_s[p, rows, :]
            ms = _head_sums(o * o, ones_bd) * (1.0 / HEAD_DIM)
            y = o * lax.rsqrt(ms + RMS_EPS) * gnw_ref[...]
            zg = pa_ref[0, rows, 3 * A_WIDTH + p * LANES:3 * A_WIDTH + (p + 1) * LANES].astype(F32)
            o_ref[0, rows, p * LANES:(p + 1) * LANES] = (y * (zg * _sigmoid(zg))).astype(BF16)
        return 0

    lax.fori_loop(0, seq // GDN_ROWS, finish, 0)


def _gdn(pa, ba, conv_w, a_log, dt_bias, gdn_norm_w):
    bsz, seq, _ = pa.shape
    npair = N_HEADS_A // PAIR
    n_chunks = seq // GDN_CHUNK
    pad = lambda vec: jnp.zeros((1, LANES), F32).at[0, N_HEADS_A:2 * N_HEADS_A].set(vec.astype(F32))
    gnw = jnp.tile(gdn_norm_w.astype(F32), PAIR)[None, :]
    fixed = lambda shape: pl.BlockSpec(shape, lambda b: (0,) * len(shape))
    per_pair_f32 = pltpu.VMEM((npair, seq, LANES), F32)
    return pl.pallas_call(
        _gdn_kernel,
        grid=(bsz,),
        in_specs=[pl.BlockSpec((1, seq, PA_COLS), lambda b: (b, 0, 0)),
                  pl.BlockSpec((1, seq, BA_COLS), lambda b: (b, 0, 0)),
                  fixed((CONV_WIDTH, 3 * A_WIDTH)), fixed((1, LANES)), fixed((1, LANES)), fixed((1, LANES))],
        out_specs=pl.BlockSpec((1, seq, A_WIDTH), lambda b: (b, 0, 0)),
        out_shape=jax.ShapeDtypeStruct((bsz, seq, A_WIDTH), BF16),
        scratch_shapes=[
            per_pair_f32, per_pair_f32, per_pair_f32,
            per_pair_f32, per_pair_f32,
            pltpu.VMEM((npair, seq, LANES), BF16),
            per_pair_f32,
            pltpu.VMEM((npair, n_chunks, LANES, LANES), F32),
            pltpu.VMEM((npair, n_chunks, LANES, LANES), BF16),
            pltpu.VMEM((npair, n_chunks, 8, LANES), F32),
            per_pair_f32,
        ],
        compiler_params=pltpu.CompilerParams(dimension_semantics=("arbitrary",), vmem_limit_bytes=VMEM_LIMIT),
        name="gated_delta",
    )(pa, ba, conv_w.astype(F32), pad(a_log), pad(dt_bias), gnw)


ROW_TILE = 512


def kernel(x, norm_w, w_in, conv_w, a_log, dt_bias, gdn_norm_w, q_norm_w, k_norm_w, w_out):
    bsz, seq, _ = x.shape
    assert w_in.shape[1:] == (D_MODEL, IN_COLS)
    shape3 = lambda t: t.reshape(bsz, seq, t.shape[-1])
    flat = lambda t: t.reshape(bsz * seq, t.shape[-1])
    x2d = flat(x)
    w_in = w_in.astype(F32)
    norm_w = norm_w.astype(F32)
    for layer in range(norm_w.shape[0]):
        pa, ba, pb, pc = _in_proj(x2d, norm_w[layer][None, :], w_in, layer, ROW_TILE)
        oa = _gdn(shape3(pa), shape3(ba), conv_w[layer], a_log[layer], dt_bias[layer], gdn_norm_w[layer])
        ob = _stick_breaking(shape3(pb))
        oc = _dilated(shape3(pc), q_norm_w[layer], k_norm_w[layer])
        x2d = _out_proj(x2d, flat(oa), flat(ob), flat(oc), w_out[layer].astype(BF16), ROW_TILE)
    return shape3(x2d)
```

```python
import functools

import jax
import jax.numpy as jnp
from jax import lax
from jax.experimental import pallas as pl
from jax.experimental.pallas import tpu as pltpu

F32 = jnp.float32
BF16 = jnp.bfloat16

D_MODEL = 1024
HEAD_DIM = 64
N_HEADS_A, N_HEADS_B, N_HEADS_C = 6, 4, 6
A_WIDTH, B_WIDTH, C_WIDTH = N_HEADS_A * HEAD_DIM, N_HEADS_B * HEAD_DIM, N_HEADS_C * HEAD_DIM
CONV_WIDTH = 4
GDN_CHUNK = 64
ROPE_DIM = HEAD_DIM // 4
ROPE_THETA = 500000.0
DILATIONS = (1, 4, 16)
RMS_EPS = 1e-6

LANES = 128
PAIR = LANES // HEAD_DIM
NEG = -1e30

PA_COLS = 4 * A_WIDTH
BA_COLS = LANES
PB_COLS = 4 * B_WIDTH
PC_COLS = 4 * C_WIDTH
IN_COLS = PA_COLS + 2 * N_HEADS_A + PB_COLS + PC_COLS
W1_COLS = PA_COLS + BA_COLS + PB_COLS + PC_COLS

VMEM_LIMIT = 56 * 1024 * 1024

LOG2E = 1.4426950408889634
QSCALE2 = HEAD_DIM ** -0.5 * LOG2E


def _dot(a, b):
    return jnp.dot(a, b, preferred_element_type=F32)


def _dot_nt(a, b):
    return lax.dot_general(a, b, (((1,), (1,)), ((), ())), preferred_element_type=F32)


def _dot_tn(a, b):
    return lax.dot_general(a, b, (((0,), (0,)), ((), ())), preferred_element_type=F32)


def _aligned(x, m):
    return x if isinstance(x, int) else pl.multiple_of(x, m)


def _sigmoid(x):
    return 1.0 / (1.0 + jnp.exp(-x))


def _softplus(x):
    return jnp.maximum(x, 0.0) + jnp.log(1.0 + jnp.exp(-jnp.abs(x)))


def _split_bf16(x):
    hi = x.astype(BF16)
    lo = (x - hi.astype(F32)).astype(BF16)
    return hi, lo


def _head_lo(shape):
    return lax.broadcasted_iota(jnp.int32, shape, len(shape) - 1) < HEAD_DIM


def _block_diag(m, head_lo):
    z = jnp.zeros_like(m)
    return jnp.concatenate([jnp.where(head_lo, m, z), jnp.where(head_lo, z, m)], axis=0).astype(BF16)


def _head_sums(x2, ones_bd):
    return _dot(x2.astype(BF16), ones_bd)


def _ones_bd():
    r = lax.broadcasted_iota(jnp.int32, (LANES, LANES), 0) // HEAD_DIM
    c = lax.broadcasted_iota(jnp.int32, (LANES, LANES), 1) // HEAD_DIM
    return (r == c).astype(BF16)


W_ROWS = 128


def _in_proj_kernel(x_ref, nw_ref, w_ref, pa_ref, ba_ref, pb_ref, pc_ref, wb):
    @pl.when(pl.program_id(0) == 0)
    def _():
        def rows(i, _):
            r = pl.ds(pl.multiple_of(i * W_ROWS, W_ROWS), W_ROWS)
            wb[r, 0:PA_COLS + BA_COLS] = w_ref[0, r, 0:PA_COLS + BA_COLS].astype(BF16)
            wb[r, PA_COLS + BA_COLS:W1_COLS] = w_ref[0, r, PA_COLS + 2 * N_HEADS_A:IN_COLS].astype(BF16)
            return 0

        lax.fori_loop(0, D_MODEL // W_ROWS, rows, 0)

    x = x_ref[...]
    ms = jnp.mean(x * x, axis=-1, keepdims=True)
    h = (x * lax.rsqrt(ms + RMS_EPS) * nw_ref[...]).astype(BF16)
    c0 = 0
    pa_ref[...] = _dot(h, wb[:, c0:c0 + PA_COLS]).astype(BF16)
    c0 += PA_COLS
    ba_ref[...] = _dot(h, wb[:, c0:c0 + BA_COLS])
    c0 += BA_COLS
    col_b = lax.broadcasted_iota(jnp.int32, (1, PB_COLS), 1)
    pb_ref[...] = (_dot(h, wb[:, c0:c0 + PB_COLS]) * jnp.where(col_b < B_WIDTH, QSCALE2, 1.0)).astype(BF16)
    c0 += PB_COLS
    pc_ref[...] = _dot(h, wb[:, c0:c0 + PC_COLS]).astype(BF16)


def _in_proj(x2d, norm_w, w_in, layer, tm):
    n = x2d.shape[0]
    row = lambda i: (i, 0)
    return pl.pallas_call(
        _in_proj_kernel,
        grid=(n // tm,),
        in_specs=[pl.BlockSpec((tm, D_MODEL), row),
                  pl.BlockSpec((1, D_MODEL), lambda i: (0, 0)),
                  pl.BlockSpec((1, D_MODEL, IN_COLS), lambda i: (layer, 0, 0), pipeline_mode=pl.Buffered(1))],
        out_specs=[pl.BlockSpec((tm, PA_COLS), row), pl.BlockSpec((tm, BA_COLS), row),
                   pl.BlockSpec((tm, PB_COLS), row), pl.BlockSpec((tm, PC_COLS), row)],
        out_shape=[jax.ShapeDtypeStruct((n, PA_COLS), BF16), jax.ShapeDtypeStruct((n, BA_COLS), F32),
                   jax.ShapeDtypeStruct((n, PB_COLS), BF16), jax.ShapeDtypeStruct((n, PC_COLS), BF16)],
        scratch_shapes=[pltpu.VMEM((D_MODEL, W1_COLS), BF16)],
        compiler_params=pltpu.CompilerParams(dimension_semantics=("arbitrary",), vmem_limit_bytes=VMEM_LIMIT),
        name="in_proj",
    )(x2d, norm_w, w_in)


def _out_proj_kernel(x_ref, oa_ref, ob_ref, oc_ref, w_ref, o_ref):
    acc = _dot(oa_ref[...], w_ref[0:A_WIDTH, :])
    acc = acc + _dot(ob_ref[...], w_ref[A_WIDTH:A_WIDTH + B_WIDTH, :])
    acc = acc + _dot(oc_ref[...], w_ref[A_WIDTH + B_WIDTH:, :])
    o_ref[...] = x_ref[...] + acc


def _out_proj(x2d, oa, ob, oc, w_out, tm):
    n = x2d.shape[0]
    row = lambda i: (i, 0)
    fixed = lambda i: (0, 0)
    return pl.pallas_call(
        _out_proj_kernel,
        grid=(n // tm,),
        in_specs=[pl.BlockSpec((tm, D_MODEL), row), pl.BlockSpec((tm, A_WIDTH), row),
                  pl.BlockSpec((tm, B_WIDTH), row), pl.BlockSpec((tm, C_WIDTH), row),
                  pl.BlockSpec((D_MODEL, D_MODEL), fixed)],
        out_specs=pl.BlockSpec((tm, D_MODEL), row),
        out_shape=jax.ShapeDtypeStruct((n, D_MODEL), F32),
        compiler_params=pltpu.CompilerParams(dimension_semantics=("arbitrary",), vmem_limit_bytes=VMEM_LIMIT),
        name="out_proj",
    )(x2d, oa, ob, oc, w_out)


SB_BLOCK = 256


def _sb_kernel(q_ref, k_ref, v_ref, z_ref, o_ref):
    seq = q_ref.shape[1]
    nb = seq // SB_BLOCK
    head_lo = _head_lo((1, LANES))
    row = lax.broadcasted_iota(jnp.int32, (SB_BLOCK, SB_BLOCK), 0)
    col = lax.broadcasted_iota(jnp.int32, (SB_BLOCK, SB_BLOCK), 1)
    earlier = col < row
    neg_suffix = jnp.where(row >= col, -1.0, 0.0).astype(BF16)

    def rows(b):
        return pl.ds(b * SB_BLOCK, SB_BLOCK)

    def scores(i, kt):
        q = q_ref[0, rows(i), :]
        zero = jnp.zeros_like(q)
        k = k_ref[0, rows(kt), :]
        return [_dot_nt(qh, k) for qh in (jnp.where(head_lo, q, zero), jnp.where(head_lo, zero, q))]

    def log_weights(z2, diag):
        nlk = [jnp.maximum(x, 0.0) + jnp.log2(1.0 + jnp.exp2(-jnp.abs(x))) for x in z2]
        if diag:
            nlk = [jnp.where(earlier, x, 0.0) for x in nlk]
        here_on = [_dot(x.astype(BF16), neg_suffix) for x in nlk]
        pre = [x + w for x, w in zip(z2, here_on)]
        if diag:
            pre = [jnp.where(earlier, x, NEG) for x in pre]
        return pre, [w[:, 0:1] for w in here_on]

    def accumulate(pre, carry, acc, kt):
        v = v_ref[0, rows(kt), :]
        w = [jnp.exp2(x + c).astype(BF16) for x, c in zip(pre, carry)]
        return [a + _dot(x, v) for a, x in zip(acc, w)]

    tiles = [(i, kt) for i in range(nb) for kt in range(i, -1, -1)]
    z2, lw = {}, {}
    carry = acc = None
    for step in range(len(tiles) + 2):
        if step < len(tiles):
            z2[step] = scores(*tiles[step])
        if 0 <= step - 1 < len(tiles):
            i, kt = tiles[step - 1]
            lw[step - 1] = log_weights(z2.pop(step - 1), i == kt)
        if 0 <= step - 2 < len(tiles):
            i, kt = tiles[step - 2]
            pre, tot = lw.pop(step - 2)
            if i == kt:
                carry = [jnp.zeros((SB_BLOCK, 1), F32)] * PAIR
                acc = [jnp.zeros((SB_BLOCK, LANES), F32)] * PAIR
            acc = accumulate(pre, carry, acc, kt)
            carry = [c + t for c, t in zip(carry, tot)]
            if kt == 0:
                o = jnp.where(head_lo, acc[0], acc[1])
                zg = z_ref[0, rows(i), :].astype(F32)
                o_ref[0, rows(i), :] = (o * (zg * _sigmoid(zg))).astype(BF16)


def _stick_breaking(pb):
    bsz, seq, _ = pb.shape
    npair = N_HEADS_B // PAIR
    spec = lambda g: pl.BlockSpec((1, seq, LANES), lambda b, p, g=g: (b, 0, g * npair + p))
    return pl.pallas_call(
        _sb_kernel,
        grid=(bsz, npair),
        in_specs=[spec(0), spec(1), spec(2), spec(3)],
        out_specs=pl.BlockSpec((1, seq, LANES), lambda b, p: (b, 0, p)),
        out_shape=jax.ShapeDtypeStruct((bsz, seq, B_WIDTH), BF16),
        compiler_params=pltpu.CompilerParams(dimension_semantics=("arbitrary", "arbitrary"),
                                             vmem_limit_bytes=VMEM_LIMIT),
        name="stick_breaking",
    )(pb, pb, pb, pb)


DW_BLOCK = 128
DW_SLOT = 2


def _dw_kernel(q_ref, k_ref, v_ref, z_ref, qw_ref, kw_ref, cos_ref, sina_ref, sinb_ref, o_ref,
               qf, kf, vf, q4f, k4f, v4f, qn, kn, qg4, kg4, vg4, qg16, kg16, vg16, m1_st, m2_st, l_st, acc_st):
    seq = q_ref.shape[1]
    head_lo = _head_lo((1, LANES))
    ones_bd = _ones_bd()
    n_rows = 256

    def prep(i, _):
        blocks = [pl.ds(pl.multiple_of((2 * i + b) * n_rows, n_rows), n_rows) for b in range(2)]
        xs = [(ref[0, rows, :].astype(F32), w_ref[...], rows)
              for rows in blocks for ref, w_ref in ((q_ref, qw_ref), (k_ref, kw_ref))]
        ms = [_head_sums(x * x, ones_bd) * (1.0 / HEAD_DIM) for x, _, _ in xs]
        ys = [x * lax.rsqrt(m + RMS_EPS) * w for (x, w, _), m in zip(xs, ms)]
        ys = [y * cos_ref[rows, :] + pltpu.roll(y, LANES - ROPE_DIM // 2, 1) * sina_ref[rows, :]
              + pltpu.roll(y, ROPE_DIM // 2, 1) * sinb_ref[rows, :] for y, (_, _, rows) in zip(ys, xs)]
        for b, rows in enumerate(blocks):
            q, k = ys[2 * b] * QSCALE2, ys[2 * b + 1]
            qf[rows, :] = q
            kf[rows, :] = k
            qn[rows, :] = q.astype(BF16)
            kn[rows, :] = k.astype(BF16)
            vf[rows, :] = v_ref[0, rows, :].astype(F32)
        return 0

    lax.fori_loop(0, seq // (2 * n_rows), prep, 0)

    qi = lax.broadcasted_iota(jnp.int32, (DW_BLOCK, DW_BLOCK), 0)
    kj = lax.broadcasted_iota(jnp.int32, (DW_BLOCK, DW_BLOCK), 1)
    own_ok = kj <= qi
    prev_ok = kj >= qi
    both_ok = jnp.concatenate([prev_ok, own_ok], axis=1)

    def pair_of(a, b):
        return jnp.where(head_lo, a, b)

    d_mid, d_far = DILATIONS[1], DILATIONS[2]
    len_mid, len_far = seq // d_mid, seq // d_far
    for src, mid_f, mid_b, far_b in ((qf, q4f, qg4, qg16), (kf, k4f, kg4, kg16), (vf, v4f, vg4, vg16)):
        for r in range(d_mid):
            x = src[pl.ds(r, len_mid, stride=d_mid), :]
            mid_f[pl.ds(r * len_mid, len_mid), :] = x
            mid_b[pl.ds(r * len_mid, len_mid), :] = x.astype(BF16)
        for r in range(d_far):
            x = mid_f[pl.ds((r % d_mid) * len_mid + r // d_mid, len_far, stride=d_far // d_mid), :]
            far_b[pl.ds(r * len_far, len_far), :] = x.astype(BF16)
    sources = {d_far: (qg16, kg16, lambda ks: vg16[ks, :]), d_mid: (qg4, kg4, lambda ks: vg4[ks, :]),
               1: (qn, kn, lambda ks: v_ref[0, ks, :])}

    def nat_rows(it):
        return pl.ds(it["start"], DW_BLOCK, stride=it["d"]) if it["d"] > 1 else pl.ds(it["start"], DW_BLOCK)

    def stage_scores(it):
        q_src, k_src, _ = sources[it["d"]]
        c0 = it["c0"]
        keys = pl.ds(c0 - DW_BLOCK, 2 * DW_BLOCK) if it["has_prev"] else pl.ds(c0, DW_BLOCK)
        ok = both_ok if it["has_prev"] else own_ok
        q = q_src[pl.ds(c0, DW_BLOCK), :]
        zero = jnp.zeros_like(q)
        k = k_src[keys, :]
        s = [jnp.where(ok, _dot_nt(qh, k), NEG) for qh in (jnp.where(head_lo, q, zero), jnp.where(head_lo, zero, q))]
        return dict(s=s, m_blk=[jnp.max(sh, axis=1, keepdims=True) for sh in s], keys=keys)

    def stage_softmax(it, st):
        width = 2 if it["has_prev"] else 1
        if it["d"] == d_far:
            m_h = [jnp.broadcast_to(m, (DW_BLOCK, LANES)) for m in st["m_blk"]]
            alpha = None
        else:
            nat = nat_rows(it)
            m_old = (m1_st[nat, :], m2_st[nat, :])
            m_h = [jnp.maximum(o, b) for o, b in zip(m_old, st["m_blk"])]
            alpha = jnp.exp2(pair_of(m_old[0], m_old[1]) - pair_of(m_h[0], m_h[1]))
        p = [jnp.exp2(sh - jnp.concatenate([mh] * width, axis=1)).astype(BF16) for sh, mh in zip(st["s"], m_h)]
        return dict(p=p, m_h=m_h, alpha=alpha, keys=st["keys"])

    def stage_values(it, st):
        _, _, v_of = sources[it["d"]]
        v = v_of(st["keys"])
        vo = jnp.concatenate([v, jnp.ones_like(v)], axis=1)
        pvl = [_dot(ph, vo) for ph in st["p"]]
        pv = pair_of(pvl[0][:, :LANES], pvl[1][:, :LANES])
        l_new = pair_of(pvl[0][:, LANES:], pvl[1][:, LANES:])
        nat = nat_rows(it)
        if st["alpha"] is not None:
            l_new = st["alpha"] * l_st[nat, :] + l_new
            pv = st["alpha"] * acc_st[nat, :] + pv
        if it["d"] == 1:
            zg = z_ref[0, nat, :].astype(F32)
            o_ref[0, nat, :] = (pv / l_new * (zg * _sigmoid(zg))).astype(BF16)
        else:
            m1_st[nat, :] = st["m_h"][0]
            m2_st[nat, :] = st["m_h"][1]
            l_st[nat, :] = l_new
            acc_st[nat, :] = pv

    groups = [[dict(d=d_far, c0=r * len_far, start=r, has_prev=False) for r in range(d_far)],
              [dict(d=d_mid, c0=r * len_mid + j * DW_BLOCK, start=r + j * DW_BLOCK * d_mid, has_prev=j > 0)
               for j in range(len_mid // DW_BLOCK) for r in range(d_mid)],
              [dict(d=1, c0=j * DW_BLOCK, start=j * DW_BLOCK, has_prev=j > 0) for j in range(seq // DW_BLOCK)]]
    slots = []
    for g in groups:
        slots += [g[i:i + DW_SLOT] for i in range(0, len(g), DW_SLOT)] + [[]]
    slots += [[]]
    scored, soft = {}, {}
    for step, slot in enumerate(slots):
        if step >= 2:
            for n, it in enumerate(slots[step - 2]):
                stage_values(it, soft.pop((step - 2, n)))
        if step >= 1:
            for n, it in enumerate(slots[step - 1]):
                soft[step - 1, n] = stage_softmax(it, scored.pop((step - 1, n)))
        for n, it in enumerate(slot):
            scored[step, n] = stage_scores(it)


def _rope_tables(seq):
    half = ROPE_DIM // 2
    inv_freq = ROPE_THETA ** (-jnp.arange(half, dtype=F32) / half)
    ang = jnp.arange(seq, dtype=jnp.int32).astype(F32)[:, None] * inv_freq[None, :]
    cos, sin = jnp.cos(ang), jnp.sin(ang)
    ones = jnp.ones((seq, HEAD_DIM - ROPE_DIM), F32)
    zeros_h = jnp.zeros((seq, half), F32)
    zeros_t = jnp.zeros((seq, HEAD_DIM - ROPE_DIM), F32)
    cos_h = jnp.concatenate([cos, cos, ones], axis=1)
    sina_h = jnp.concatenate([-sin, zeros_h, zeros_t], axis=1)
    sinb_h = jnp.concatenate([zeros_h, sin, zeros_t], axis=1)
    tile = lambda t: jnp.tile(t, (1, PAIR))
    return tile(cos_h), tile(sina_h), tile(sinb_h)


def _dilated(pc, q_norm_w, k_norm_w):
    bsz, seq, _ = pc.shape
    npair = N_HEADS_C // PAIR
    assert seq % (DW_BLOCK * DILATIONS[-1]) == 0 and DILATIONS[-1] % DILATIONS[1] == 0
    cos, sina, sinb = _rope_tables(seq)
    qw = jnp.tile(q_norm_w.astype(F32), PAIR)[None, :]
    kw = jnp.tile(k_norm_w.astype(F32), PAIR)[None, :]
    spec = lambda g: pl.BlockSpec((1, seq, LANES), lambda b, p, g=g: (b, 0, g * npair + p))
    fixed = lambda shape: pl.BlockSpec(shape, lambda b, p: (0, 0))
    f32_rows, bf16_rows = pltpu.VMEM((seq, LANES), F32), pltpu.VMEM((seq, LANES), BF16)
    return pl.pallas_call(
        _dw_kernel,
        grid=(bsz, npair),
        in_specs=[spec(0), spec(1), spec(2), spec(3), fixed((1, LANES)), fixed((1, LANES)),
                  fixed((seq, LANES)), fixed((seq, LANES)), fixed((seq, LANES))],
        out_specs=pl.BlockSpec((1, seq, LANES), lambda b, p: (b, 0, p)),
        out_shape=jax.ShapeDtypeStruct((bsz, seq, C_WIDTH), BF16),
        scratch_shapes=[f32_rows] * 6 + [bf16_rows] * 8 + [f32_rows] * 4,
        compiler_params=pltpu.CompilerParams(dimension_semantics=("arbitrary", "arbitrary"),
                                             vmem_limit_bytes=VMEM_LIMIT),
        name="dilated_window",
    )(pc, pc, pc, pc, qw, kw, cos, sina, sinb)


GDN_ROWS = 256
GDN_HALO = 16
GDN_GROUP = 16


def _gdn_constants():
    npair = N_HEADS_A // PAIR
    iota = lambda shape, dim: lax.broadcasted_iota(jnp.int32, shape, dim)
    n_shift = CONV_WIDTH - 1
    sr, sc = iota((n_shift * GDN_ROWS, GDN_ROWS), 0), iota((n_shift * GDN_ROWS, GDN_ROWS), 1)
    shift = sc == sr % GDN_ROWS - (CONV_WIDTH - 1) + sr // GDN_ROWS
    hr, hc = iota((n_shift * 8, GDN_HALO), 0), iota((n_shift * 8, GDN_HALO), 1)
    halo = hc == GDN_HALO + hr % 8 - (CONV_WIDTH - 1) + hr // 8
    ri, ci = iota((GDN_ROWS, 2 * GDN_ROWS), 0), iota((GDN_ROWS, 2 * GDN_ROWS), 1) % GDN_ROWS
    cum = (ri // GDN_CHUNK == ci // GDN_CHUNK) & (ci <= ri)
    er, ec = iota((2 * LANES, 2 * npair * LANES), 0) % LANES, iota((2 * LANES, 2 * npair * LANES), 1)
    expand = er == (ec // LANES % 2) * N_HEADS_A + 2 * (ec // (2 * LANES)) + ec % LANES // HEAD_DIM
    bd4 = iota((2 * LANES, 2 * LANES), 0) // HEAD_DIM == iota((2 * LANES, 2 * LANES), 1) // HEAD_DIM
    return [m.astype(BF16) for m in (shift, halo, cum, expand, bd4)]


def _gdn_kernel(pa_ref, ba_ref, cw_ref, alog_ref, dtb_ref, gnw_ref, shift_ref, halo_ref, cum_ref, expand_ref,
                bd4_ref, o_ref, qn_s, kn_s, v_s, be_s, gc_s, qt_s, o0_s, nf_s, kw_s, gl_s, zg_s):
    seq = pa_ref.shape[1]
    n_chunks = seq // GDN_CHUNK
    npair = N_HEADS_A // PAIR
    lane = lax.broadcasted_iota(jnp.int32, (1, LANES), 1)
    head_lo = lane < HEAD_DIM
    ones_bd = _ones_bd()
    bd_mask = (lax.broadcasted_iota(jnp.int32, (LANES, LANES), 0) // HEAD_DIM
               == lax.broadcasted_iota(jnp.int32, (LANES, LANES), 1) // HEAD_DIM)

    ii = lax.broadcasted_iota(jnp.int32, (GDN_CHUNK, LANES), 0)
    jl = lax.broadcasted_iota(jnp.int32, (GDN_CHUNK, LANES), 1) % HEAD_DIM
    incl = ii >= jl
    strict = ii > jl
    eye_pair = ii == jl
    eye_f = eye_pair.astype(F32)

    def qkv_cols(p):
        return [slice(t * A_WIDTH + p * LANES, t * A_WIDTH + (p + 1) * LANES) for t in range(3)]

    def prep_taps(rows, p):
        return _dot(shift_ref[...], jnp.concatenate([pa_ref[0, rows, c] for c in qkv_cols(p)], axis=1))

    def prep_rest(i, rows, taps):
        r0 = i * GDN_ROWS
        halo_rows = pl.ds(pl.multiple_of(jnp.maximum(r0 - GDN_HALO, 0), GDN_HALO), GDN_HALO)
        halo = pa_ref[0, halo_rows, 0:3 * A_WIDTH]
        halo = jnp.where(i > 0, halo, jnp.zeros_like(halo))
        edge = _dot(halo_ref[...], halo)

        ba = ba_ref[0, rows, :]
        g = -jnp.exp(alog_ref[...]) * _softplus(ba + dtb_ref[...])
        g = jnp.where((lane >= N_HEADS_A) & (lane < 2 * N_HEADS_A), g, 0.0)
        hi, lo = _split_bf16(g)
        gc = _dot(cum_ref[...], jnp.concatenate([hi, lo], axis=0))
        nar = jnp.where(lane < N_HEADS_A, _sigmoid(ba), gc)
        hi, lo = _split_bf16(nar)
        wide = _dot(jnp.concatenate([hi, lo], axis=1), expand_ref[...])

        def conv_silu(p, t):
            c = qkv_cols(p)[t]
            w = cw_ref[:, c]
            y = w[CONV_WIDTH - 1:CONV_WIDTH, :] * pa_ref[0, rows, c].astype(F32)
            y0 = 0.0
            for kk in range(CONV_WIDTH - 1):
                y = y + w[kk:kk + 1, :] * taps[p][kk * GDN_ROWS:(kk + 1) * GDN_ROWS, t * LANES:(t + 1) * LANES]
                y0 = y0 + w[kk:kk + 1, :] * edge[kk * 8:(kk + 1) * 8, c]
            y = jnp.concatenate([y[:8, :] + y0, y[8:, :]], axis=0)
            return y * _sigmoid(y)

        ys = [[conv_silu(p, t) for t in range(3)] for p in range(npair)]
        ss = [_dot(jnp.concatenate([y[0] * y[0], y[1] * y[1]], axis=1).astype(BF16), bd4_ref[...]) for y in ys]
        for p in range(npair):
            qn_s[p, rows, :] = ys[p][0] * lax.rsqrt(ss[p][:, :LANES] + RMS_EPS) * (HEAD_DIM ** -0.5)
            kn_s[p, rows, :] = ys[p][1] * lax.rsqrt(ss[p][:, LANES:] + RMS_EPS)
            v_s[p, rows, :] = ys[p][2]
            be_s[p, rows, :] = wide[:, 2 * p * LANES:(2 * p + 1) * LANES]
            gc_s[p, rows, :] = wide[:, (2 * p + 1) * LANES:(2 * p + 2) * LANES]

    def bd(ms):
        return [_block_diag(m, head_lo) for m in ms]

    def chunk_group(p, i, _):
        cs = [i * GDN_GROUP + gi for gi in range(GDN_GROUP)]
        rows = [pl.ds(pl.multiple_of(c * GDN_CHUNK, GDN_CHUNK), GDN_CHUNK) for c in cs]
        qn = [qn_s[p, r, :] for r in rows]
        kn = [kn_s[p, r, :] for r in rows]
        v = [v_s[p, r, :] for r in rows]
        be = [be_s[p, r, :] for r in rows]
        gc = [gc_s[p, r, :] for r in rows]
        kb = [a * b for a, b in zip(kn, be)]
        ap = [_dot_nt(jnp.concatenate([a, b], axis=0).astype(BF16), m) for a, b, m in zip(kb, qn, bd(kn))]
        g_row = [jnp.sum(jnp.where(eye_pair, x, 0.0), axis=0, keepdims=True) for x in gc]
        decay = [jnp.exp(jnp.where(incl, x - y, NEG)) for x, y in zip(gc, g_row)]
        x = [jnp.where(strict, -a[:GDN_CHUNK] * d, 0.0) for a, d in zip(ap, decay)]
        pm = [a[GDN_CHUNK:] * d for a, d in zip(ap, decay)]
        tm = [eye_f + a for a in x]
        xr = [_dot(a.astype(BF16), m) for a, m in zip(x, bd(x))]
        for r in range(1, 6):
            rhs = bd(xr)
            if r < 5:
                y = [_dot(jnp.concatenate([a, b], axis=0).astype(BF16), m) for a, b, m in zip(xr, tm, rhs)]
                xr = [a[:GDN_CHUNK] for a in y]
                tm = [a + b[GDN_CHUNK:] for a, b in zip(tm, y)]
            else:
                tm = [a + _dot(a.astype(BF16), m) for a, m in zip(tm, rhs)]
        eg = [jnp.exp(a) for a in gc]
        vb = bd([a * b for a, b in zip(v, be)])
        kbg = bd([a * b for a, b in zip(kb, eg)])
        uw = [_dot(a.astype(BF16), jnp.concatenate([b, c], axis=1)) for a, b, c in zip(tm, vb, kbg)]
        u_bd = bd([a[:, :LANES] for a in uw])
        w_bd = bd([a[:, LANES:] for a in uw])
        puw = [_dot(a.astype(BF16), jnp.concatenate([b, c], axis=1)) for a, b, c in zip(pm, u_bd, w_bd)]
        g_last = [a[GDN_CHUNK - 1:GDN_CHUNK, :] for a in gc]
        kg = [a * jnp.exp(b - c) for a, b, c in zip(kn, g_last, gc)]
        kuw = [_dot_tn(a.astype(BF16), b.astype(BF16)) for a, b in zip(kg, uw)]
        for c, r, q, e, a, b, gl in zip(cs, rows, qn, eg, puw, kuw, g_last):
            qt_s[p, r, :] = (q * e - a[:, LANES:]).astype(BF16)
            o0_s[p, r, :] = a[:, :LANES]
            nf_s[p, c] = jnp.where(bd_mask, b[:, :LANES], 0.0)
            kw_s[p, c] = jnp.where(bd_mask, b[:, LANES:], 0.0).astype(BF16)
            gl_s[p, c] = jnp.broadcast_to(jnp.exp(gl), (8, LANES))
        return 0

    def per_pair(p, _):
        lax.fori_loop(0, n_chunks // GDN_GROUP, functools.partial(chunk_group, p), 0)
        return 0

    step = pl.program_id(0)
    last_step = pl.num_programs(0) - 1
    steps_per_rows = GDN_ROWS // GDN_CHUNK

    @pl.when(step == 0)
    def _():
        def zero(c, _):
            rows = pl.ds(pl.multiple_of(c * GDN_CHUNK, GDN_CHUNK), GDN_CHUNK)
            zg_s[rows, :] = jnp.zeros((GDN_CHUNK, A_WIDTH), BF16)
            for p in range(npair):
                qt_s[p, rows, :] = jnp.zeros((GDN_CHUNK, LANES), BF16)
                o0_s[p, rows, :] = jnp.zeros((GDN_CHUNK, LANES), F32)
                nf_s[p, c] = jnp.zeros((LANES, LANES), F32)
                kw_s[p, c] = jnp.zeros((LANES, LANES), BF16)
                gl_s[p, c] = jnp.zeros((8, LANES), F32)
            return 0

        lax.fori_loop(0, n_chunks, zero, 0)

    def rows_step(with_prep, i, states):
        rows = pl.ds(pl.multiple_of(i * GDN_ROWS, GDN_ROWS), GDN_ROWS)
        outs = [[] for _ in range(npair)]
        taps = []
        for s in range(steps_per_rows):
            c = i * steps_per_rows + s
            crow = pl.ds(pl.multiple_of(c * GDN_CHUNK, GDN_CHUNK), GDN_CHUNK)
            sb = [st.astype(BF16) for st in states]
            ks = [_dot(kw_s[p, c], sb[p]) for p in range(npair)]
            for p in range(npair):
                outs[p].append(_dot(qt_s[p, crow, :], sb[p]) + o0_s[p, crow, :])
            states = tuple(gl_s[p, c][0:1, :] * states[p] + nf_s[p, c] - ks[p] for p in range(npair))
            if with_prep and s < npair:
                taps.append(prep_taps(rows, s))
        for p in range(npair):
            o = jnp.concatenate(outs[p], axis=0)
            ms = _head_sums(o * o, ones_bd) * (1.0 / HEAD_DIM)
            y = o * lax.rsqrt(ms + RMS_EPS) * gnw_ref[...]
            zg = zg_s[rows, p * LANES:(p + 1) * LANES].astype(F32)
            o_ref[0, rows, p * LANES:(p + 1) * LANES] = (y * (zg * _sigmoid(zg))).astype(BF16)
        if with_prep:
            prep_rest(i, rows, taps)
            zg_s[rows, :] = pa_ref[0, rows, 3 * A_WIDTH:4 * A_WIDTH]
        return states

    zero_states = tuple(jnp.zeros((LANES, LANES), F32) for _ in range(npair))

    @pl.when(step < last_step)
    def _():
        lax.fori_loop(0, seq // GDN_ROWS, functools.partial(rows_step, True), zero_states)
        lax.fori_loop(0, npair, per_pair, 0)

    @pl.when(step == last_step)
    def _():
        lax.fori_loop(0, seq // GDN_ROWS, functools.partial(rows_step, False), zero_states)


def _gdn(pa, ba, conv_w, a_log, dt_bias, gdn_norm_w):
    bsz, seq, _ = pa.shape
    npair = N_HEADS_A // PAIR
    n_chunks = seq // GDN_CHUNK
    pad = lambda vec: jnp.zeros((1, LANES), F32).at[0, N_HEADS_A:2 * N_HEADS_A].set(vec.astype(F32))
    gnw = jnp.tile(gdn_norm_w.astype(F32), PAIR)[None, :]
    fixed = lambda shape: pl.BlockSpec(shape, lambda b: (0,) * len(shape))
    per_pair_f32 = pltpu.VMEM((npair, seq, LANES), F32)
    consts = _gdn_constants()
    return pl.pallas_call(
        _gdn_kernel,
        grid=(bsz + 1,),
        in_specs=[pl.BlockSpec((1, seq, PA_COLS), lambda b: (jnp.minimum(b, bsz - 1), 0, 0)),
                  pl.BlockSpec((1, seq, BA_COLS), lambda b: (jnp.minimum(b, bsz - 1), 0, 0)),
                  fixed((CONV_WIDTH, 3 * A_WIDTH)), fixed((1, LANES)), fixed((1, LANES)), fixed((1, LANES))]
        + [fixed(m.shape) for m in consts],
        out_specs=pl.BlockSpec((1, seq, A_WIDTH), lambda b: (jnp.maximum(b - 1, 0), 0, 0)),
        out_shape=jax.ShapeDtypeStruct((bsz, seq, A_WIDTH), BF16),
        scratch_shapes=[
            per_pair_f32, per_pair_f32, per_pair_f32,
            per_pair_f32, per_pair_f32,
            pltpu.VMEM((npair, seq, LANES), BF16),
            per_pair_f32,
            pltpu.VMEM((npair, n_chunks, LANES, LANES), F32),
            pltpu.VMEM((npair, n_chunks, LANES, LANES), BF16),
            pltpu.VMEM((npair, n_chunks, 8, LANES), F32),
            pltpu.VMEM((seq, A_WIDTH), BF16),
        ],
        compiler_params=pltpu.CompilerParams(dimension_semantics=("arbitrary",), vmem_limit_bytes=VMEM_LIMIT),
        name="gated_delta",
    )(pa, ba, conv_w.astype(F32), pad(a_log), pad(dt_bias), gnw, *consts)


ROW_TILE = 512


def kernel(x, norm_w, w_in, conv_w, a_log, dt_bias, gdn_norm_w, q_norm_w, k_norm_w, w_out):
    bsz, seq, _ = x.shape
    assert w_in.shape[1:] == (D_MODEL, IN_COLS)
    shape3 = lambda t: t.reshape(bsz, seq, t.shape[-1])
    flat = lambda t: t.reshape(bsz * seq, t.shape[-1])
    x2d = flat(x)
    w_in = w_in.astype(F32)
    norm_w = norm_w.astype(F32)
    for layer in range(norm_w.shape[0]):
        pa, ba, pb, pc = _in_proj(x2d, norm_w[layer][None, :], w_in, layer, ROW_TILE)
        oa = _gdn(shape3(pa), shape3(ba), conv_w[layer], a_log[layer], dt_bias[layer], gdn_norm_w[layer])
        ob = _stick_breaking(shape3(pb))
        oc = _dilated(shape3(pc), q_norm_w[layer], k_norm_w[layer])
        x2d = _out_proj(x2d, flat(oa), flat(ob), flat(oc), w_out[layer].astype(BF16), ROW_TILE)
    return shape3(x2d)
```

```python
import functools

import jax
import jax.numpy as jnp
from jax import lax
from jax.experimental import pallas as pl
from jax.experimental.pallas import tpu as pltpu

F32 = jnp.float32
BF16 = jnp.bfloat16

D_MODEL = 1024
HEAD_DIM = 64
N_HEADS_A, N_HEADS_B, N_HEADS_C = 6, 4, 6
A_WIDTH, B_WIDTH, C_WIDTH = N_HEADS_A * HEAD_DIM, N_HEADS_B * HEAD_DIM, N_HEADS_C * HEAD_DIM
CONV_WIDTH = 4
GDN_CHUNK = 64
ROPE_DIM = HEAD_DIM // 4
ROPE_THETA = 500000.0
DILATIONS = (1, 4, 16)
RMS_EPS = 1e-6

LANES = 128
PAIR = LANES // HEAD_DIM
NEG = -1e30

PA_COLS = 4 * A_WIDTH
BA_COLS = LANES
PB_COLS = 4 * B_WIDTH
PC_COLS = 4 * C_WIDTH
IN_COLS = PA_COLS + 2 * N_HEADS_A + PB_COLS + PC_COLS
W1_COLS = PA_COLS + BA_COLS + PB_COLS + PC_COLS

VMEM_LIMIT = 56 * 1024 * 1024

LOG2E = 1.4426950408889634
QSCALE2 = HEAD_DIM ** -0.5 * LOG2E


def _dot(a, b):
    return jnp.dot(a, b, preferred_element_type=F32)


def _dot_nt(a, b):
    return lax.dot_general(a, b, (((1,), (1,)), ((), ())), preferred_element_type=F32)


def _dot_tn(a, b):
    return lax.dot_general(a, b, (((0,), (0,)), ((), ())), preferred_element_type=F32)


def _aligned(x, m):
    return x if isinstance(x, int) else pl.multiple_of(x, m)


def _sigmoid(x):
    return 1.0 / (1.0 + jnp.exp(-x))


def _softplus(x):
    return jnp.maximum(x, 0.0) + jnp.log(1.0 + jnp.exp(-jnp.abs(x)))


def _split_bf16(x):
    hi = x.astype(BF16)
    lo = (x - hi.astype(F32)).astype(BF16)
    return hi, lo


def _head_lo(shape):
    return lax.broadcasted_iota(jnp.int32, shape, len(shape) - 1) < HEAD_DIM


def _block_diag(m, head_lo):
    z = jnp.zeros_like(m)
    return jnp.concatenate([jnp.where(head_lo, m, z), jnp.where(head_lo, z, m)], axis=0).astype(BF16)


def _head_sums(x2, ones_bd):
    return _dot(x2.astype(BF16), ones_bd)


def _ones_bd():
    r = lax.broadcasted_iota(jnp.int32, (LANES, LANES), 0) // HEAD_DIM
    c = lax.broadcasted_iota(jnp.int32, (LANES, LANES), 1) // HEAD_DIM
    return (r == c).astype(BF16)


W_ROWS = 128


def _in_proj_kernel(x_ref, nw_ref, w_ref, pa_ref, ba_ref, pb_ref, pc_ref, wb):
    @pl.when(pl.program_id(0) == 0)
    def _():
        def rows(i, _):
            r = pl.ds(pl.multiple_of(i * W_ROWS, W_ROWS), W_ROWS)
            wb[r, 0:PA_COLS + BA_COLS] = w_ref[0, r, 0:PA_COLS + BA_COLS].astype(BF16)
            wb[r, PA_COLS + BA_COLS:W1_COLS] = w_ref[0, r, PA_COLS + 2 * N_HEADS_A:IN_COLS].astype(BF16)
            return 0

        lax.fori_loop(0, D_MODEL // W_ROWS, rows, 0)

    x = x_ref[...]
    ms = jnp.mean(x * x, axis=-1, keepdims=True)
    h = (x * lax.rsqrt(ms + RMS_EPS) * nw_ref[...]).astype(BF16)
    c0 = 0
    pa_ref[...] = _dot(h, wb[:, c0:c0 + PA_COLS]).astype(BF16)
    c0 += PA_COLS
    ba_ref[...] = _dot(h, wb[:, c0:c0 + BA_COLS])
    c0 += BA_COLS
    col_b = lax.broadcasted_iota(jnp.int32, (1, PB_COLS), 1)
    pb_ref[...] = (_dot(h, wb[:, c0:c0 + PB_COLS]) * jnp.where(col_b < B_WIDTH, QSCALE2, 1.0)).astype(BF16)
    c0 += PB_COLS
    pc_ref[...] = _dot(h, wb[:, c0:c0 + PC_COLS]).astype(BF16)


def _in_proj(x2d, norm_w, w_in, layer, tm):
    n = x2d.shape[0]
    row = lambda i: (i, 0)
    return pl.pallas_call(
        _in_proj_kernel,
        grid=(n // tm,),
        in_specs=[pl.BlockSpec((tm, D_MODEL), row),
                  pl.BlockSpec((1, D_MODEL), lambda i: (0, 0)),
                  pl.BlockSpec((1, D_MODEL, IN_COLS), lambda i: (layer, 0, 0), pipeline_mode=pl.Buffered(1))],
        out_specs=[pl.BlockSpec((tm, PA_COLS), row), pl.BlockSpec((tm, BA_COLS), row),
                   pl.BlockSpec((tm, PB_COLS), row), pl.BlockSpec((tm, PC_COLS), row)],
        out_shape=[jax.ShapeDtypeStruct((n, PA_COLS), BF16), jax.ShapeDtypeStruct((n, BA_COLS), F32),
                   jax.ShapeDtypeStruct((n, PB_COLS), BF16), jax.ShapeDtypeStruct((n, PC_COLS), BF16)],
        scratch_shapes=[pltpu.VMEM((D_MODEL, W1_COLS), BF16)],
        compiler_params=pltpu.CompilerParams(dimension_semantics=("arbitrary",), vmem_limit_bytes=VMEM_LIMIT),
        name="in_proj",
    )(x2d, norm_w, w_in)


def _out_proj_kernel(x_ref, oa_ref, ob_ref, oc_ref, w_ref, o_ref):
    acc = _dot(oa_ref[...], w_ref[0:A_WIDTH, :])
    acc = acc + _dot(ob_ref[...], w_ref[A_WIDTH:A_WIDTH + B_WIDTH, :])
    acc = acc + _dot(oc_ref[...], w_ref[A_WIDTH + B_WIDTH:, :])
    o_ref[...] = x_ref[...] + acc


def _out_proj(x2d, oa, ob, oc, w_out, tm):
    n = x2d.shape[0]
    row = lambda i: (i, 0)
    fixed = lambda i: (0, 0)
    return pl.pallas_call(
        _out_proj_kernel,
        grid=(n // tm,),
        in_specs=[pl.BlockSpec((tm, D_MODEL), row), pl.BlockSpec((tm, A_WIDTH), row),
                  pl.BlockSpec((tm, B_WIDTH), row), pl.BlockSpec((tm, C_WIDTH), row),
                  pl.BlockSpec((D_MODEL, D_MODEL), fixed)],
        out_specs=pl.BlockSpec((tm, D_MODEL), row),
        out_shape=jax.ShapeDtypeStruct((n, D_MODEL), F32),
        compiler_params=pltpu.CompilerParams(dimension_semantics=("arbitrary",), vmem_limit_bytes=VMEM_LIMIT),
        name="out_proj",
    )(x2d, oa, ob, oc, w_out)


SB_BLOCK = 256


def _sb_kernel(q_ref, k_ref, v_ref, z_ref, o_ref):
    seq = q_ref.shape[1]
    nb = seq // SB_BLOCK
    head_lo = _head_lo((1, LANES))
    row = lax.broadcasted_iota(jnp.int32, (SB_BLOCK, SB_BLOCK), 0)
    col = lax.broadcasted_iota(jnp.int32, (SB_BLOCK, SB_BLOCK), 1)
    earlier = col < row
    neg_suffix = jnp.where(row >= col, -1.0, 0.0).astype(BF16)

    def rows(b):
        return pl.ds(b * SB_BLOCK, SB_BLOCK)

    def scores(i, kt):
        q = q_ref[0, rows(i), :]
        zero = jnp.zeros_like(q)
        k = k_ref[0, rows(kt), :]
        return [_dot_nt(qh, k) for qh in (jnp.where(head_lo, q, zero), jnp.where(head_lo, zero, q))]

    def log_weights(z2, diag):
        nlk = [jnp.maximum(x, 0.0) + jnp.log2(1.0 + jnp.exp2(-jnp.abs(x))) for x in z2]
        if diag:
            nlk = [jnp.where(earlier, x, 0.0) for x in nlk]
        here_on = [_dot(x.astype(BF16), neg_suffix) for x in nlk]
        pre = [x + w for x, w in zip(z2, here_on)]
        if diag:
            pre = [jnp.where(earlier, x, NEG) for x in pre]
        return pre, [w[:, 0:1] for w in here_on]

    def accumulate(pre, carry, acc, kt):
        v = v_ref[0, rows(kt), :]
        w = [jnp.exp2(x + c).astype(BF16) for x, c in zip(pre, carry)]
        return [a + _dot(x, v) for a, x in zip(acc, w)]

    tiles = [(i, kt) for i in range(nb) for kt in range(i, -1, -1)]
    z2, lw = {}, {}
    carry = acc = None
    for step in range(len(tiles) + 2):
        if step < len(tiles):
            z2[step] = scores(*tiles[step])
        if 0 <= step - 1 < len(tiles):
            i, kt = tiles[step - 1]
            lw[step - 1] = log_weights(z2.pop(step - 1), i == kt)
        if 0 <= step - 2 < len(tiles):
            i, kt = tiles[step - 2]
            pre, tot = lw.pop(step - 2)
            if i == kt:
                carry = [jnp.zeros((SB_BLOCK, 1), F32)] * PAIR
                acc = [jnp.zeros((SB_BLOCK, LANES), F32)] * PAIR
            acc = accumulate(pre, carry, acc, kt)
            carry = [c + t for c, t in zip(carry, tot)]
            if kt == 0:
                o = jnp.where(head_lo, acc[0], acc[1])
                zg = z_ref[0, rows(i), :].astype(F32)
                o_ref[0, rows(i), :] = (o * (zg * _sigmoid(zg))).astype(BF16)


def _stick_breaking(pb):
    bsz, seq, _ = pb.shape
    npair = N_HEADS_B // PAIR
    spec = lambda g: pl.BlockSpec((1, seq, LANES), lambda b, p, g=g: (b, 0, g * npair + p))
    return pl.pallas_call(
        _sb_kernel,
        grid=(bsz, npair),
        in_specs=[spec(0), spec(1), spec(2), spec(3)],
        out_specs=pl.BlockSpec((1, seq, LANES), lambda b, p: (b, 0, p)),
        out_shape=jax.ShapeDtypeStruct((bsz, seq, B_WIDTH), BF16),
        compiler_params=pltpu.CompilerParams(dimension_semantics=("arbitrary", "arbitrary"),
                                             vmem_limit_bytes=VMEM_LIMIT),
        name="stick_breaking",
    )(pb, pb, pb, pb)


DW_BLOCK = 128
DW_SLOT = 2


def _dw_kernel(q_ref, k_ref, v_ref, z_ref, qw_ref, kw_ref, cos_ref, sina_ref, sinb_ref, o_ref,
               qf, kf, vf, q4f, k4f, v4f, qn, kn, qg4, kg4, vg4, qg16, kg16, vg16,
               m1_md, m2_md, l_md, acc_md, m1_st, m2_st, l_st, acc_st):
    seq = q_ref.shape[1]
    head_lo = _head_lo((1, LANES))
    ones_bd = _ones_bd()
    n_rows = 256

    def prep(i, _):
        blocks = [pl.ds(pl.multiple_of((2 * i + b) * n_rows, n_rows), n_rows) for b in range(2)]
        xs = [(ref[0, rows, :].astype(F32), w_ref[...], rows)
              for rows in blocks for ref, w_ref in ((q_ref, qw_ref), (k_ref, kw_ref))]
        ms = [_head_sums(x * x, ones_bd) * (1.0 / HEAD_DIM) for x, _, _ in xs]
        ys = [x * lax.rsqrt(m + RMS_EPS) * w for (x, w, _), m in zip(xs, ms)]
        ys = [y * cos_ref[rows, :] + pltpu.roll(y, LANES - ROPE_DIM // 2, 1) * sina_ref[rows, :]
              + pltpu.roll(y, ROPE_DIM // 2, 1) * sinb_ref[rows, :] for y, (_, _, rows) in zip(ys, xs)]
        for b, rows in enumerate(blocks):
            q, k = ys[2 * b] * QSCALE2, ys[2 * b + 1]
            qf[rows, :] = q
            kf[rows, :] = k
            qn[rows, :] = q.astype(BF16)
            kn[rows, :] = k.astype(BF16)
            vf[rows, :] = v_ref[0, rows, :].astype(F32)
        return 0

    lax.fori_loop(0, seq // (2 * n_rows), prep, 0)

    qi = lax.broadcasted_iota(jnp.int32, (DW_BLOCK, DW_BLOCK), 0)
    kj = lax.broadcasted_iota(jnp.int32, (DW_BLOCK, DW_BLOCK), 1)
    own_ok = kj <= qi
    prev_ok = kj >= qi
    both_ok = jnp.concatenate([prev_ok, own_ok], axis=1)

    def pair_of(a, b):
        return jnp.where(head_lo, a, b)

    d_mid, d_far = DILATIONS[1], DILATIONS[2]
    len_mid, len_far = seq // d_mid, seq // d_far
    for src, mid_f, mid_b, far_b in ((qf, q4f, qg4, qg16), (kf, k4f, kg4, kg16), (vf, v4f, vg4, vg16)):
        for r in range(d_mid):
            x = src[pl.ds(r, len_mid, stride=d_mid), :]
            mid_f[pl.ds(r * len_mid, len_mid), :] = x
            mid_b[pl.ds(r * len_mid, len_mid), :] = x.astype(BF16)
        for r in range(d_far):
            x = mid_f[pl.ds((r % d_mid) * len_mid + r // d_mid, len_far, stride=d_far // d_mid), :]
            far_b[pl.ds(r * len_far, len_far), :] = x.astype(BF16)
    sources = {d_far: (qg16, kg16, lambda ks: vg16[ks, :]), d_mid: (qg4, kg4, lambda ks: vg4[ks, :]),
               1: (qn, kn, lambda ks: v_ref[0, ks, :])}

    state_mid, state_nat = (m1_md, m2_md, l_md, acc_md), (m1_st, m2_st, l_st, acc_st)

    def nat_rows(it):
        return pl.ds(it["start"], DW_BLOCK, stride=it["d"]) if it["d"] > 1 else pl.ds(it["start"], DW_BLOCK)

    def state_in(it):
        if it["d"] == d_mid:
            return state_mid, pl.ds(it["c0"], DW_BLOCK)
        return state_nat, nat_rows(it)

    def state_out(it):
        if it["d"] == d_far:
            r = it["start"]
            return state_mid, pl.ds((r % d_mid) * len_mid + r // d_mid, DW_BLOCK, stride=d_far // d_mid)
        return state_nat, nat_rows(it)

    def stage_scores(it):
        q_src, k_src, _ = sources[it["d"]]
        c0 = it["c0"]
        keys = pl.ds(c0 - DW_BLOCK, 2 * DW_BLOCK) if it["has_prev"] else pl.ds(c0, DW_BLOCK)
        ok = both_ok if it["has_prev"] else own_ok
        q = q_src[pl.ds(c0, DW_BLOCK), :]
        zero = jnp.zeros_like(q)
        k = k_src[keys, :]
        s = [jnp.where(ok, _dot_nt(qh, k), NEG) for qh in (jnp.where(head_lo, q, zero), jnp.where(head_lo, zero, q))]
        return dict(s=s, m_blk=[jnp.max(sh, axis=1, keepdims=True) for sh in s], keys=keys)

    def stage_softmax(it, st):
        width = 2 if it["has_prev"] else 1
        if it["d"] == d_far:
            m_h = [jnp.broadcast_to(m, (DW_BLOCK, LANES)) for m in st["m_blk"]]
            alpha = None
        else:
            (m1_in, m2_in, _, _), rows_in = state_in(it)
            m_old = (m1_in[rows_in, :], m2_in[rows_in, :])
            m_h = [jnp.maximum(o, b) for o, b in zip(m_old, st["m_blk"])]
            alpha = jnp.exp2(pair_of(m_old[0], m_old[1]) - pair_of(m_h[0], m_h[1]))
        p = [jnp.exp2(sh - jnp.concatenate([mh] * width, axis=1)).astype(BF16) for sh, mh in zip(st["s"], m_h)]
        return dict(p=p, m_h=m_h, alpha=alpha, keys=st["keys"])

    def stage_values(it, st):
        _, _, v_of = sources[it["d"]]
        v = v_of(st["keys"])
        vo = jnp.concatenate([v, jnp.ones_like(v)], axis=1)
        pvl = [_dot(ph, vo) for ph in st["p"]]
        pv = pair_of(pvl[0][:, :LANES], pvl[1][:, :LANES])
        l_new = pair_of(pvl[0][:, LANES:], pvl[1][:, LANES:])
        if st["alpha"] is not None:
            (_, _, l_in, acc_in), rows_in = state_in(it)
            l_new = st["alpha"] * l_in[rows_in, :] + l_new
            pv = st["alpha"] * acc_in[rows_in, :] + pv
        if it["d"] == 1:
            nat = nat_rows(it)
            zg = z_ref[0, nat, :].astype(F32)
            o_ref[0, nat, :] = (pv / l_new * (zg * _sigmoid(zg))).astype(BF16)
        else:
            (m1_out, m2_out, l_out, acc_out), rows_out = state_out(it)
            m1_out[rows_out, :] = st["m_h"][0]
            m2_out[rows_out, :] = st["m_h"][1]
            l_out[rows_out, :] = l_new
            acc_out[rows_out, :] = pv

    groups = [[dict(d=d_far, c0=r * len_far, start=r, has_prev=False) for r in range(d_far)],
              [dict(d=d_mid, c0=r * len_mid + j * DW_BLOCK, start=r + j * DW_BLOCK * d_mid, has_prev=j > 0)
               for j in range(len_mid // DW_BLOCK) for r in range(d_mid)],
              [dict(d=1, c0=j * DW_BLOCK, start=j * DW_BLOCK, has_prev=j > 0) for j in range(seq // DW_BLOCK)]]
    slots = []
    for g in groups:
        slots += [g[i:i + DW_SLOT] for i in range(0, len(g), DW_SLOT)] + [[]]
    slots += [[]]
    scored, soft = {}, {}
    for step, slot in enumerate(slots):
        if step >= 2:
            for n, it in enumerate(slots[step - 2]):
                stage_values(it, soft.pop((step - 2, n)))
        if step >= 1:
            for n, it in enumerate(slots[step - 1]):
                soft[step - 1, n] = stage_softmax(it, scored.pop((step - 1, n)))
        for n, it in enumerate(slot):
            scored[step, n] = stage_scores(it)


def _rope_tables(seq):
    half = ROPE_DIM // 2
    inv_freq = ROPE_THETA ** (-jnp.arange(half, dtype=F32) / half)
    ang = jnp.arange(seq, dtype=jnp.int32).astype(F32)[:, None] * inv_freq[None, :]
    cos, sin = jnp.cos(ang), jnp.sin(ang)
    ones = jnp.ones((seq, HEAD_DIM - ROPE_DIM), F32)
    zeros_h = jnp.zeros((seq, half), F32)
    zeros_t = jnp.zeros((seq, HEAD_DIM - ROPE_DIM), F32)
    cos_h = jnp.concatenate([cos, cos, ones], axis=1)
    sina_h = jnp.concatenate([-sin, zeros_h, zeros_t], axis=1)
    sinb_h = jnp.concatenate([zeros_h, sin, zeros_t], axis=1)
    tile = lambda t: jnp.tile(t, (1, PAIR))
    return tile(cos_h), tile(sina_h), tile(sinb_h)


def _dilated(pc, q_norm_w, k_norm_w):
    bsz, seq, _ = pc.shape
    npair = N_HEADS_C // PAIR
    assert seq % (DW_BLOCK * DILATIONS[-1]) == 0 and DILATIONS[-1] % DILATIONS[1] == 0
    cos, sina, sinb = _rope_tables(seq)
    qw = jnp.tile(q_norm_w.astype(F32), PAIR)[None, :]
    kw = jnp.tile(k_norm_w.astype(F32), PAIR)[None, :]
    spec = lambda g: pl.BlockSpec((1, seq, LANES), lambda b, p, g=g: (b, 0, g * npair + p))
    fixed = lambda shape: pl.BlockSpec(shape, lambda b, p: (0, 0))
    f32_rows, bf16_rows = pltpu.VMEM((seq, LANES), F32), pltpu.VMEM((seq, LANES), BF16)
    return pl.pallas_call(
        _dw_kernel,
        grid=(bsz, npair),
        in_specs=[spec(0), spec(1), spec(2), spec(3), fixed((1, LANES)), fixed((1, LANES)),
                  fixed((seq, LANES)), fixed((seq, LANES)), fixed((seq, LANES))],
        out_specs=pl.BlockSpec((1, seq, LANES), lambda b, p: (b, 0, p)),
        out_shape=jax.ShapeDtypeStruct((bsz, seq, C_WIDTH), BF16),
        scratch_shapes=[f32_rows] * 6 + [bf16_rows] * 8 + [f32_rows] * 8,
        compiler_params=pltpu.CompilerParams(dimension_semantics=("arbitrary", "arbitrary"),
                                             vmem_limit_bytes=VMEM_LIMIT),
        name="dilated_window",
    )(pc, pc, pc, pc, qw, kw, cos, sina, sinb)


GDN_ROWS = 256
GDN_HALO = 16
GDN_GROUP = 16


def _gdn_constants():
    npair = N_HEADS_A // PAIR
    iota = lambda shape, dim: lax.broadcasted_iota(jnp.int32, shape, dim)
    n_shift = CONV_WIDTH - 1
    sr, sc = iota((n_shift * GDN_ROWS, GDN_ROWS), 0), iota((n_shift * GDN_ROWS, GDN_ROWS), 1)
    shift = sc == sr % GDN_ROWS - (CONV_WIDTH - 1) + sr // GDN_ROWS
    hr, hc = iota((n_shift * 8, GDN_HALO), 0), iota((n_shift * 8, GDN_HALO), 1)
    halo = hc == GDN_HALO + hr % 8 - (CONV_WIDTH - 1) + hr // 8
    ri, ci = iota((GDN_ROWS, 2 * GDN_ROWS), 0), iota((GDN_ROWS, 2 * GDN_ROWS), 1) % GDN_ROWS
    cum = (ri // GDN_CHUNK == ci // GDN_CHUNK) & (ci <= ri)
    er, ec = iota((2 * LANES, 2 * npair * LANES), 0) % LANES, iota((2 * LANES, 2 * npair * LANES), 1)
    expand = er == (ec // LANES % 2) * N_HEADS_A + 2 * (ec // (2 * LANES)) + ec % LANES // HEAD_DIM
    bd4 = iota((2 * LANES, 2 * LANES), 0) // HEAD_DIM == iota((2 * LANES, 2 * LANES), 1) // HEAD_DIM
    return [m.astype(BF16) for m in (shift, halo, cum, expand, bd4)]


def _gdn_kernel(pa_ref, ba_ref, cw_ref, alog_ref, dtb_ref, gnw_ref, shift_ref, halo_ref, cum_ref, expand_ref,
                bd4_ref, o_ref, qn_s, kn_s, v_s, be_s, gc_s, qt_s, o0_s, nf_s, kw_s, gl_s, zg_s):
    seq = pa_ref.shape[1]
    n_chunks = seq // GDN_CHUNK
    npair = N_HEADS_A // PAIR
    lane = lax.broadcasted_iota(jnp.int32, (1, LANES), 1)
    head_lo = lane < HEAD_DIM
    ones_bd = _ones_bd()
    bd_mask = (lax.broadcasted_iota(jnp.int32, (LANES, LANES), 0) // HEAD_DIM
               == lax.broadcasted_iota(jnp.int32, (LANES, LANES), 1) // HEAD_DIM)

    ii = lax.broadcasted_iota(jnp.int32, (GDN_CHUNK, LANES), 0)
    jl = lax.broadcasted_iota(jnp.int32, (GDN_CHUNK, LANES), 1) % HEAD_DIM
    incl = ii >= jl
    strict = ii > jl
    eye_pair = ii == jl
    eye_f = eye_pair.astype(F32)

    def qkv_cols(p):
        return [slice(t * A_WIDTH + p * LANES, t * A_WIDTH + (p + 1) * LANES) for t in range(3)]

    def prep_taps(rows, p):
        return _dot(shift_ref[...], jnp.concatenate([pa_ref[0, rows, c] for c in qkv_cols(p)], axis=1))

    def prep_rest(i, rows, taps):
        r0 = i * GDN_ROWS
        halo_rows = pl.ds(pl.multiple_of(jnp.maximum(r0 - GDN_HALO, 0), GDN_HALO), GDN_HALO)
        halo = pa_ref[0, halo_rows, 0:3 * A_WIDTH]
        halo = jnp.where(i > 0, halo, jnp.zeros_like(halo))
        edge = _dot(halo_ref[...], halo)

        ba = ba_ref[0, rows, :]
        g = -jnp.exp(alog_ref[...]) * _softplus(ba + dtb_ref[...])
        g = jnp.where((lane >= N_HEADS_A) & (lane < 2 * N_HEADS_A), g, 0.0)
        hi, lo = _split_bf16(g)
        gc = _dot(cum_ref[...], jnp.concatenate([hi, lo], axis=0))
        nar = jnp.where(lane < N_HEADS_A, _sigmoid(ba), gc)
        hi, lo = _split_bf16(nar)
        wide = _dot(jnp.concatenate([hi, lo], axis=1), expand_ref[...])

        def conv_silu(p, t):
            c = qkv_cols(p)[t]
            w = cw_ref[:, c]
            y = w[CONV_WIDTH - 1:CONV_WIDTH, :] * pa_ref[0, rows, c].astype(F32)
            y0 = 0.0
            for kk in range(CONV_WIDTH - 1):
                y = y + w[kk:kk + 1, :] * taps[p][kk * GDN_ROWS:(kk + 1) * GDN_ROWS, t * LANES:(t + 1) * LANES]
                y0 = y0 + w[kk:kk + 1, :] * edge[kk * 8:(kk + 1) * 8, c]
            y = jnp.concatenate([y[:8, :] + y0, y[8:, :]], axis=0)
            return y * _sigmoid(y)

        ys = [[conv_silu(p, t) for t in range(3)] for p in range(npair)]
        ss = [_dot(jnp.concatenate([y[0] * y[0], y[1] * y[1]], axis=1).astype(BF16), bd4_ref[...]) for y in ys]
        for p in range(npair):
            qn_s[p, rows, :] = ys[p][0] * lax.rsqrt(ss[p][:, :LANES] + RMS_EPS) * (HEAD_DIM ** -0.5)
            kn_s[p, rows, :] = ys[p][1] * lax.rsqrt(ss[p][:, LANES:] + RMS_EPS)
            v_s[p, rows, :] = ys[p][2]
            be_s[p, rows, :] = wide[:, 2 * p * LANES:(2 * p + 1) * LANES]
            gc_s[p, rows, :] = wide[:, (2 * p + 1) * LANES:(2 * p + 2) * LANES]

    def bd(ms):
        return [_block_diag(m, head_lo) for m in ms]

    def chunk_group(p, i, _):
        cs = [i * GDN_GROUP + gi for gi in range(GDN_GROUP)]
        rows = [pl.ds(pl.multiple_of(c * GDN_CHUNK, GDN_CHUNK), GDN_CHUNK) for c in cs]
        qn = [qn_s[p, r, :] for r in rows]
        kn = [kn_s[p, r, :] for r in rows]
        v = [v_s[p, r, :] for r in rows]
        be = [be_s[p, r, :] for r in rows]
        gc = [gc_s[p, r, :] for r in rows]
        kb = [a * b for a, b in zip(kn, be)]
        ap = [_dot_nt(jnp.concatenate([a, b], axis=0).astype(BF16), m) for a, b, m in zip(kb, qn, bd(kn))]
        g_row = [jnp.sum(jnp.where(eye_pair, x, 0.0), axis=0, keepdims=True) for x in gc]
        decay = [jnp.exp(jnp.where(incl, x - y, NEG)) for x, y in zip(gc, g_row)]
        x = [jnp.where(strict, -a[:GDN_CHUNK] * d, 0.0) for a, d in zip(ap, decay)]
        pm = [a[GDN_CHUNK:] * d for a, d in zip(ap, decay)]
        tm = [eye_f + a for a in x]
        xr = [_dot(a.astype(BF16), m) for a, m in zip(x, bd(x))]
        for r in range(1, 6):
            rhs = bd(xr)
            if r < 5:
                y = [_dot(jnp.concatenate([a, b], axis=0).astype(BF16), m) for a, b, m in zip(xr, tm, rhs)]
                xr = [a[:GDN_CHUNK] for a in y]
                tm = [a + b[GDN_CHUNK:] for a, b in zip(tm, y)]
            else:
                tm = [a + _dot(a.astype(BF16), m) for a, m in zip(tm, rhs)]
        eg = [jnp.exp(a) for a in gc]
        vb = bd([a * b for a, b in zip(v, be)])
        kbg = bd([a * b for a, b in zip(kb, eg)])
        uw = [_dot(a.astype(BF16), jnp.concatenate([b, c], axis=1)) for a, b, c in zip(tm, vb, kbg)]
        u_bd = bd([a[:, :LANES] for a in uw])
        w_bd = bd([a[:, LANES:] for a in uw])
        puw = [_dot(a.astype(BF16), jnp.concatenate([b, c], axis=1)) for a, b, c in zip(pm, u_bd, w_bd)]
        g_last = [a[GDN_CHUNK - 1:GDN_CHUNK, :] for a in gc]
        kg = [a * jnp.exp(b - c) for a, b, c in zip(kn, g_last, gc)]
        kuw = [_dot_tn(a.astype(BF16), b.astype(BF16)) for a, b in zip(kg, uw)]
        for c, r, q, e, a, b, gl in zip(cs, rows, qn, eg, puw, kuw, g_last):
            qt_s[p, r, :] = (q * e - a[:, LANES:]).astype(BF16)
            o0_s[p, r, :] = a[:, :LANES]
            nf_s[p, c] = jnp.where(bd_mask, b[:, :LANES], 0.0)
            kw_s[p, c] = jnp.where(bd_mask, b[:, LANES:], 0.0).astype(BF16)
            gl_s[p, c] = jnp.broadcast_to(jnp.exp(gl), (8, LANES))
        return 0

    def per_pair(p, _):
        lax.fori_loop(0, n_chunks // GDN_GROUP, functools.partial(chunk_group, p), 0)
        return 0

    step = pl.program_id(0)
    last_step = pl.num_programs(0) - 1
    steps_per_rows = GDN_ROWS // GDN_CHUNK

    @pl.when(step == 0)
    def _():
        def zero(c, _):
            rows = pl.ds(pl.multiple_of(c * GDN_CHUNK, GDN_CHUNK), GDN_CHUNK)
            zg_s[rows, :] = jnp.zeros((GDN_CHUNK, A_WIDTH), BF16)
            for p in range(npair):
                qt_s[p, rows, :] = jnp.zeros((GDN_CHUNK, LANES), BF16)
                o0_s[p, rows, :] = jnp.zeros((GDN_CHUNK, LANES), F32)
                nf_s[p, c] = jnp.zeros((LANES, LANES), F32)
                kw_s[p, c] = jnp.zeros((LANES, LANES), BF16)
                gl_s[p, c] = jnp.zeros((8, LANES), F32)
            return 0

        lax.fori_loop(0, n_chunks, zero, 0)

    def rows_step(with_prep, i, states):
        rows = pl.ds(pl.multiple_of(i * GDN_ROWS, GDN_ROWS), GDN_ROWS)
        outs = [[] for _ in range(npair)]
        taps = []
        for s in range(steps_per_rows):
            c = i * steps_per_rows + s
            crow = pl.ds(pl.multiple_of(c * GDN_CHUNK, GDN_CHUNK), GDN_CHUNK)
            sb = [st.astype(BF16) for st in states]
            ks = [_dot(kw_s[p, c], sb[p]) for p in range(npair)]
            for p in range(npair):
                outs[p].append(_dot(qt_s[p, crow, :], sb[p]) + o0_s[p, crow, :])
            states = tuple(gl_s[p, c][0:1, :] * states[p] + nf_s[p, c] - ks[p] for p in range(npair))
            if with_prep and s < npair:
                taps.append(prep_taps(rows, s))
        for p in range(npair):
            o = jnp.concatenate(outs[p], axis=0)
            ms = _head_sums(o * o, ones_bd) * (1.0 / HEAD_DIM)
            y = o * lax.rsqrt(ms + RMS_EPS) * gnw_ref[...]
            zg = zg_s[rows, p * LANES:(p + 1) * LANES].astype(F32)
            o_ref[0, rows, p * LANES:(p + 1) * LANES] = (y * (zg * _sigmoid(zg))).astype(BF16)
        if with_prep:
            prep_rest(i, rows, taps)
            zg_s[rows, :] = pa_ref[0, rows, 3 * A_WIDTH:4 * A_WIDTH]
        return states

    zero_states = tuple(jnp.zeros((LANES, LANES), F32) for _ in range(npair))

    @pl.when(step < last_step)
    def _():
        lax.fori_loop(0, seq // GDN_ROWS, functools.partial(rows_step, True), zero_states)
        lax.fori_loop(0, npair, per_pair, 0)

    @pl.when(step == last_step)
    def _():
        lax.fori_loop(0, seq // GDN_ROWS, functools.partial(rows_step, False), zero_states)


def _gdn(pa, ba, conv_w, a_log, dt_bias, gdn_norm_w):
    bsz, seq, _ = pa.shape
    npair = N_HEADS_A // PAIR
    n_chunks = seq // GDN_CHUNK
    pad = lambda vec: jnp.zeros((1, LANES), F32).at[0, N_HEADS_A:2 * N_HEADS_A].set(vec.astype(F32))
    gnw = jnp.tile(gdn_norm_w.astype(F32), PAIR)[None, :]
    fixed = lambda shape: pl.BlockSpec(shape, lambda b: (0,) * len(shape))
    per_pair_f32 = pltpu.VMEM((npair, seq, LANES), F32)
    consts = _gdn_constants()
    return pl.pallas_call(
        _gdn_kernel,
        grid=(bsz + 1,),
        in_specs=[pl.BlockSpec((1, seq, PA_COLS), lambda b: (jnp.minimum(b, bsz - 1), 0, 0)),
                  pl.BlockSpec((1, seq, BA_COLS), lambda b: (jnp.minimum(b, bsz - 1), 0, 0)),
                  fixed((CONV_WIDTH, 3 * A_WIDTH)), fixed((1, LANES)), fixed((1, LANES)), fixed((1, LANES))]
        + [fixed(m.shape) for m in consts],
        out_specs=pl.BlockSpec((1, seq, A_WIDTH), lambda b: (jnp.maximum(b - 1, 0), 0, 0)),
        out_shape=jax.ShapeDtypeStruct((bsz, seq, A_WIDTH), BF16),
        scratch_shapes=[
            per_pair_f32, per_pair_f32, per_pair_f32,
            per_pair_f32, per_pair_f32,
            pltpu.VMEM((npair, seq, LANES), BF16),
            per_pair_f32,
            pltpu.VMEM((npair, n_chunks, LANES, LANES), F32),
            pltpu.VMEM((npair, n_chunks, LANES, LANES), BF16),
            pltpu.VMEM((npair, n_chunks, 8, LANES), F32),
            pltpu.VMEM((seq, A_WIDTH), BF16),
        ],
        compiler_params=pltpu.CompilerParams(dimension_semantics=("arbitrary",), vmem_limit_bytes=VMEM_LIMIT),
        name="gated_delta",
    )(pa, ba, conv_w.astype(F32), pad(a_log), pad(dt_bias), gnw, *consts)


ROW_TILE = 512
OUT_ROW_TILE = 1024


def kernel(x, norm_w, w_in, conv_w, a_log, dt_bias, gdn_norm_w, q_norm_w, k_norm_w, w_out):
    bsz, seq, _ = x.shape
    assert w_in.shape[1:] == (D_MODEL, IN_COLS)
    shape3 = lambda t: t.reshape(bsz, seq, t.shape[-1])
    flat = lambda t: t.reshape(bsz * seq, t.shape[-1])
    x2d = flat(x)
    w_in = w_in.astype(F32)
    norm_w = norm_w.astype(F32)
    for layer in range(norm_w.shape[0]):
        pa, ba, pb, pc = _in_proj(x2d, norm_w[layer][None, :], w_in, layer, ROW_TILE)
        oa = _gdn(shape3(pa), shape3(ba), conv_w[layer], a_log[layer], dt_bias[layer], gdn_norm_w[layer])
        ob = _stick_breaking(shape3(pb))
        oc = _dilated(shape3(pc), q_norm_w[layer], k_norm_w[layer])
        x2d = _out_proj(x2d, flat(oa), flat(ob), flat(oc), w_out[layer].astype(BF16), OUT_ROW_TILE)
    return shape3(x2d)
```

```python
import functools

import jax
import jax.numpy as jnp
from jax import lax
from jax.experimental import pallas as pl
from jax.experimental.pallas import tpu as pltpu

F32 = jnp.float32
BF16 = jnp.bfloat16

D_MODEL = 1024
HEAD_DIM = 64
N_HEADS_A, N_HEADS_B, N_HEADS_C = 6, 4, 6
A_WIDTH, B_WIDTH, C_WIDTH = N_HEADS_A * HEAD_DIM, N_HEADS_B * HEAD_DIM, N_HEADS_C * HEAD_DIM
CONV_WIDTH = 4
GDN_CHUNK = 64
ROPE_DIM = HEAD_DIM // 4
ROPE_THETA = 500000.0
DILATIONS = (1, 4, 16)
RMS_EPS = 1e-6

LANES = 128
PAIR = LANES // HEAD_DIM
NEG = -1e30

PA_COLS = 4 * A_WIDTH
BA_COLS = LANES
PB_COLS = 4 * B_WIDTH
PC_COLS = 4 * C_WIDTH
IN_COLS = PA_COLS + 2 * N_HEADS_A + PB_COLS + PC_COLS
W1_COLS = PA_COLS + BA_COLS + PB_COLS + PC_COLS

VMEM_LIMIT = 56 * 1024 * 1024

LOG2E = 1.4426950408889634
QSCALE2 = HEAD_DIM ** -0.5 * LOG2E


def _dot(a, b):
    return jnp.dot(a, b, preferred_element_type=F32)


def _dot_nt(a, b):
    return lax.dot_general(a, b, (((1,), (1,)), ((), ())), preferred_element_type=F32)


def _dot_tn(a, b):
    return lax.dot_general(a, b, (((0,), (0,)), ((), ())), preferred_element_type=F32)


def _aligned(x, m):
    return x if isinstance(x, int) else pl.multiple_of(x, m)


def _sigmoid(x):
    return 1.0 / (1.0 + jnp.exp(-x))


def _softplus(x):
    return jnp.maximum(x, 0.0) + jnp.log(1.0 + jnp.exp(-jnp.abs(x)))


def _split_bf16(x):
    hi = x.astype(BF16)
    lo = (x - hi.astype(F32)).astype(BF16)
    return hi, lo


def _head_lo(shape):
    return lax.broadcasted_iota(jnp.int32, shape, len(shape) - 1) < HEAD_DIM


def _block_diag(m, head_lo):
    z = jnp.zeros_like(m)
    return jnp.concatenate([jnp.where(head_lo, m, z), jnp.where(head_lo, z, m)], axis=0).astype(BF16)


def _head_sums(x2, ones_bd):
    return _dot(x2.astype(BF16), ones_bd)


def _ones_bd():
    r = lax.broadcasted_iota(jnp.int32, (LANES, LANES), 0) // HEAD_DIM
    c = lax.broadcasted_iota(jnp.int32, (LANES, LANES), 1) // HEAD_DIM
    return (r == c).astype(BF16)


W_ROWS = 128


def _in_proj_kernel(x_ref, nw_ref, w_ref, pa_ref, ba_ref, pb_ref, pc_ref, wb):
    @pl.when(pl.program_id(0) == 0)
    def _():
        def rows(i, _):
            r = pl.ds(pl.multiple_of(i * W_ROWS, W_ROWS), W_ROWS)
            wb[r, 0:PA_COLS + BA_COLS] = w_ref[0, r, 0:PA_COLS + BA_COLS].astype(BF16)
            wb[r, PA_COLS + BA_COLS:W1_COLS] = w_ref[0, r, PA_COLS + 2 * N_HEADS_A:IN_COLS].astype(BF16)
            return 0

        lax.fori_loop(0, D_MODEL // W_ROWS, rows, 0)

    x = x_ref[...]
    ms = jnp.mean(x * x, axis=-1, keepdims=True)
    h = (x * lax.rsqrt(ms + RMS_EPS) * nw_ref[...]).astype(BF16)
    c0 = 0
    pa_ref[...] = _dot(h, wb[:, c0:c0 + PA_COLS]).astype(BF16)
    c0 += PA_COLS
    ba_ref[...] = _dot(h, wb[:, c0:c0 + BA_COLS])
    c0 += BA_COLS
    col_b = lax.broadcasted_iota(jnp.int32, (1, PB_COLS), 1)
    pb_ref[...] = (_dot(h, wb[:, c0:c0 + PB_COLS]) * jnp.where(col_b < B_WIDTH, QSCALE2, 1.0)).astype(BF16)
    c0 += PB_COLS
    pc_ref[...] = _dot(h, wb[:, c0:c0 + PC_COLS]).astype(BF16)


def _in_proj(x2d, norm_w, w_in, layer, tm):
    n = x2d.shape[0]
    row = lambda i: (i, 0)
    return pl.pallas_call(
        _in_proj_kernel,
        grid=(n // tm,),
        in_specs=[pl.BlockSpec((tm, D_MODEL), row),
                  pl.BlockSpec((1, D_MODEL), lambda i: (0, 0)),
                  pl.BlockSpec((1, D_MODEL, IN_COLS), lambda i: (layer, 0, 0), pipeline_mode=pl.Buffered(1))],
        out_specs=[pl.BlockSpec((tm, PA_COLS), row), pl.BlockSpec((tm, BA_COLS), row),
                   pl.BlockSpec((tm, PB_COLS), row), pl.BlockSpec((tm, PC_COLS), row)],
        out_shape=[jax.ShapeDtypeStruct((n, PA_COLS), BF16), jax.ShapeDtypeStruct((n, BA_COLS), F32),
                   jax.ShapeDtypeStruct((n, PB_COLS), BF16), jax.ShapeDtypeStruct((n, PC_COLS), BF16)],
        scratch_shapes=[pltpu.VMEM((D_MODEL, W1_COLS), BF16)],
        compiler_params=pltpu.CompilerParams(dimension_semantics=("arbitrary",), vmem_limit_bytes=VMEM_LIMIT),
        name="in_proj",
    )(x2d, norm_w, w_in)


def _out_proj_kernel(x_ref, oa_ref, ob_ref, oc_ref, w_ref, o_ref):
    acc = _dot(oa_ref[...], w_ref[0:A_WIDTH, :])
    acc = acc + _dot(ob_ref[...], w_ref[A_WIDTH:A_WIDTH + B_WIDTH, :])
    acc = acc + _dot(oc_ref[...], w_ref[A_WIDTH + B_WIDTH:, :])
    o_ref[...] = x_ref[...] + acc


def _out_proj(x2d, oa, ob, oc, w_out, tm):
    n = x2d.shape[0]
    row = lambda i: (i, 0)
    fixed = lambda i: (0, 0)
    return pl.pallas_call(
        _out_proj_kernel,
        grid=(n // tm,),
        in_specs=[pl.BlockSpec((tm, D_MODEL), row), pl.BlockSpec((tm, A_WIDTH), row),
                  pl.BlockSpec((tm, B_WIDTH), row), pl.BlockSpec((tm, C_WIDTH), row),
                  pl.BlockSpec((D_MODEL, D_MODEL), fixed)],
        out_specs=pl.BlockSpec((tm, D_MODEL), row),
        out_shape=jax.ShapeDtypeStruct((n, D_MODEL), F32),
        compiler_params=pltpu.CompilerParams(dimension_semantics=("arbitrary",), vmem_limit_bytes=VMEM_LIMIT),
        name="out_proj",
    )(x2d, oa, ob, oc, w_out)


SB_BLOCK = 256


def _sb_kernel(q_ref, k_ref, v_ref, z_ref, o_ref):
    seq = q_ref.shape[1]
    nb = seq // SB_BLOCK
    head_lo = _head_lo((1, LANES))
    row = lax.broadcasted_iota(jnp.int32, (SB_BLOCK, SB_BLOCK), 0)
    col = lax.broadcasted_iota(jnp.int32, (SB_BLOCK, SB_BLOCK), 1)
    earlier = col < row
    neg_suffix = jnp.where(row >= col, -1.0, 0.0).astype(BF16)

    def rows(b):
        return pl.ds(b * SB_BLOCK, SB_BLOCK)

    def scores(i, kt):
        q = q_ref[0, rows(i), :]
        zero = jnp.zeros_like(q)
        k = k_ref[0, rows(kt), :]
        return [_dot_nt(qh, k) for qh in (jnp.where(head_lo, q, zero), jnp.where(head_lo, zero, q))]

    def log_weights(z2, diag):
        nlk = [jnp.maximum(x, 0.0) + jnp.log2(1.0 + jnp.exp2(-jnp.abs(x))) for x in z2]
        if diag:
            nlk = [jnp.where(earlier, x, 0.0) for x in nlk]
        here_on = [_dot(x.astype(BF16), neg_suffix) for x in nlk]
        pre = [x + w for x, w in zip(z2, here_on)]
        if diag:
            pre = [jnp.where(earlier, x, NEG) for x in pre]
        return pre, [w[:, 0:1] for w in here_on]

    def accumulate(pre, carry, acc, kt):
        v = v_ref[0, rows(kt), :]
        w = [jnp.exp2(x + c).astype(BF16) for x, c in zip(pre, carry)]
        return [a + _dot(x, v) for a, x in zip(acc, w)]

    tiles = [(i, kt) for i in range(nb) for kt in range(i, -1, -1)]
    z2, lw = {}, {}
    carry = acc = None
    for step in range(len(tiles) + 2):
        if step < len(tiles):
            z2[step] = scores(*tiles[step])
        if 0 <= step - 1 < len(tiles):
            i, kt = tiles[step - 1]
            lw[step - 1] = log_weights(z2.pop(step - 1), i == kt)
        if 0 <= step - 2 < len(tiles):
            i, kt = tiles[step - 2]
            pre, tot = lw.pop(step - 2)
            if i == kt:
                carry = [jnp.zeros((SB_BLOCK, 1), F32)] * PAIR
                acc = [jnp.zeros((SB_BLOCK, LANES), F32)] * PAIR
            acc = accumulate(pre, carry, acc, kt)
            carry = [c + t for c, t in zip(carry, tot)]
            if kt == 0:
                o = jnp.where(head_lo, acc[0], acc[1])
                zg = z_ref[0, rows(i), :].astype(F32)
                o_ref[0, rows(i), :] = (o * (zg * _sigmoid(zg))).astype(BF16)


def _stick_breaking(pb):
    bsz, seq, _ = pb.shape
    npair = N_HEADS_B // PAIR
    spec = lambda g: pl.BlockSpec((1, seq, LANES), lambda b, p, g=g: (b, 0, g * npair + p))
    return pl.pallas_call(
        _sb_kernel,
        grid=(bsz, npair),
        in_specs=[spec(0), spec(1), spec(2), spec(3)],
        out_specs=pl.BlockSpec((1, seq, LANES), lambda b, p: (b, 0, p)),
        out_shape=jax.ShapeDtypeStruct((bsz, seq, B_WIDTH), BF16),
        compiler_params=pltpu.CompilerParams(dimension_semantics=("arbitrary", "arbitrary"),
                                             vmem_limit_bytes=VMEM_LIMIT),
        name="stick_breaking",
    )(pb, pb, pb, pb)


DW_BLOCK = 128
DW_SLOT = 2


def _dw_kernel(q_ref, k_ref, v_ref, z_ref, qw_ref, kw_ref, cos_ref, sina_ref, sinb_ref, o_ref,
               qf, kf, vf, q4f, k4f, v4f, qn, kn, qg4, kg4, vg4, qg16, kg16, vg16,
               m1_md, m2_md, l_md, acc_md, m1_st, m2_st, l_st, acc_st):
    seq = q_ref.shape[1]
    head_lo = _head_lo((1, LANES))
    ones_bd = _ones_bd()
    n_rows = 256

    def prep(i, _):
        blocks = [pl.ds(pl.multiple_of((2 * i + b) * n_rows, n_rows), n_rows) for b in range(2)]
        xs = [(ref[0, rows, :].astype(F32), w_ref[...], rows)
              for rows in blocks for ref, w_ref in ((q_ref, qw_ref), (k_ref, kw_ref))]
        ms = [_head_sums(x * x, ones_bd) * (1.0 / HEAD_DIM) for x, _, _ in xs]
        ys = [x * lax.rsqrt(m + RMS_EPS) * w for (x, w, _), m in zip(xs, ms)]
        ys = [y * cos_ref[rows, :] + pltpu.roll(y, LANES - ROPE_DIM // 2, 1) * sina_ref[rows, :]
              + pltpu.roll(y, ROPE_DIM // 2, 1) * sinb_ref[rows, :] for y, (_, _, rows) in zip(ys, xs)]
        for b, rows in enumerate(blocks):
            q, k = ys[2 * b] * QSCALE2, ys[2 * b + 1]
            qf[rows, :] = q
            kf[rows, :] = k
            qn[rows, :] = q.astype(BF16)
            kn[rows, :] = k.astype(BF16)
            vf[rows, :] = v_ref[0, rows, :].astype(F32)
        return 0

    lax.fori_loop(0, seq // (2 * n_rows), prep, 0)

    qi = lax.broadcasted_iota(jnp.int32, (DW_BLOCK, DW_BLOCK), 0)
    kj = lax.broadcasted_iota(jnp.int32, (DW_BLOCK, DW_BLOCK), 1)
    own_ok = kj <= qi
    prev_ok = kj >= qi
    both_ok = jnp.concatenate([prev_ok, own_ok], axis=1)

    def pair_of(a, b):
        return jnp.where(head_lo, a, b)

    d_mid, d_far = DILATIONS[1], DILATIONS[2]
    len_mid, len_far = seq // d_mid, seq // d_far
    for src, mid_f, mid_b, far_b in ((qf, q4f, qg4, qg16), (kf, k4f, kg4, kg16), (vf, v4f, vg4, vg16)):
        for r in range(d_mid):
            x = src[pl.ds(r, len_mid, stride=d_mid), :]
            mid_f[pl.ds(r * len_mid, len_mid), :] = x
            mid_b[pl.ds(r * len_mid, len_mid), :] = x.astype(BF16)
        for r in range(d_far):
            x = mid_f[pl.ds((r % d_mid) * len_mid + r // d_mid, len_far, stride=d_far // d_mid), :]
            far_b[pl.ds(r * len_far, len_far), :] = x.astype(BF16)
    sources = {d_far: (qg16, kg16, lambda ks: vg16[ks, :]), d_mid: (qg4, kg4, lambda ks: vg4[ks, :]),
               1: (qn, kn, lambda ks: v_ref[0, ks, :])}

    state_mid, state_nat = (m1_md, m2_md, l_md, acc_md), (m1_st, m2_st, l_st, acc_st)

    def nat_rows(it):
        return pl.ds(it["start"], DW_BLOCK, stride=it["d"]) if it["d"] > 1 else pl.ds(it["start"], DW_BLOCK)

    def state_in(it):
        if it["d"] == d_mid:
            return state_mid, pl.ds(it["c0"], DW_BLOCK)
        return state_nat, nat_rows(it)

    def state_out(it):
        if it["d"] == d_far:
            r = it["start"]
            return state_mid, pl.ds((r % d_mid) * len_mid + r // d_mid, DW_BLOCK, stride=d_far // d_mid)
        return state_nat, nat_rows(it)

    def stage_scores(it):
        q_src, k_src, _ = sources[it["d"]]
        c0 = it["c0"]
        keys = pl.ds(c0 - DW_BLOCK, 2 * DW_BLOCK) if it["has_prev"] else pl.ds(c0, DW_BLOCK)
        ok = both_ok if it["has_prev"] else own_ok
        q = q_src[pl.ds(c0, DW_BLOCK), :]
        zero = jnp.zeros_like(q)
        k = k_src[keys, :]
        s = [jnp.where(ok, _dot_nt(qh, k), NEG) for qh in (jnp.where(head_lo, q, zero), jnp.where(head_lo, zero, q))]
        return dict(s=s, m_blk=[jnp.max(sh, axis=1, keepdims=True) for sh in s], keys=keys)

    def stage_softmax(it, st):
        width = 2 if it["has_prev"] else 1
        if it["d"] == d_far:
            m_h = [jnp.broadcast_to(m, (DW_BLOCK, LANES)) for m in st["m_blk"]]
            alpha = None
        else:
            (m1_in, m2_in, _, _), rows_in = state_in(it)
            m_old = (m1_in[rows_in, :], m2_in[rows_in, :])
            m_h = [jnp.maximum(o, b) for o, b in zip(m_old, st["m_blk"])]
            alpha = jnp.exp2(pair_of(m_old[0], m_old[1]) - pair_of(m_h[0], m_h[1]))
        p = [jnp.exp2(sh - jnp.concatenate([mh] * width, axis=1)).astype(BF16) for sh, mh in zip(st["s"], m_h)]
        return dict(p=p, m_h=m_h, alpha=alpha, keys=st["keys"])

    def stage_values(it, st):
        _, _, v_of = sources[it["d"]]
        v = v_of(st["keys"])
        vo = jnp.concatenate([v, jnp.ones_like(v)], axis=1)
        pvl = [_dot(ph, vo) for ph in st["p"]]
        pv = pair_of(pvl[0][:, :LANES], pvl[1][:, :LANES])
        l_new = pair_of(pvl[0][:, LANES:], pvl[1][:, LANES:])
        if st["alpha"] is not None:
            (_, _, l_in, acc_in), rows_in = state_in(it)
            l_new = st["alpha"] * l_in[rows_in, :] + l_new
            pv = st["alpha"] * acc_in[rows_in, :] + pv
        if it["d"] == 1:
            nat = nat_rows(it)
            zg = z_ref[0, nat, :].astype(F32)
            o_ref[0, nat, :] = (pv / l_new * (zg * _sigmoid(zg))).astype(BF16)
        else:
            (m1_out, m2_out, l_out, acc_out), rows_out = state_out(it)
            m1_out[rows_out, :] = st["m_h"][0]
            m2_out[rows_out, :] = st["m_h"][1]
            l_out[rows_out, :] = l_new
            acc_out[rows_out, :] = pv

    groups = [[dict(d=d_far, c0=r * len_far, start=r, has_prev=False) for r in range(d_far)],
              [dict(d=d_mid, c0=r * len_mid + j * DW_BLOCK, start=r + j * DW_BLOCK * d_mid, has_prev=j > 0)
               for j in range(len_mid // DW_BLOCK) for r in range(d_mid)],
              [dict(d=1, c0=j * DW_BLOCK, start=j * DW_BLOCK, has_prev=j > 0) for j in range(seq // DW_BLOCK)]]
    slots = []
    for g in groups:
        slots += [g[i:i + DW_SLOT] for i in range(0, len(g), DW_SLOT)] + [[]]
    slots += [[]]
    scored, soft = {}, {}
    for step, slot in enumerate(slots):
        for n, it in enumerate(slot):
            scored[step, n] = stage_scores(it)
        if step >= 1:
            for n, it in enumerate(slots[step - 1]):
                soft[step - 1, n] = stage_softmax(it, scored.pop((step - 1, n)))
        if step >= 2:
            for n, it in enumerate(slots[step - 2]):
                stage_values(it, soft.pop((step - 2, n)))


def _rope_tables(seq):
    half = ROPE_DIM // 2
    inv_freq = ROPE_THETA ** (-jnp.arange(half, dtype=F32) / half)
    ang = jnp.arange(seq, dtype=jnp.int32).astype(F32)[:, None] * inv_freq[None, :]
    cos, sin = jnp.cos(ang), jnp.sin(ang)
    ones = jnp.ones((seq, HEAD_DIM - ROPE_DIM), F32)
    zeros_h = jnp.zeros((seq, half), F32)
    zeros_t = jnp.zeros((seq, HEAD_DIM - ROPE_DIM), F32)
    cos_h = jnp.concatenate([cos, cos, ones], axis=1)
    sina_h = jnp.concatenate([-sin, zeros_h, zeros_t], axis=1)
    sinb_h = jnp.concatenate([zeros_h, sin, zeros_t], axis=1)
    tile = lambda t: jnp.tile(t, (1, PAIR))
    return tile(cos_h), tile(sina_h), tile(sinb_h)


def _dilated(pc, q_norm_w, k_norm_w):
    bsz, seq, _ = pc.shape
    npair = N_HEADS_C // PAIR
    assert seq % (DW_BLOCK * DILATIONS[-1]) == 0 and DILATIONS[-1] % DILATIONS[1] == 0
    cos, sina, sinb = _rope_tables(seq)
    qw = jnp.tile(q_norm_w.astype(F32), PAIR)[None, :]
    kw = jnp.tile(k_norm_w.astype(F32), PAIR)[None, :]
    spec = lambda g: pl.BlockSpec((1, seq, LANES), lambda b, p, g=g: (b, 0, g * npair + p))
    fixed = lambda shape: pl.BlockSpec(shape, lambda b, p: (0, 0))
    f32_rows, bf16_rows = pltpu.VMEM((seq, LANES), F32), pltpu.VMEM((seq, LANES), BF16)
    return pl.pallas_call(
        _dw_kernel,
        grid=(bsz, npair),
        in_specs=[spec(0), spec(1), spec(2), spec(3), fixed((1, LANES)), fixed((1, LANES)),
                  fixed((seq, LANES)), fixed((seq, LANES)), fixed((seq, LANES))],
        out_specs=pl.BlockSpec((1, seq, LANES), lambda b, p: (b, 0, p)),
        out_shape=jax.ShapeDtypeStruct((bsz, seq, C_WIDTH), BF16),
        scratch_shapes=[f32_rows] * 6 + [bf16_rows] * 8 + [f32_rows] * 8,
        compiler_params=pltpu.CompilerParams(dimension_semantics=("arbitrary", "arbitrary"),
                                             vmem_limit_bytes=VMEM_LIMIT),
        name="dilated_window",
    )(pc, pc, pc, pc, qw, kw, cos, sina, sinb)


GDN_ROWS = 256
GDN_HALO = 16
GDN_GROUP = 16


def _gdn_constants():
    npair = N_HEADS_A // PAIR
    iota = lambda shape, dim: lax.broadcasted_iota(jnp.int32, shape, dim)
    n_shift = CONV_WIDTH - 1
    sr, sc = iota((n_shift * GDN_ROWS, GDN_ROWS), 0), iota((n_shift * GDN_ROWS, GDN_ROWS), 1)
    shift = sc == sr % GDN_ROWS - (CONV_WIDTH - 1) + sr // GDN_ROWS
    hr, hc = iota((n_shift * 8, GDN_HALO), 0), iota((n_shift * 8, GDN_HALO), 1)
    halo = hc == GDN_HALO + hr % 8 - (CONV_WIDTH - 1) + hr // 8
    ri, ci = iota((GDN_ROWS, 2 * GDN_ROWS), 0), iota((GDN_ROWS, 2 * GDN_ROWS), 1) % GDN_ROWS
    cum = (ri // GDN_CHUNK == ci // GDN_CHUNK) & (ci <= ri)
    er, ec = iota((2 * LANES, 2 * npair * LANES), 0) % LANES, iota((2 * LANES, 2 * npair * LANES), 1)
    expand = er == (ec // LANES % 2) * N_HEADS_A + 2 * (ec // (2 * LANES)) + ec % LANES // HEAD_DIM
    bd4 = iota((2 * LANES, 2 * LANES), 0) // HEAD_DIM == iota((2 * LANES, 2 * LANES), 1) // HEAD_DIM
    return [m.astype(BF16) for m in (shift, halo, cum, expand, bd4)]


def _gdn_kernel(pa_ref, ba_ref, cw_ref, alog_ref, dtb_ref, gnw_ref, shift_ref, halo_ref, cum_ref, expand_ref,
                bd4_ref, o_ref, qn_s, kn_s, v_s, be_s, gc_s, qt_s, o0_s, nf_s, kw_s, gl_s, zg_s):
    seq = pa_ref.shape[1]
    n_chunks = seq // GDN_CHUNK
    npair = N_HEADS_A // PAIR
    lane = lax.broadcasted_iota(jnp.int32, (1, LANES), 1)
    head_lo = lane < HEAD_DIM
    ones_bd = _ones_bd()
    bd_mask = (lax.broadcasted_iota(jnp.int32, (LANES, LANES), 0) // HEAD_DIM
               == lax.broadcasted_iota(jnp.int32, (LANES, LANES), 1) // HEAD_DIM)

    ii = lax.broadcasted_iota(jnp.int32, (GDN_CHUNK, LANES), 0)
    jl = lax.broadcasted_iota(jnp.int32, (GDN_CHUNK, LANES), 1) % HEAD_DIM
    incl = ii >= jl
    strict = ii > jl
    eye_pair = ii == jl
    eye_f = eye_pair.astype(F32)

    def qkv_cols(p):
        return [slice(t * A_WIDTH + p * LANES, t * A_WIDTH + (p + 1) * LANES) for t in range(3)]

    def prep_taps(rows, p):
        return _dot(shift_ref[...], jnp.concatenate([pa_ref[0, rows, c] for c in qkv_cols(p)], axis=1))

    def prep_rest(i, rows, taps):
        r0 = i * GDN_ROWS
        halo_rows = pl.ds(pl.multiple_of(jnp.maximum(r0 - GDN_HALO, 0), GDN_HALO), GDN_HALO)
        halo = pa_ref[0, halo_rows, 0:3 * A_WIDTH]
        halo = jnp.where(i > 0, halo, jnp.zeros_like(halo))
        edge = _dot(halo_ref[...], halo)

        ba = ba_ref[0, rows, :]
        g = -jnp.exp(alog_ref[...]) * _softplus(ba + dtb_ref[...])
        g = jnp.where((lane >= N_HEADS_A) & (lane < 2 * N_HEADS_A), g, 0.0)
        hi, lo = _split_bf16(g)
        gc = _dot(cum_ref[...], jnp.concatenate([hi, lo], axis=0))
        nar = jnp.where(lane < N_HEADS_A, _sigmoid(ba), gc)
        hi, lo = _split_bf16(nar)
        wide = _dot(jnp.concatenate([hi, lo], axis=1), expand_ref[...])

        def conv_silu(p, t):
            c = qkv_cols(p)[t]
            w = cw_ref[:, c]
            y = w[CONV_WIDTH - 1:CONV_WIDTH, :] * pa_ref[0, rows, c].astype(F32)
            y0 = 0.0
            for kk in range(CONV_WIDTH - 1):
                y = y + w[kk:kk + 1, :] * taps[p][kk * GDN_ROWS:(kk + 1) * GDN_ROWS, t * LANES:(t + 1) * LANES]
                y0 = y0 + w[kk:kk + 1, :] * edge[kk * 8:(kk + 1) * 8, c]
            y = jnp.concatenate([y[:8, :] + y0, y[8:, :]], axis=0)
            return y * _sigmoid(y)

        ys = [[conv_silu(p, t) for t in range(3)] for p in range(npair)]
        ss = [_dot(jnp.concatenate([y[0] * y[0], y[1] * y[1]], axis=1).astype(BF16), bd4_ref[...]) for y in ys]
        for p in range(npair):
            qn_s[p, rows, :] = ys[p][0] * lax.rsqrt(ss[p][:, :LANES] + RMS_EPS) * (HEAD_DIM ** -0.5)
            kn_s[p, rows, :] = ys[p][1] * lax.rsqrt(ss[p][:, LANES:] + RMS_EPS)
            v_s[p, rows, :] = ys[p][2]
            be_s[p, rows, :] = wide[:, 2 * p * LANES:(2 * p + 1) * LANES]
            gc_s[p, rows, :] = wide[:, (2 * p + 1) * LANES:(2 * p + 2) * LANES]

    def bd(ms):
        return [_block_diag(m, head_lo) for m in ms]

    def chunk_group(p, i, _):
        cs = [i * GDN_GROUP + gi for gi in range(GDN_GROUP)]
        rows = [pl.ds(pl.multiple_of(c * GDN_CHUNK, GDN_CHUNK), GDN_CHUNK) for c in cs]
        qn = [qn_s[p, r, :] for r in rows]
        kn = [kn_s[p, r, :] for r in rows]
        v = [v_s[p, r, :] for r in rows]
        be = [be_s[p, r, :] for r in rows]
        gc = [gc_s[p, r, :] for r in rows]
        kb = [a * b for a, b in zip(kn, be)]
        ap = [_dot_nt(jnp.concatenate([a, b], axis=0).astype(BF16), m) for a, b, m in zip(kb, qn, bd(kn))]
        g_row = [jnp.sum(jnp.where(eye_pair, x, 0.0), axis=0, keepdims=True) for x in gc]
        decay = [jnp.exp(jnp.where(incl, x - y, NEG)) for x, y in zip(gc, g_row)]
        x = [jnp.where(strict, -a[:GDN_CHUNK] * d, 0.0) for a, d in zip(ap, decay)]
        pm = [a[GDN_CHUNK:] * d for a, d in zip(ap, decay)]
        tm = [eye_f + a for a in x]
        xr = [_dot(a.astype(BF16), m) for a, m in zip(x, bd(x))]
        for r in range(1, 6):
            rhs = bd(xr)
            if r < 5:
                y = [_dot(jnp.concatenate([a, b], axis=0).astype(BF16), m) for a, b, m in zip(xr, tm, rhs)]
                xr = [a[:GDN_CHUNK] for a in y]
                tm = [a + b[GDN_CHUNK:] for a, b in zip(tm, y)]
            else:
                tm = [a + _dot(a.astype(BF16), m) for a, m in zip(tm, rhs)]
        eg = [jnp.exp(a) for a in gc]
        vb = bd([a * b for a, b in zip(v, be)])
        kbg = bd([a * b for a, b in zip(kb, eg)])
        uw = [_dot(a.astype(BF16), jnp.concatenate([b, c], axis=1)) for a, b, c in zip(tm, vb, kbg)]
        u_bd = bd([a[:, :LANES] for a in uw])
        w_bd = bd([a[:, LANES:] for a in uw])
        puw = [_dot(a.astype(BF16), jnp.concatenate([b, c], axis=1)) for a, b, c in zip(pm, u_bd, w_bd)]
        g_last = [a[GDN_CHUNK - 1:GDN_CHUNK, :] for a in gc]
        kg = [a * jnp.exp(b - c) for a, b, c in zip(kn, g_last, gc)]
        kuw = [_dot_tn(a.astype(BF16), b.astype(BF16)) for a, b in zip(kg, uw)]
        for c, r, q, e, a, b, gl in zip(cs, rows, qn, eg, puw, kuw, g_last):
            qt_s[p, r, :] = (q * e - a[:, LANES:]).astype(BF16)
            o0_s[p, r, :] = a[:, :LANES]
            nf_s[p, c] = jnp.where(bd_mask, b[:, :LANES], 0.0)
            kw_s[p, c] = jnp.where(bd_mask, b[:, LANES:], 0.0).astype(BF16)
            gl_s[p, c] = jnp.broadcast_to(jnp.exp(gl), (8, LANES))
        return 0

    def per_pair(p, _):
        lax.fori_loop(0, n_chunks // GDN_GROUP, functools.partial(chunk_group, p), 0)
        return 0

    step = pl.program_id(0)
    last_step = pl.num_programs(0) - 1
    steps_per_rows = GDN_ROWS // GDN_CHUNK

    @pl.when(step == 0)
    def _():
        def zero(c, _):
            rows = pl.ds(pl.multiple_of(c * GDN_CHUNK, GDN_CHUNK), GDN_CHUNK)
            zg_s[rows, :] = jnp.zeros((GDN_CHUNK, A_WIDTH), BF16)
            for p in range(npair):
                qt_s[p, rows, :] = jnp.zeros((GDN_CHUNK, LANES), BF16)
                o0_s[p, rows, :] = jnp.zeros((GDN_CHUNK, LANES), F32)
                nf_s[p, c] = jnp.zeros((LANES, LANES), F32)
                kw_s[p, c] = jnp.zeros((LANES, LANES), BF16)
                gl_s[p, c] = jnp.zeros((8, LANES), F32)
            return 0

        lax.fori_loop(0, n_chunks, zero, 0)

    def rows_step(with_prep, i, states):
        rows = pl.ds(pl.multiple_of(i * GDN_ROWS, GDN_ROWS), GDN_ROWS)
        outs = [[] for _ in range(npair)]
        taps = []
        for s in range(steps_per_rows):
            c = i * steps_per_rows + s
            crow = pl.ds(pl.multiple_of(c * GDN_CHUNK, GDN_CHUNK), GDN_CHUNK)
            sb = [st.astype(BF16) for st in states]
            ks = [_dot(kw_s[p, c], sb[p]) for p in range(npair)]
            for p in range(npair):
                outs[p].append(_dot(qt_s[p, crow, :], sb[p]) + o0_s[p, crow, :])
            states = tuple(gl_s[p, c][0:1, :] * states[p] + nf_s[p, c] - ks[p] for p in range(npair))
            if with_prep and s < npair:
                taps.append(prep_taps(rows, s))
        for p in range(npair):
            o = jnp.concatenate(outs[p], axis=0)
            ms = _head_sums(o * o, ones_bd) * (1.0 / HEAD_DIM)
            y = o * lax.rsqrt(ms + RMS_EPS) * gnw_ref[...]
            zg = zg_s[rows, p * LANES:(p + 1) * LANES].astype(F32)
            o_ref[0, rows, p * LANES:(p + 1) * LANES] = (y * (zg * _sigmoid(zg))).astype(BF16)
        if with_prep:
            prep_rest(i, rows, taps)
            zg_s[rows, :] = pa_ref[0, rows, 3 * A_WIDTH:4 * A_WIDTH]
        return states

    zero_states = tuple(jnp.zeros((LANES, LANES), F32) for _ in range(npair))

    @pl.when(step < last_step)
    def _():
        lax.fori_loop(0, seq // GDN_ROWS, functools.partial(rows_step, True), zero_states)
        lax.fori_loop(0, npair, per_pair, 0)

    @pl.when(step == last_step)
    def _():
        lax.fori_loop(0, seq // GDN_ROWS, functools.partial(rows_step, False), zero_states)


def _gdn(pa, ba, conv_w, a_log, dt_bias, gdn_norm_w):
    bsz, seq, _ = pa.shape
    npair = N_HEADS_A // PAIR
    n_chunks = seq // GDN_CHUNK
    pad = lambda vec: jnp.zeros((1, LANES), F32).at[0, N_HEADS_A:2 * N_HEADS_A].set(vec.astype(F32))
    gnw = jnp.tile(gdn_norm_w.astype(F32), PAIR)[None, :]
    fixed = lambda shape: pl.BlockSpec(shape, lambda b: (0,) * len(shape))
    per_pair_f32 = pltpu.VMEM((npair, seq, LANES), F32)
    consts = _gdn_constants()
    return pl.pallas_call(
        _gdn_kernel,
        grid=(bsz + 1,),
        in_specs=[pl.BlockSpec((1, seq, PA_COLS), lambda b: (jnp.minimum(b, bsz - 1), 0, 0)),
                  pl.BlockSpec((1, seq, BA_COLS), lambda b: (jnp.minimum(b, bsz - 1), 0, 0)),
                  fixed((CONV_WIDTH, 3 * A_WIDTH)), fixed((1, LANES)), fixed((1, LANES)), fixed((1, LANES))]
        + [fixed(m.shape) for m in consts],
        out_specs=pl.BlockSpec((1, seq, A_WIDTH), lambda b: (jnp.maximum(b - 1, 0), 0, 0)),
        out_shape=jax.ShapeDtypeStruct((bsz, seq, A_WIDTH), BF16),
        scratch_shapes=[
            per_pair_f32, per_pair_f32, per_pair_f32,
            per_pair_f32, per_pair_f32,
            pltpu.VMEM((npair, seq, LANES), BF16),
            per_pair_f32,
            pltpu.VMEM((npair, n_chunks, LANES, LANES), F32),
            pltpu.VMEM((npair, n_chunks, LANES, LANES), BF16),
            pltpu.VMEM((npair, n_chunks, 8, LANES), F32),
            pltpu.VMEM((seq, A_WIDTH), BF16),
        ],
        compiler_params=pltpu.CompilerParams(dimension_semantics=("arbitrary",), vmem_limit_bytes=VMEM_LIMIT),
        name="gated_delta",
    )(pa, ba, conv_w.astype(F32), pad(a_log), pad(dt_bias), gnw, *consts)


ROW_TILE = 512
OUT_ROW_TILE = 2048


def kernel(x, norm_w, w_in, conv_w, a_log, dt_bias, gdn_norm_w, q_norm_w, k_norm_w, w_out):
    bsz, seq, _ = x.shape
    assert w_in.shape[1:] == (D_MODEL, IN_COLS)
    shape3 = lambda t: t.reshape(bsz, seq, t.shape[-1])
    flat = lambda t: t.reshape(bsz * seq, t.shape[-1])
    x2d = flat(x)
    w_in = w_in.astype(F32)
    norm_w = norm_w.astype(F32)
    for layer in range(norm_w.shape[0]):
        pa, ba, pb, pc = _in_proj(x2d, norm_w[layer][None, :], w_in, layer, ROW_TILE)
        oa = _gdn(shape3(pa), shape3(ba), conv_w[layer], a_log[layer], dt_bias[layer], gdn_norm_w[layer])
        ob = _stick_breaking(shape3(pb))
        oc = _dilated(shape3(pc), q_norm_w[layer], k_norm_w[layer])
        x2d = _out_proj(x2d, flat(oa), flat(ob), flat(oc), w_out[layer].astype(BF16), OUT_ROW_TILE)
    return shape3(x2d)
```

```python
import functools

import jax
import jax.numpy as jnp
from jax import lax
from jax.experimental import pallas as pl
from jax.experimental.pallas import tpu as pltpu

F32 = jnp.float32
BF16 = jnp.bfloat16

D_MODEL = 1024
HEAD_DIM = 64
N_HEADS_A, N_HEADS_B, N_HEADS_C = 6, 4, 6
A_WIDTH, B_WIDTH, C_WIDTH = N_HEADS_A * HEAD_DIM, N_HEADS_B * HEAD_DIM, N_HEADS_C * HEAD_DIM
CONV_WIDTH = 4
GDN_CHUNK = 64
ROPE_DIM = HEAD_DIM // 4
ROPE_THETA = 500000.0
DILATIONS = (1, 4, 16)
RMS_EPS = 1e-6

LANES = 128
PAIR = LANES // HEAD_DIM
NEG = -1e30

PA_COLS = 4 * A_WIDTH
BA_COLS = LANES
PB_COLS = 4 * B_WIDTH
PC_COLS = 4 * C_WIDTH
IN_COLS = PA_COLS + 2 * N_HEADS_A + PB_COLS + PC_COLS
W1_COLS = PA_COLS + BA_COLS + PB_COLS + PC_COLS

VMEM_LIMIT = 56 * 1024 * 1024

LOG2E = 1.4426950408889634
QSCALE2 = HEAD_DIM ** -0.5 * LOG2E


def _dot(a, b):
    return jnp.dot(a, b, preferred_element_type=F32)


def _dot_nt(a, b):
    return lax.dot_general(a, b, (((1,), (1,)), ((), ())), preferred_element_type=F32)


def _dot_tn(a, b):
    return lax.dot_general(a, b, (((0,), (0,)), ((), ())), preferred_element_type=F32)


def _aligned(x, m):
    return x if isinstance(x, int) else pl.multiple_of(x, m)


def _sigmoid(x):
    return 1.0 / (1.0 + jnp.exp2(x * (-LOG2E)))


def _softplus(x):
    return jnp.maximum(x, 0.0) + jnp.log(1.0 + jnp.exp(-jnp.abs(x)))


def _split_bf16(x):
    hi = x.astype(BF16)
    lo = (x - hi.astype(F32)).astype(BF16)
    return hi, lo


def _head_lo(shape):
    return lax.broadcasted_iota(jnp.int32, shape, len(shape) - 1) < HEAD_DIM


def _block_diag(m, head_lo):
    z = jnp.zeros_like(m)
    return jnp.concatenate([jnp.where(head_lo, m, z), jnp.where(head_lo, z, m)], axis=0).astype(BF16)


def _head_sums(x2, ones_bd):
    return _dot(x2.astype(BF16), ones_bd)


def _ones_bd():
    r = lax.broadcasted_iota(jnp.int32, (LANES, LANES), 0) // HEAD_DIM
    c = lax.broadcasted_iota(jnp.int32, (LANES, LANES), 1) // HEAD_DIM
    return (r == c).astype(BF16)


W_ROWS = 128


def _in_proj_kernel(x_ref, nw_ref, w_ref, pa_ref, ba_ref, pb_ref, pc_ref, wb):
    @pl.when(pl.program_id(0) == 0)
    def _():
        def rows(i, _):
            r = pl.ds(pl.multiple_of(i * W_ROWS, W_ROWS), W_ROWS)
            wb[r, 0:PA_COLS + BA_COLS] = w_ref[0, r, 0:PA_COLS + BA_COLS].astype(BF16)
            wb[r, PA_COLS + BA_COLS:W1_COLS] = w_ref[0, r, PA_COLS + 2 * N_HEADS_A:IN_COLS].astype(BF16)
            return 0

        lax.fori_loop(0, D_MODEL // W_ROWS, rows, 0)

    x = x_ref[...]
    ms = jnp.mean(x * x, axis=-1, keepdims=True)
    h = (x * lax.rsqrt(ms + RMS_EPS) * nw_ref[...]).astype(BF16)
    c0 = 0
    pa_ref[...] = _dot(h, wb[:, c0:c0 + PA_COLS]).astype(BF16)
    c0 += PA_COLS
    ba_ref[...] = _dot(h, wb[:, c0:c0 + BA_COLS])
    c0 += BA_COLS
    col_b = lax.broadcasted_iota(jnp.int32, (1, PB_COLS), 1)
    pb_ref[...] = (_dot(h, wb[:, c0:c0 + PB_COLS]) * jnp.where(col_b < B_WIDTH, QSCALE2, 1.0)).astype(BF16)
    c0 += PB_COLS
    pc_ref[...] = _dot(h, wb[:, c0:c0 + PC_COLS]).astype(BF16)


def _in_proj(x2d, norm_w, w_in, layer, tm):
    n = x2d.shape[0]
    row = lambda i: (i, 0)
    return pl.pallas_call(
        _in_proj_kernel,
        grid=(n // tm,),
        in_specs=[pl.BlockSpec((tm, D_MODEL), row),
                  pl.BlockSpec((1, D_MODEL), lambda i: (0, 0)),
                  pl.BlockSpec((1, D_MODEL, IN_COLS), lambda i: (layer, 0, 0), pipeline_mode=pl.Buffered(1))],
        out_specs=[pl.BlockSpec((tm, PA_COLS), row), pl.BlockSpec((tm, BA_COLS), row),
                   pl.BlockSpec((tm, PB_COLS), row), pl.BlockSpec((tm, PC_COLS), row)],
        out_shape=[jax.ShapeDtypeStruct((n, PA_COLS), BF16), jax.ShapeDtypeStruct((n, BA_COLS), F32),
                   jax.ShapeDtypeStruct((n, PB_COLS), BF16), jax.ShapeDtypeStruct((n, PC_COLS), BF16)],
        scratch_shapes=[pltpu.VMEM((D_MODEL, W1_COLS), BF16)],
        compiler_params=pltpu.CompilerParams(dimension_semantics=("arbitrary",), vmem_limit_bytes=VMEM_LIMIT),
        name="in_proj",
    )(x2d, norm_w, w_in)


def _out_proj_kernel(x_ref, oa_ref, ob_ref, oc_ref, w_ref, o_ref):
    acc = _dot(oa_ref[...], w_ref[0:A_WIDTH, :])
    acc = acc + _dot(ob_ref[...], w_ref[A_WIDTH:A_WIDTH + B_WIDTH, :])
    acc = acc + _dot(oc_ref[...], w_ref[A_WIDTH + B_WIDTH:, :])
    o_ref[...] = x_ref[...] + acc


def _out_proj(x2d, oa, ob, oc, w_out, tm):
    n = x2d.shape[0]
    row = lambda i: (i, 0)
    fixed = lambda i: (0, 0)
    return pl.pallas_call(
        _out_proj_kernel,
        grid=(n // tm,),
        in_specs=[pl.BlockSpec((tm, D_MODEL), row), pl.BlockSpec((tm, A_WIDTH), row),
                  pl.BlockSpec((tm, B_WIDTH), row), pl.BlockSpec((tm, C_WIDTH), row),
                  pl.BlockSpec((D_MODEL, D_MODEL), fixed)],
        out_specs=pl.BlockSpec((tm, D_MODEL), row),
        out_shape=jax.ShapeDtypeStruct((n, D_MODEL), F32),
        compiler_params=pltpu.CompilerParams(dimension_semantics=("arbitrary",), vmem_limit_bytes=VMEM_LIMIT),
        name="out_proj",
    )(x2d, oa, ob, oc, w_out)


SB_BLOCK = 256


def _sb_kernel(q_ref, k_ref, v_ref, z_ref, o_ref):
    seq = q_ref.shape[1]
    nb = seq // SB_BLOCK
    head_lo = _head_lo((1, LANES))
    row = lax.broadcasted_iota(jnp.int32, (SB_BLOCK, SB_BLOCK), 0)
    col = lax.broadcasted_iota(jnp.int32, (SB_BLOCK, SB_BLOCK), 1)
    earlier = col < row
    neg_suffix = jnp.where(row >= col, -1.0, 0.0).astype(BF16)

    def rows(b):
        return pl.ds(b * SB_BLOCK, SB_BLOCK)

    q_heads = {}

    def scores(i, kt):
        if i not in q_heads:
            q = q_ref[0, rows(i), :]
            zero = jnp.zeros_like(q)
            q_heads[i] = (jnp.where(head_lo, q, zero), jnp.where(head_lo, zero, q))
        k = k_ref[0, rows(kt), :]
        return [_dot_nt(qh, k) for qh in q_heads[i]]

    def log_weights(z2, diag):
        nlk = [jnp.maximum(x, 0.0) + jnp.log2(1.0 + jnp.exp2(-jnp.abs(x))) for x in z2]
        if diag:
            nlk = [jnp.where(earlier, x, 0.0) for x in nlk]
        here_on = [_dot(x.astype(BF16), neg_suffix) for x in nlk]
        pre = [x + w for x, w in zip(z2, here_on)]
        if diag:
            pre = [jnp.where(earlier, x, NEG) for x in pre]
        return pre, [w[:, 0:1] for w in here_on]

    def accumulate(pre, carry, acc, kt):
        v = v_ref[0, rows(kt), :]
        w = [jnp.exp2(x + c).astype(BF16) for x, c in zip(pre, carry)]
        return [a + _dot(x, v) for a, x in zip(acc, w)]

    tiles = [(i, kt) for i in range(nb) for kt in range(i, -1, -1)]
    z2, lw = {}, {}
    carry = acc = None
    for step in range(len(tiles) + 2):
        if step < len(tiles):
            z2[step] = scores(*tiles[step])
        if 0 <= step - 1 < len(tiles):
            i, kt = tiles[step - 1]
            lw[step - 1] = log_weights(z2.pop(step - 1), i == kt)
        if 0 <= step - 2 < len(tiles):
            i, kt = tiles[step - 2]
            pre, tot = lw.pop(step - 2)
            if i == kt:
                carry = [jnp.zeros((SB_BLOCK, 1), F32)] * PAIR
                acc = [jnp.zeros((SB_BLOCK, LANES), F32)] * PAIR
            acc = accumulate(pre, carry, acc, kt)
            carry = [c + t for c, t in zip(carry, tot)]
            if kt == 0:
                o = jnp.where(head_lo, acc[0], acc[1])
                zg = z_ref[0, rows(i), :].astype(F32)
                o_ref[0, rows(i), :] = (o * (zg * _sigmoid(zg))).astype(BF16)


def _stick_breaking(pb):
    bsz, seq, _ = pb.shape
    npair = N_HEADS_B // PAIR
    spec = lambda g: pl.BlockSpec((1, seq, LANES), lambda b, p, g=g: (b, 0, g * npair + p))
    return pl.pallas_call(
        _sb_kernel,
        grid=(bsz, npair),
        in_specs=[spec(0), spec(1), spec(2), spec(3)],
        out_specs=pl.BlockSpec((1, seq, LANES), lambda b, p: (b, 0, p)),
        out_shape=jax.ShapeDtypeStruct((bsz, seq, B_WIDTH), BF16),
        compiler_params=pltpu.CompilerParams(dimension_semantics=("arbitrary", "arbitrary"),
                                             vmem_limit_bytes=VMEM_LIMIT),
        name="stick_breaking",
    )(pb, pb, pb, pb)


DW_BLOCK = 128
DW_SLOT = 2


def _dw_kernel(q_ref, k_ref, v_ref, z_ref, qw_ref, kw_ref, cos_ref, sin_ref, perm_ref, mean_ref, o_ref,
               qf, kf, vf, q4f, k4f, v4f, qn, kn, qg4, kg4, vg4, qg16, kg16, vg16,
               m1_md, m2_md, l_md, acc_md, m1_st, m2_st, l_st, acc_st):
    seq = q_ref.shape[1]
    head_lo = _head_lo((1, LANES))
    n_rows = 256

    def prep(i, _):
        blocks = [pl.ds(pl.multiple_of((2 * i + b) * n_rows, n_rows), n_rows) for b in range(2)]
        xs = [(ref[0, rows, :].astype(F32), w_ref[...], rows)
              for rows in blocks for ref, w_ref in ((q_ref, qw_ref), (k_ref, kw_ref))]
        ms = [_dot((x * x).astype(BF16), mean_ref[...]) for x, _, _ in xs]
        ys = [x * lax.rsqrt(m + RMS_EPS) * w for (x, w, _), m in zip(xs, ms)]
        partner = [_dot(jnp.concatenate(_split_bf16(y), axis=1), perm_ref[...]) for y in ys]
        ys = [y * cos_ref[rows, :] + pr * sin_ref[rows, :] for y, pr, (_, _, rows) in zip(ys, partner, xs)]
        for b, rows in enumerate(blocks):
            q, k = ys[2 * b], ys[2 * b + 1]
            qf[rows, :] = q
            kf[rows, :] = k
            qn[rows, :] = q.astype(BF16)
            kn[rows, :] = k.astype(BF16)
            vf[rows, :] = v_ref[0, rows, :].astype(F32)
        return 0

    lax.fori_loop(0, seq // (2 * n_rows), prep, 0)

    qi = lax.broadcasted_iota(jnp.int32, (DW_BLOCK, DW_BLOCK), 0)
    kj = lax.broadcasted_iota(jnp.int32, (DW_BLOCK, DW_BLOCK), 1)
    own_ok = kj <= qi
    prev_ok = kj >= qi
    both_ok = jnp.concatenate([prev_ok, own_ok], axis=1)

    def pair_of(a, b):
        return jnp.where(head_lo, a, b)

    d_mid, d_far = DILATIONS[1], DILATIONS[2]
    len_mid, len_far = seq // d_mid, seq // d_far
    for src, mid_f, mid_b, far_b in ((qf, q4f, qg4, qg16), (kf, k4f, kg4, kg16), (vf, v4f, vg4, vg16)):
        for r in range(d_mid):
            x = src[pl.ds(r, len_mid, stride=d_mid), :]
            mid_f[pl.ds(r * len_mid, len_mid), :] = x
            mid_b[pl.ds(r * len_mid, len_mid), :] = x.astype(BF16)
        for r in range(d_far):
            x = mid_f[pl.ds((r % d_mid) * len_mid + r // d_mid, len_far, stride=d_far // d_mid), :]
            far_b[pl.ds(r * len_far, len_far), :] = x.astype(BF16)
    sources = {d_far: (qg16, kg16, lambda ks: vg16[ks, :]), d_mid: (qg4, kg4, lambda ks: vg4[ks, :]),
               1: (qn, kn, lambda ks: v_ref[0, ks, :])}

    state_mid, state_nat = (m1_md, m2_md, l_md, acc_md), (m1_st, m2_st, l_st, acc_st)

    def nat_rows(it):
        return pl.ds(it["start"], DW_BLOCK, stride=it["d"]) if it["d"] > 1 else pl.ds(it["start"], DW_BLOCK)

    def state_in(it):
        if it["d"] == d_mid:
            return state_mid, pl.ds(it["c0"], DW_BLOCK)
        return state_nat, nat_rows(it)

    def state_out(it):
        if it["d"] == d_far:
            r = it["start"]
            return state_mid, pl.ds((r % d_mid) * len_mid + r // d_mid, DW_BLOCK, stride=d_far // d_mid)
        return state_nat, nat_rows(it)

    def stage_scores(it):
        q_src, k_src, _ = sources[it["d"]]
        c0 = it["c0"]
        keys = pl.ds(c0 - DW_BLOCK, 2 * DW_BLOCK) if it["has_prev"] else pl.ds(c0, DW_BLOCK)
        ok = both_ok if it["has_prev"] else own_ok
        q = q_src[pl.ds(c0, DW_BLOCK), :]
        zero = jnp.zeros_like(q)
        k = k_src[keys, :]
        s = [jnp.where(ok, _dot_nt(qh, k), NEG) for qh in (jnp.where(head_lo, q, zero), jnp.where(head_lo, zero, q))]
        return dict(s=s, m_blk=[jnp.max(sh, axis=1, keepdims=True) for sh in s], keys=keys)

    def stage_softmax(it, st):
        width = 2 if it["has_prev"] else 1
        if it["d"] == d_far:
            m_h = [jnp.broadcast_to(m, (DW_BLOCK, LANES)) for m in st["m_blk"]]
            alpha = None
        else:
            (m1_in, m2_in, _, _), rows_in = state_in(it)
            m_old = (m1_in[rows_in, :], m2_in[rows_in, :])
            m_h = [jnp.maximum(o, b) for o, b in zip(m_old, st["m_blk"])]
            alpha = jnp.exp2(pair_of(m_old[0], m_old[1]) - pair_of(m_h[0], m_h[1]))
        p = [jnp.exp2(sh - jnp.concatenate([mh] * width, axis=1)).astype(BF16) for sh, mh in zip(st["s"], m_h)]
        return dict(p=p, m_h=m_h, alpha=alpha, keys=st["keys"])

    def stage_values(it, st):
        _, _, v_of = sources[it["d"]]
        v = v_of(st["keys"])
        vo = jnp.concatenate([v, jnp.ones_like(v)], axis=1)
        pvl = [_dot(ph, vo) for ph in st["p"]]
        pv = pair_of(pvl[0][:, :LANES], pvl[1][:, :LANES])
        l_new = pair_of(pvl[0][:, LANES:], pvl[1][:, LANES:])
        if st["alpha"] is not None:
            (_, _, l_in, acc_in), rows_in = state_in(it)
            l_new = st["alpha"] * l_in[rows_in, :] + l_new
            pv = st["alpha"] * acc_in[rows_in, :] + pv
        if it["d"] == 1:
            nat = nat_rows(it)
            zg = z_ref[0, nat, :].astype(F32)
            o_ref[0, nat, :] = (pv / l_new * (zg * _sigmoid(zg))).astype(BF16)
        else:
            (m1_out, m2_out, l_out, acc_out), rows_out = state_out(it)
            m1_out[rows_out, :] = st["m_h"][0]
            m2_out[rows_out, :] = st["m_h"][1]
            l_out[rows_out, :] = l_new
            acc_out[rows_out, :] = pv

    groups = [[dict(d=d_far, c0=r * len_far, start=r, has_prev=False) for r in range(d_far)],
              [dict(d=d_mid, c0=r * len_mid + j * DW_BLOCK, start=r + j * DW_BLOCK * d_mid, has_prev=j > 0)
               for j in range(len_mid // DW_BLOCK) for r in range(d_mid)],
              [dict(d=1, c0=j * DW_BLOCK, start=j * DW_BLOCK, has_prev=j > 0) for j in range(seq // DW_BLOCK)]]
    slots = []
    for g in groups:
        slots += [g[i:i + DW_SLOT] for i in range(0, len(g), DW_SLOT)] + [[]]
    slots += [[]]
    scored, soft = {}, {}
    for step, slot in enumerate(slots):
        for n, it in enumerate(slot):
            scored[step, n] = stage_scores(it)
        if step >= 1:
            for n, it in enumerate(slots[step - 1]):
                soft[step - 1, n] = stage_softmax(it, scored.pop((step - 1, n)))
        if step >= 2:
            for n, it in enumerate(slots[step - 2]):
                stage_values(it, soft.pop((step - 2, n)))


def _rope_tables(seq):
    half = ROPE_DIM // 2
    inv_freq = ROPE_THETA ** (-jnp.arange(half, dtype=F32) / half)
    ang = jnp.arange(seq, dtype=jnp.int32).astype(F32)[:, None] * inv_freq[None, :]
    cos, sin = jnp.cos(ang), jnp.sin(ang)
    ones = jnp.ones((seq, HEAD_DIM - ROPE_DIM), F32)
    zeros_t = jnp.zeros((seq, HEAD_DIM - ROPE_DIM), F32)
    cos_h = jnp.concatenate([cos, cos, ones], axis=1)
    sin_h = jnp.concatenate([-sin, sin, zeros_t], axis=1)
    tile = lambda t: jnp.tile(t, (1, PAIR))
    src = lax.broadcasted_iota(jnp.int32, (LANES, LANES), 0)
    dst = lax.broadcasted_iota(jnp.int32, (LANES, LANES), 1)
    in_head = dst % HEAD_DIM
    want = jnp.where(in_head < half, dst + half, jnp.where(in_head < ROPE_DIM, dst - half, -1))
    perm = (src == want).astype(BF16)
    return tile(cos_h), tile(sin_h), jnp.concatenate([perm, perm], axis=0)


def _dilated(pc, q_norm_w, k_norm_w):
    bsz, seq, _ = pc.shape
    npair = N_HEADS_C // PAIR
    assert seq % (DW_BLOCK * DILATIONS[-1]) == 0 and DILATIONS[-1] % DILATIONS[1] == 0
    cos, sin, perm = _rope_tables(seq)
    qw = jnp.tile(q_norm_w.astype(F32), PAIR)[None, :] * QSCALE2
    kw = jnp.tile(k_norm_w.astype(F32), PAIR)[None, :]
    head_mean = _ones_bd() * jnp.asarray(1.0 / HEAD_DIM, BF16)
    spec = lambda g: pl.BlockSpec((1, seq, LANES), lambda b, p, g=g: (b, 0, g * npair + p))
    fixed = lambda shape: pl.BlockSpec(shape, lambda b, p: (0, 0))
    f32_rows, bf16_rows = pltpu.VMEM((seq, LANES), F32), pltpu.VMEM((seq, LANES), BF16)
    return pl.pallas_call(
        _dw_kernel,
        grid=(bsz, npair),
        in_specs=[spec(0), spec(1), spec(2), spec(3), fixed((1, LANES)), fixed((1, LANES)),
                  fixed((seq, LANES)), fixed((seq, LANES)), fixed((2 * LANES, LANES)), fixed((LANES, LANES))],
        out_specs=pl.BlockSpec((1, seq, LANES), lambda b, p: (b, 0, p)),
        out_shape=jax.ShapeDtypeStruct((bsz, seq, C_WIDTH), BF16),
        scratch_shapes=[f32_rows] * 6 + [bf16_rows] * 8 + [f32_rows] * 8,
        compiler_params=pltpu.CompilerParams(dimension_semantics=("arbitrary", "arbitrary"),
                                             vmem_limit_bytes=VMEM_LIMIT),
        name="dilated_window",
    )(pc, pc, pc, pc, qw, kw, cos, sin, perm, head_mean)


GDN_ROWS = 256
GDN_HALO = 16
GDN_GROUP = 32


def _gdn_constants():
    npair = N_HEADS_A // PAIR
    iota = lambda shape, dim: lax.broadcasted_iota(jnp.int32, shape, dim)
    n_shift = CONV_WIDTH - 1
    sr, sc = iota((n_shift * GDN_ROWS, GDN_ROWS), 0), iota((n_shift * GDN_ROWS, GDN_ROWS), 1)
    shift = sc == sr % GDN_ROWS - (CONV_WIDTH - 1) + sr // GDN_ROWS
    hr, hc = iota((n_shift * 8, GDN_HALO), 0), iota((n_shift * 8, GDN_HALO), 1)
    halo = hc == GDN_HALO + hr % 8 - (CONV_WIDTH - 1) + hr // 8
    ri, ci = iota((GDN_ROWS, 2 * GDN_ROWS), 0), iota((GDN_ROWS, 2 * GDN_ROWS), 1) % GDN_ROWS
    cum = (ri // GDN_CHUNK == ci // GDN_CHUNK) & (ci <= ri)
    er, ec = iota((2 * LANES, 2 * npair * LANES), 0) % LANES, iota((2 * LANES, 2 * npair * LANES), 1)
    expand = er == (ec // LANES % 2) * N_HEADS_A + 2 * (ec // (2 * LANES)) + ec % LANES // HEAD_DIM
    same_head = iota((2 * LANES, 2 * LANES), 0) // HEAD_DIM == iota((2 * LANES, 2 * LANES), 1) // HEAD_DIM
    bd4 = jnp.where(same_head, jnp.where(iota((2 * LANES, 2 * LANES), 1) < LANES, float(HEAD_DIM), 1.0), 0.0)
    return [m.astype(BF16) for m in (shift, halo, cum, expand, bd4)]


def _gdn_kernel(pa_ref, ba_ref, cw_ref, alog_ref, dtb_ref, gnw_ref, shift_ref, halo_ref, cum_ref, expand_ref,
                bd4_ref, o_ref, qn_s, kn_s, v_s, be_s, gc_s, qt_s, o0_s, nf_s, kw_s, gl_s, zg_s):
    seq = pa_ref.shape[1]
    n_chunks = seq // GDN_CHUNK
    npair = N_HEADS_A // PAIR
    lane = lax.broadcasted_iota(jnp.int32, (1, LANES), 1)
    head_lo = lane < HEAD_DIM
    ones_bd = _ones_bd()
    bd_mask = (lax.broadcasted_iota(jnp.int32, (LANES, LANES), 0) // HEAD_DIM
               == lax.broadcasted_iota(jnp.int32, (LANES, LANES), 1) // HEAD_DIM)

    ii = lax.broadcasted_iota(jnp.int32, (GDN_CHUNK, LANES), 0)
    jl = lax.broadcasted_iota(jnp.int32, (GDN_CHUNK, LANES), 1) % HEAD_DIM
    incl = ii >= jl
    strict = ii > jl
    eye_pair = ii == jl
    eye_f = eye_pair.astype(F32)

    def qkv_cols(p):
        return [slice(t * A_WIDTH + p * LANES, t * A_WIDTH + (p + 1) * LANES) for t in range(3)]

    def prep_taps(rows, p):
        return _dot(shift_ref[...], jnp.concatenate([pa_ref[0, rows, c] for c in qkv_cols(p)], axis=1))

    def prep_gates(i, rows):
        r0 = i * GDN_ROWS
        halo_rows = pl.ds(pl.multiple_of(jnp.maximum(r0 - GDN_HALO, 0), GDN_HALO), GDN_HALO)
        halo = pa_ref[0, halo_rows, 0:3 * A_WIDTH]
        halo = jnp.where(i > 0, halo, jnp.zeros_like(halo))
        edge = _dot(halo_ref[...], halo)

        ba = ba_ref[0, rows, :]
        g = -jnp.exp(alog_ref[...]) * _softplus(ba + dtb_ref[...])
        g = jnp.where((lane >= N_HEADS_A) & (lane < 2 * N_HEADS_A), g, 0.0)
        hi, lo = _split_bf16(g)
        gc = _dot(cum_ref[...], jnp.concatenate([hi, lo], axis=0))
        nar = jnp.where(lane < N_HEADS_A, _sigmoid(ba), gc)
        hi, lo = _split_bf16(nar)
        return edge, _dot(jnp.concatenate([hi, lo], axis=1), expand_ref[...])

    def prep_conv(rows, taps, edge, wide):
        def conv_silu(p, t):
            c = qkv_cols(p)[t]
            w = cw_ref[:, c]
            y = w[CONV_WIDTH - 1:CONV_WIDTH, :] * pa_ref[0, rows, c].astype(F32)
            y0 = 0.0
            for kk in range(CONV_WIDTH - 1):
                y = y + w[kk:kk + 1, :] * taps[p][kk * GDN_ROWS:(kk + 1) * GDN_ROWS, t * LANES:(t + 1) * LANES]
                y0 = y0 + w[kk:kk + 1, :] * edge[kk * 8:(kk + 1) * 8, c]
            y = jnp.concatenate([y[:8, :] + y0, y[8:, :]], axis=0)
            return y * _sigmoid(y)

        ys = [[conv_silu(p, t) for t in range(3)] for p in range(npair)]
        ss = [_dot(jnp.concatenate([y[0] * y[0], y[1] * y[1]], axis=1).astype(BF16), bd4_ref[...]) for y in ys]
        for p in range(npair):
            qn_s[p, rows, :] = ys[p][0] * lax.rsqrt(ss[p][:, :LANES] + HEAD_DIM * RMS_EPS)
            kn_s[p, rows, :] = ys[p][1] * lax.rsqrt(ss[p][:, LANES:] + RMS_EPS)
            v_s[p, rows, :] = ys[p][2]
            be_s[p, rows, :] = wide[:, 2 * p * LANES:(2 * p + 1) * LANES]
            gc_s[p, rows, :] = wide[:, (2 * p + 1) * LANES:(2 * p + 2) * LANES]

    def bd(ms):
        return [_block_diag(m, head_lo) for m in ms]

    def chunk_group(p, i, _):
        cs = [i * GDN_GROUP + gi for gi in range(GDN_GROUP)]
        rows = [pl.ds(pl.multiple_of(c * GDN_CHUNK, GDN_CHUNK), GDN_CHUNK) for c in cs]
        qn = [qn_s[p, r, :] for r in rows]
        kn = [kn_s[p, r, :] for r in rows]
        v = [v_s[p, r, :] for r in rows]
        be = [be_s[p, r, :] for r in rows]
        gc = [gc_s[p, r, :] for r in rows]
        kb = [a * b for a, b in zip(kn, be)]
        ap = [_dot_nt(jnp.concatenate([a, b], axis=0).astype(BF16), m) for a, b, m in zip(kb, qn, bd(kn))]
        g_row = [jnp.sum(jnp.where(eye_pair, x, 0.0), axis=0, keepdims=True) for x in gc]
        decay = [jnp.exp(jnp.where(incl, x - y, NEG)) for x, y in zip(gc, g_row)]
        x = [jnp.where(strict, -a[:GDN_CHUNK] * d, 0.0) for a, d in zip(ap, decay)]
        pm = [a[GDN_CHUNK:] * d for a, d in zip(ap, decay)]
        tm = [eye_f + a for a in x]
        xr = [_dot(a.astype(BF16), m) for a, m in zip(x, bd(x))]
        for r in range(1, 6):
            rhs = bd(xr)
            if r < 5:
                y = [_dot(jnp.concatenate([a, b], axis=0).astype(BF16), m) for a, b, m in zip(xr, tm, rhs)]
                xr = [a[:GDN_CHUNK] for a in y]
                tm = [a + b[GDN_CHUNK:] for a, b in zip(tm, y)]
            else:
                tm = [a + _dot(a.astype(BF16), m) for a, m in zip(tm, rhs)]
        eg = [jnp.exp(a) for a in gc]
        vb = bd([a * b for a, b in zip(v, be)])
        kbg = bd([a * b for a, b in zip(kb, eg)])
        uw = [_dot(a.astype(BF16), jnp.concatenate([b, c], axis=1)) for a, b, c in zip(tm, vb, kbg)]
        u_bd = bd([a[:, :LANES] for a in uw])
        w_bd = bd([a[:, LANES:] for a in uw])
        puw = [_dot(a.astype(BF16), jnp.concatenate([b, c], axis=1)) for a, b, c in zip(pm, u_bd, w_bd)]
        g_last = [a[GDN_CHUNK - 1:GDN_CHUNK, :] for a in gc]
        kg = [a * jnp.exp(b - c) for a, b, c in zip(kn, g_last, gc)]
        kuw = [_dot_tn(a.astype(BF16), b.astype(BF16)) for a, b in zip(kg, uw)]
        for c, r, q, e, a, b, gl in zip(cs, rows, qn, eg, puw, kuw, g_last):
            qt_s[p, r, :] = (q * e - a[:, LANES:]).astype(BF16)
            o0_s[p, r, :] = a[:, :LANES]
            nf_s[p, c] = jnp.where(bd_mask, b[:, :LANES], 0.0)
            kw_s[p, c] = jnp.where(bd_mask, b[:, LANES:], 0.0).astype(BF16)
            gl_s[p, c] = jnp.broadcast_to(jnp.exp(gl), (8, LANES))
        return 0

    def per_pair(p, _):
        lax.fori_loop(0, n_chunks // GDN_GROUP, functools.partial(chunk_group, p), 0)
        return 0

    step = pl.program_id(0)
    last_step = pl.num_programs(0) - 1
    steps_per_rows = GDN_ROWS // GDN_CHUNK

    @pl.when(step == 0)
    def _():
        def zero(c, _):
            rows = pl.ds(pl.multiple_of(c * GDN_CHUNK, GDN_CHUNK), GDN_CHUNK)
            zg_s[rows, :] = jnp.zeros((GDN_CHUNK, A_WIDTH), BF16)
            for p in range(npair):
                qt_s[p, rows, :] = jnp.zeros((GDN_CHUNK, LANES), BF16)
                o0_s[p, rows, :] = jnp.zeros((GDN_CHUNK, LANES), F32)
                nf_s[p, c] = jnp.zeros((LANES, LANES), F32)
                kw_s[p, c] = jnp.zeros((LANES, LANES), BF16)
                gl_s[p, c] = jnp.zeros((8, LANES), F32)
            return 0

        lax.fori_loop(0, n_chunks, zero, 0)

    def rows_step(with_prep, i, states):
        rows = pl.ds(pl.multiple_of(i * GDN_ROWS, GDN_ROWS), GDN_ROWS)
        outs = [[] for _ in range(npair)]
        taps = []
        for s in range(steps_per_rows):
            c = i * steps_per_rows + s
            crow = pl.ds(pl.multiple_of(c * GDN_CHUNK, GDN_CHUNK), GDN_CHUNK)
            sb = [st.astype(BF16) for st in states]
            ks = [_dot(kw_s[p, c], sb[p]) for p in range(npair)]
            for p in range(npair):
                outs[p].append(_dot(qt_s[p, crow, :], sb[p]) + o0_s[p, crow, :])
            states = tuple(gl_s[p, c][0:1, :] * states[p] + nf_s[p, c] - ks[p] for p in range(npair))
            if with_prep and s < npair:
                taps.append(prep_taps(rows, s))
        if with_prep:
            edge, wide = prep_gates(i, rows)
        for p in range(npair):
            o = jnp.concatenate(outs[p], axis=0)
            ms = _head_sums(o * o, ones_bd) * (1.0 / HEAD_DIM)
            y = o * lax.rsqrt(ms + RMS_EPS) * gnw_ref[...]
            zg = zg_s[rows, p * LANES:(p + 1) * LANES].astype(F32)
            o_ref[0, rows, p * LANES:(p + 1) * LANES] = (y * (zg * _sigmoid(zg))).astype(BF16)
        if with_prep:
            prep_conv(rows, taps, edge, wide)
            zg_s[rows, :] = pa_ref[0, rows, 3 * A_WIDTH:4 * A_WIDTH]
        return states

    zero_states = tuple(jnp.zeros((LANES, LANES), F32) for _ in range(npair))

    @pl.when(step < last_step)
    def _():
        lax.fori_loop(0, seq // GDN_ROWS, functools.partial(rows_step, True), zero_states)
        lax.fori_loop(0, npair, per_pair, 0)

    @pl.when(step == last_step)
    def _():
        lax.fori_loop(0, seq // GDN_ROWS, functools.partial(rows_step, False), zero_states)


def _gdn(pa, ba, conv_w, a_log, dt_bias, gdn_norm_w):
    bsz, seq, _ = pa.shape
    npair = N_HEADS_A // PAIR
    n_chunks = seq // GDN_CHUNK
    pad = lambda vec: jnp.zeros((1, LANES), F32).at[0, N_HEADS_A:2 * N_HEADS_A].set(vec.astype(F32))
    gnw = jnp.tile(gdn_norm_w.astype(F32), PAIR)[None, :]
    fixed = lambda shape: pl.BlockSpec(shape, lambda b: (0,) * len(shape))
    per_pair_f32 = pltpu.VMEM((npair, seq, LANES), F32)
    consts = _gdn_constants()
    return pl.pallas_call(
        _gdn_kernel,
        grid=(bsz + 1,),
        in_specs=[pl.BlockSpec((1, seq, PA_COLS), lambda b: (jnp.minimum(b, bsz - 1), 0, 0)),
                  pl.BlockSpec((1, seq, BA_COLS), lambda b: (jnp.minimum(b, bsz - 1), 0, 0)),
                  fixed((CONV_WIDTH, 3 * A_WIDTH)), fixed((1, LANES)), fixed((1, LANES)), fixed((1, LANES))]
        + [fixed(m.shape) for m in consts],
        out_specs=pl.BlockSpec((1, seq, A_WIDTH), lambda b: (jnp.maximum(b - 1, 0), 0, 0)),
        out_shape=jax.ShapeDtypeStruct((bsz, seq, A_WIDTH), BF16),
        scratch_shapes=[
            per_pair_f32, per_pair_f32, per_pair_f32,
            per_pair_f32, per_pair_f32,
            pltpu.VMEM((npair, seq, LANES), BF16),
            per_pair_f32,
            pltpu.VMEM((npair, n_chunks, LANES, LANES), F32),
            pltpu.VMEM((npair, n_chunks, LANES, LANES), BF16),
            pltpu.VMEM((npair, n_chunks, 8, LANES), F32),
            pltpu.VMEM((seq, A_WIDTH), BF16),
        ],
        compiler_params=pltpu.CompilerParams(dimension_semantics=("arbitrary",), vmem_limit_bytes=VMEM_LIMIT),
        name="gated_delta",
    )(pa, ba, conv_w.astype(F32), pad(a_log), pad(dt_bias), gnw, *consts)


ROW_TILE = 512
OUT_ROW_TILE = 2048


def kernel(x, norm_w, w_in, conv_w, a_log, dt_bias, gdn_norm_w, q_norm_w, k_norm_w, w_out):
    bsz, seq, _ = x.shape
    assert w_in.shape[1:] == (D_MODEL, IN_COLS)
    shape3 = lambda t: t.reshape(bsz, seq, t.shape[-1])
    flat = lambda t: t.reshape(bsz * seq, t.shape[-1])
    x2d = flat(x)
    w_in = w_in.astype(F32)
    norm_w = norm_w.astype(F32)
    for layer in range(norm_w.shape[0]):
        pa, ba, pb, pc = _in_proj(x2d, norm_w[layer][None, :], w_in, layer, ROW_TILE)
        oa = _gdn(shape3(pa), shape3(ba), conv_w[layer], a_log[layer], dt_bias[layer], gdn_norm_w[layer])
        ob = _stick_breaking(shape3(pb))
        oc = _dilated(shape3(pc), q_norm_w[layer], k_norm_w[layer])
        x2d = _out_proj(x2d, flat(oa), flat(ob), flat(oc), w_out[layer].astype(BF16), OUT_ROW_TILE)
    return shape3(x2d)
```

```python
import functools

import jax
import jax.numpy as jnp
from jax import lax
from jax.experimental import pallas as pl
from jax.experimental.pallas import tpu as pltpu

F32 = jnp.float32
BF16 = jnp.bfloat16

D_MODEL = 1024
HEAD_DIM = 64
N_HEADS_A, N_HEADS_B, N_HEADS_C = 6, 4, 6
A_WIDTH, B_WIDTH, C_WIDTH = N_HEADS_A * HEAD_DIM, N_HEADS_B * HEAD_DIM, N_HEADS_C * HEAD_DIM
CONV_WIDTH = 4
GDN_CHUNK = 64
ROPE_DIM = HEAD_DIM // 4
ROPE_THETA = 500000.0
DILATIONS = (1, 4, 16)
RMS_EPS = 1e-6

LANES = 128
PAIR = LANES // HEAD_DIM
NEG = -1e30

PA_COLS = 4 * A_WIDTH
BA_COLS = LANES
PB_COLS = 4 * B_WIDTH
PC_COLS = 4 * C_WIDTH
IN_COLS = PA_COLS + 2 * N_HEADS_A + PB_COLS + PC_COLS
W1_COLS = PA_COLS + BA_COLS + PB_COLS + PC_COLS

VMEM_LIMIT = 56 * 1024 * 1024

LOG2E = 1.4426950408889634
QSCALE2 = HEAD_DIM ** -0.5 * LOG2E


def _dot(a, b):
    return jnp.dot(a, b, preferred_element_type=F32)


def _dot_nt(a, b):
    return lax.dot_general(a, b, (((1,), (1,)), ((), ())), preferred_element_type=F32)


def _dot_tn(a, b):
    return lax.dot_general(a, b, (((0,), (0,)), ((), ())), preferred_element_type=F32)


def _aligned(x, m):
    return x if isinstance(x, int) else pl.multiple_of(x, m)


def _sigmoid(x):
    return 1.0 / (1.0 + jnp.exp2(x * (-LOG2E)))


def _softplus(x):
    return jnp.maximum(x, 0.0) + jnp.log(1.0 + jnp.exp(-jnp.abs(x)))


def _split_bf16(x):
    hi = x.astype(BF16)
    lo = (x - hi.astype(F32)).astype(BF16)
    return hi, lo


def _head_lo(shape):
    return lax.broadcasted_iota(jnp.int32, shape, len(shape) - 1) < HEAD_DIM


def _block_diag(m, head_lo):
    z = jnp.zeros_like(m)
    return jnp.concatenate([jnp.where(head_lo, m, z), jnp.where(head_lo, z, m)], axis=0).astype(BF16)


def _head_sums(x2, ones_bd):
    return _dot(x2.astype(BF16), ones_bd)


def _ones_bd():
    r = lax.broadcasted_iota(jnp.int32, (LANES, LANES), 0) // HEAD_DIM
    c = lax.broadcasted_iota(jnp.int32, (LANES, LANES), 1) // HEAD_DIM
    return (r == c).astype(BF16)


W_ROWS = 128


def _in_proj_kernel(x_ref, nw_ref, w_ref, pa_ref, ba_ref, pb_ref, pc_ref, wb):
    @pl.when(pl.program_id(0) == 0)
    def _():
        def rows(i, _):
            r = pl.ds(pl.multiple_of(i * W_ROWS, W_ROWS), W_ROWS)
            wb[r, 0:PA_COLS + BA_COLS] = w_ref[0, r, 0:PA_COLS + BA_COLS].astype(BF16)
            wb[r, PA_COLS + BA_COLS:W1_COLS] = w_ref[0, r, PA_COLS + 2 * N_HEADS_A:IN_COLS].astype(BF16)
            return 0

        lax.fori_loop(0, D_MODEL // W_ROWS, rows, 0)

    x = x_ref[...]
    ms = jnp.mean(x * x, axis=-1, keepdims=True)
    h = (x * lax.rsqrt(ms + RMS_EPS) * nw_ref[...]).astype(BF16)
    c0 = 0
    pa_ref[...] = _dot(h, wb[:, c0:c0 + PA_COLS]).astype(BF16)
    c0 += PA_COLS
    ba_ref[...] = _dot(h, wb[:, c0:c0 + BA_COLS])
    c0 += BA_COLS
    col_b = lax.broadcasted_iota(jnp.int32, (1, PB_COLS), 1)
    pb_ref[...] = (_dot(h, wb[:, c0:c0 + PB_COLS]) * jnp.where(col_b < B_WIDTH, QSCALE2, 1.0)).astype(BF16)
    c0 += PB_COLS
    pc_ref[...] = _dot(h, wb[:, c0:c0 + PC_COLS]).astype(BF16)


def _in_proj(x2d, norm_w, w_in, layer, tm):
    n = x2d.shape[0]
    row = lambda i: (i, 0)
    return pl.pallas_call(
        _in_proj_kernel,
        grid=(n // tm,),
        in_specs=[pl.BlockSpec((tm, D_MODEL), row),
                  pl.BlockSpec((1, D_MODEL), lambda i: (0, 0)),
                  pl.BlockSpec((1, D_MODEL, IN_COLS), lambda i: (layer, 0, 0), pipeline_mode=pl.Buffered(1))],
        out_specs=[pl.BlockSpec((tm, PA_COLS), row), pl.BlockSpec((tm, BA_COLS), row),
                   pl.BlockSpec((tm, PB_COLS), row), pl.BlockSpec((tm, PC_COLS), row)],
        out_shape=[jax.ShapeDtypeStruct((n, PA_COLS), BF16), jax.ShapeDtypeStruct((n, BA_COLS), F32),
                   jax.ShapeDtypeStruct((n, PB_COLS), BF16), jax.ShapeDtypeStruct((n, PC_COLS), BF16)],
        scratch_shapes=[pltpu.VMEM((D_MODEL, W1_COLS), BF16)],
        compiler_params=pltpu.CompilerParams(dimension_semantics=("arbitrary",), vmem_limit_bytes=VMEM_LIMIT),
        name="in_proj",
    )(x2d, norm_w, w_in)


def _out_proj_kernel(x_ref, oa_ref, ob_ref, oc_ref, w_ref, o_ref):
    acc = _dot(oa_ref[...], w_ref[0:A_WIDTH, :])
    acc = acc + _dot(ob_ref[...], w_ref[A_WIDTH:A_WIDTH + B_WIDTH, :])
    acc = acc + _dot(oc_ref[...], w_ref[A_WIDTH + B_WIDTH:, :])
    o_ref[...] = x_ref[...] + acc


def _out_proj(x2d, oa, ob, oc, w_out, tm):
    n = x2d.shape[0]
    row = lambda i: (i, 0)
    fixed = lambda i: (0, 0)
    return pl.pallas_call(
        _out_proj_kernel,
        grid=(n // tm,),
        in_specs=[pl.BlockSpec((tm, D_MODEL), row), pl.BlockSpec((tm, A_WIDTH), row),
                  pl.BlockSpec((tm, B_WIDTH), row), pl.BlockSpec((tm, C_WIDTH), row),
                  pl.BlockSpec((D_MODEL, D_MODEL), fixed)],
        out_specs=pl.BlockSpec((tm, D_MODEL), row),
        out_shape=jax.ShapeDtypeStruct((n, D_MODEL), F32),
        compiler_params=pltpu.CompilerParams(dimension_semantics=("arbitrary",), vmem_limit_bytes=VMEM_LIMIT),
        name="out_proj",
    )(x2d, oa, ob, oc, w_out)


SB_BLOCK = 256


def _sb_kernel(q_ref, k_ref, v_ref, z_ref, o_ref):
    seq = q_ref.shape[1]
    nb = seq // SB_BLOCK
    head_lo = _head_lo((1, LANES))
    row = lax.broadcasted_iota(jnp.int32, (SB_BLOCK, SB_BLOCK), 0)
    col = lax.broadcasted_iota(jnp.int32, (SB_BLOCK, SB_BLOCK), 1)
    earlier = col < row
    neg_suffix = jnp.where(row > col, -1.0, 0.0).astype(BF16)

    def rows(b):
        return pl.ds(b * SB_BLOCK, SB_BLOCK)

    q_heads = {}

    def scores(i, kt):
        if i not in q_heads:
            q = q_ref[0, rows(i), :]
            zero = jnp.zeros_like(q)
            q_heads[i] = (jnp.where(head_lo, q, zero), jnp.where(head_lo, zero, q))
        k = k_ref[0, rows(kt), :]
        return [_dot_nt(qh, k) for qh in q_heads[i]]

    def log_weights(z2, diag):
        nlk = [jnp.maximum(x, 0.0) + jnp.log2(1.0 + jnp.exp2(-jnp.abs(x))) for x in z2]
        if diag:
            nlk = [jnp.where(earlier, x, 0.0) for x in nlk]
        later = [_dot(x.astype(BF16), neg_suffix) for x in nlk]
        pre = [(x - y) + w for x, y, w in zip(z2, nlk, later)]
        if diag:
            pre = [jnp.where(earlier, x, NEG) for x in pre]
        return pre, [w[:, 0:1] - y[:, 0:1] for y, w in zip(nlk, later)]

    def accumulate(pre, carry, acc, kt):
        v = v_ref[0, rows(kt), :]
        w = [jnp.exp2(x + c).astype(BF16) for x, c in zip(pre, carry)]
        return [a + _dot(x, v) for a, x in zip(acc, w)]

    tiles = [(i, kt) for i in range(nb) for kt in range(i, -1, -1)]
    z2, lw = {}, {}
    carry = acc = None
    for step in range(len(tiles) + 2):
        if step < len(tiles):
            z2[step] = scores(*tiles[step])
        if 0 <= step - 1 < len(tiles):
            i, kt = tiles[step - 1]
            lw[step - 1] = log_weights(z2.pop(step - 1), i == kt)
        if 0 <= step - 2 < len(tiles):
            i, kt = tiles[step - 2]
            pre, tot = lw.pop(step - 2)
            if i == kt:
                carry = [jnp.zeros((SB_BLOCK, 1), F32)] * PAIR
                acc = [jnp.zeros((SB_BLOCK, LANES), F32)] * PAIR
            acc = accumulate(pre, carry, acc, kt)
            carry = [c + t for c, t in zip(carry, tot)]
            if kt == 0:
                o = jnp.where(head_lo, acc[0], acc[1])
                zg = z_ref[0, rows(i), :].astype(F32)
                o_ref[0, rows(i), :] = (o * (zg * _sigmoid(zg))).astype(BF16)


def _stick_breaking(pb):
    bsz, seq, _ = pb.shape
    npair = N_HEADS_B // PAIR
    spec = lambda g: pl.BlockSpec((1, seq, LANES), lambda b, p, g=g: (b, 0, g * npair + p))
    return pl.pallas_call(
        _sb_kernel,
        grid=(bsz, npair),
        in_specs=[spec(0), spec(1), spec(2), spec(3)],
        out_specs=pl.BlockSpec((1, seq, LANES), lambda b, p: (b, 0, p)),
        out_shape=jax.ShapeDtypeStruct((bsz, seq, B_WIDTH), BF16),
        compiler_params=pltpu.CompilerParams(dimension_semantics=("arbitrary", "arbitrary"),
                                             vmem_limit_bytes=VMEM_LIMIT),
        name="stick_breaking",
    )(pb, pb, pb, pb)


DW_BLOCK = 128
DW_SLOT = 2


def _dw_kernel(q_ref, k_ref, v_ref, z_ref, qw_ref, kw_ref, cos_ref, sin_ref, perm_ref, mean_ref, o_ref,
               qf, kf, vf, q4f, k4f, v4f, qn, kn, qg4, kg4, vg4, qg16, kg16, vg16,
               m1_md, m2_md, l_md, acc_md, m1_st, m2_st, l_st, acc_st):
    seq = q_ref.shape[1]
    head_lo = _head_lo((1, LANES))
    n_rows = 256

    def prep(i, _):
        blocks = [pl.ds(pl.multiple_of((2 * i + b) * n_rows, n_rows), n_rows) for b in range(2)]
        xs = [(ref[0, rows, :].astype(F32), w_ref[...], rows)
              for rows in blocks for ref, w_ref in ((q_ref, qw_ref), (k_ref, kw_ref))]
        ms = [_dot((x * x).astype(BF16), mean_ref[...]) for x, _, _ in xs]
        ys = [x * lax.rsqrt(m + RMS_EPS) * w for (x, w, _), m in zip(xs, ms)]
        partner = [_dot(jnp.concatenate(_split_bf16(y), axis=1), perm_ref[...]) for y in ys]
        ys = [y * cos_ref[rows, :] + pr * sin_ref[rows, :] for y, pr, (_, _, rows) in zip(ys, partner, xs)]
        for b, rows in enumerate(blocks):
            q, k = ys[2 * b], ys[2 * b + 1]
            qf[rows, :] = q
            kf[rows, :] = k
            qn[rows, :] = q.astype(BF16)
            kn[rows, :] = k.astype(BF16)
            vf[rows, :] = v_ref[0, rows, :].astype(F32)
        return 0

    lax.fori_loop(0, seq // (2 * n_rows), prep, 0)

    qi = lax.broadcasted_iota(jnp.int32, (DW_BLOCK, DW_BLOCK), 0)
    kj = lax.broadcasted_iota(jnp.int32, (DW_BLOCK, DW_BLOCK), 1)
    own_ok = kj <= qi
    prev_ok = kj >= qi
    both_ok = jnp.concatenate([prev_ok, own_ok], axis=1)

    def pair_of(a, b):
        return jnp.where(head_lo, a, b)

    d_mid, d_far = DILATIONS[1], DILATIONS[2]
    len_mid, len_far = seq // d_mid, seq // d_far
    for src, mid_f, mid_b, far_b in ((qf, q4f, qg4, qg16), (kf, k4f, kg4, kg16), (vf, v4f, vg4, vg16)):
        for r in range(d_mid):
            x = src[pl.ds(r, len_mid, stride=d_mid), :]
            mid_f[pl.ds(r * len_mid, len_mid), :] = x
            mid_b[pl.ds(r * len_mid, len_mid), :] = x.astype(BF16)
        for r in range(d_far):
            x = mid_f[pl.ds((r % d_mid) * len_mid + r // d_mid, len_far, stride=d_far // d_mid), :]
            far_b[pl.ds(r * len_far, len_far), :] = x.astype(BF16)
    sources = {d_far: (qg16, kg16, lambda ks: vg16[ks, :]), d_mid: (qg4, kg4, lambda ks: vg4[ks, :]),
               1: (qn, kn, lambda ks: v_ref[0, ks, :])}

    state_mid, state_nat = (m1_md, m2_md, l_md, acc_md), (m1_st, m2_st, l_st, acc_st)

    def nat_rows(it):
        return pl.ds(it["start"], DW_BLOCK, stride=it["d"]) if it["d"] > 1 else pl.ds(it["start"], DW_BLOCK)

    def state_in(it):
        if it["d"] == d_mid:
            return state_mid, pl.ds(it["c0"], DW_BLOCK)
        return state_nat, nat_rows(it)

    def state_out(it):
        if it["d"] == d_far:
            r = it["start"]
            return state_mid, pl.ds((r % d_mid) * len_mid + r // d_mid, DW_BLOCK, stride=d_far // d_mid)
        return state_nat, nat_rows(it)

    def stage_scores(it):
        q_src, k_src, _ = sources[it["d"]]
        c0 = it["c0"]
        keys = pl.ds(c0 - DW_BLOCK, 2 * DW_BLOCK) if it["has_prev"] else pl.ds(c0, DW_BLOCK)
        ok = both_ok if it["has_prev"] else own_ok
        q = q_src[pl.ds(c0, DW_BLOCK), :]
        zero = jnp.zeros_like(q)
        k = k_src[keys, :]
        s = [jnp.where(ok, _dot_nt(qh, k), NEG) for qh in (jnp.where(head_lo, q, zero), jnp.where(head_lo, zero, q))]
        return dict(s=s, m_blk=[jnp.max(sh, axis=1, keepdims=True) for sh in s], keys=keys)

    def stage_softmax(it, st):
        width = 2 if it["has_prev"] else 1
        if it["d"] == d_far:
            m_h = [jnp.broadcast_to(m, (DW_BLOCK, LANES)) for m in st["m_blk"]]
            alpha = None
        else:
            (m1_in, m2_in, _, _), rows_in = state_in(it)
            m_old = (m1_in[rows_in, :], m2_in[rows_in, :])
            m_h = [jnp.maximum(o, b) for o, b in zip(m_old, st["m_blk"])]
            alpha = jnp.exp2(pair_of(m_old[0], m_old[1]) - pair_of(m_h[0], m_h[1]))
        p = [jnp.exp2(sh - jnp.concatenate([mh] * width, axis=1)).astype(BF16) for sh, mh in zip(st["s"], m_h)]
        return dict(p=p, m_h=m_h, alpha=alpha, keys=st["keys"])

    def stage_values(it, st):
        _, _, v_of = sources[it["d"]]
        v = v_of(st["keys"])
        vo = jnp.concatenate([v, jnp.ones_like(v)], axis=1)
        pvl = [_dot(ph, vo) for ph in st["p"]]
        pv = pair_of(pvl[0][:, :LANES], pvl[1][:, :LANES])
        l_new = pair_of(pvl[0][:, LANES:], pvl[1][:, LANES:])
        if st["alpha"] is not None:
            (_, _, l_in, acc_in), rows_in = state_in(it)
            l_new = st["alpha"] * l_in[rows_in, :] + l_new
            pv = st["alpha"] * acc_in[rows_in, :] + pv
        if it["d"] == 1:
            nat = nat_rows(it)
            zg = z_ref[0, nat, :].astype(F32)
            o_ref[0, nat, :] = (pv / l_new * (zg * _sigmoid(zg))).astype(BF16)
        else:
            (m1_out, m2_out, l_out, acc_out), rows_out = state_out(it)
            m1_out[rows_out, :] = st["m_h"][0]
            m2_out[rows_out, :] = st["m_h"][1]
            l_out[rows_out, :] = l_new
            acc_out[rows_out, :] = pv

    groups = [[dict(d=d_far, c0=r * len_far, start=r, has_prev=False) for r in range(d_far)],
              [dict(d=d_mid, c0=r * len_mid + j * DW_BLOCK, start=r + j * DW_BLOCK * d_mid, has_prev=j > 0)
               for j in range(len_mid // DW_BLOCK) for r in range(d_mid)],
              [dict(d=1, c0=j * DW_BLOCK, start=j * DW_BLOCK, has_prev=j > 0) for j in range(seq // DW_BLOCK)]]
    slots = []
    for g in groups:
        slots += [g[i:i + DW_SLOT] for i in range(0, len(g), DW_SLOT)] + [[]]
    slots += [[]]
    scored, soft = {}, {}
    for step, slot in enumerate(slots):
        for n, it in enumerate(slot):
            scored[step, n] = stage_scores(it)
        if step >= 1:
            for n, it in enumerate(slots[step - 1]):
                soft[step - 1, n] = stage_softmax(it, scored.pop((step - 1, n)))
        if step >= 2:
            for n, it in enumerate(slots[step - 2]):
                stage_values(it, soft.pop((step - 2, n)))


def _rope_tables(seq):
    half = ROPE_DIM // 2
    inv_freq = ROPE_THETA ** (-jnp.arange(half, dtype=F32) / half)
    ang = jnp.arange(seq, dtype=jnp.int32).astype(F32)[:, None] * inv_freq[None, :]
    cos, sin = jnp.cos(ang), jnp.sin(ang)
    ones = jnp.ones((seq, HEAD_DIM - ROPE_DIM), F32)
    zeros_t = jnp.zeros((seq, HEAD_DIM - ROPE_DIM), F32)
    cos_h = jnp.concatenate([cos, cos, ones], axis=1)
    sin_h = jnp.concatenate([-sin, sin, zeros_t], axis=1)
    tile = lambda t: jnp.tile(t, (1, PAIR))
    src = lax.broadcasted_iota(jnp.int32, (LANES, LANES), 0)
    dst = lax.broadcasted_iota(jnp.int32, (LANES, LANES), 1)
    in_head = dst % HEAD_DIM
    want = jnp.where(in_head < half, dst + half, jnp.where(in_head < ROPE_DIM, dst - half, -1))
    perm = (src == want).astype(BF16)
    return tile(cos_h), tile(sin_h), jnp.concatenate([perm, perm], axis=0)


def _dilated(pc, q_norm_w, k_norm_w):
    bsz, seq, _ = pc.shape
    npair = N_HEADS_C // PAIR
    assert seq % (DW_BLOCK * DILATIONS[-1]) == 0 and DILATIONS[-1] % DILATIONS[1] == 0
    cos, sin, perm = _rope_tables(seq)
    qw = jnp.tile(q_norm_w.astype(F32), PAIR)[None, :] * QSCALE2
    kw = jnp.tile(k_norm_w.astype(F32), PAIR)[None, :]
    head_mean = _ones_bd() * jnp.asarray(1.0 / HEAD_DIM, BF16)
    spec = lambda g: pl.BlockSpec((1, seq, LANES), lambda b, p, g=g: (b, 0, g * npair + p))
    fixed = lambda shape: pl.BlockSpec(shape, lambda b, p: (0, 0))
    f32_rows, bf16_rows = pltpu.VMEM((seq, LANES), F32), pltpu.VMEM((seq, LANES), BF16)
    return pl.pallas_call(
        _dw_kernel,
        grid=(bsz, npair),
        in_specs=[spec(0), spec(1), spec(2), spec(3), fixed((1, LANES)), fixed((1, LANES)),
                  fixed((seq, LANES)), fixed((seq, LANES)), fixed((2 * LANES, LANES)), fixed((LANES, LANES))],
        out_specs=pl.BlockSpec((1, seq, LANES), lambda b, p: (b, 0, p)),
        out_shape=jax.ShapeDtypeStruct((bsz, seq, C_WIDTH), BF16),
        scratch_shapes=[f32_rows] * 6 + [bf16_rows] * 8 + [f32_rows] * 8,
        compiler_params=pltpu.CompilerParams(dimension_semantics=("arbitrary", "arbitrary"),
                                             vmem_limit_bytes=VMEM_LIMIT),
        name="dilated_window",
    )(pc, pc, pc, pc, qw, kw, cos, sin, perm, head_mean)


GDN_ROWS = 256
GDN_HALO = 16
GDN_GROUP = 32


def _gdn_constants():
    npair = N_HEADS_A // PAIR
    iota = lambda shape, dim: lax.broadcasted_iota(jnp.int32, shape, dim)
    n_shift = CONV_WIDTH - 1
    sr, sc = iota((n_shift * GDN_ROWS, GDN_ROWS), 0), iota((n_shift * GDN_ROWS, GDN_ROWS), 1)
    shift = sc == sr % GDN_ROWS - (CONV_WIDTH - 1) + sr // GDN_ROWS
    hr, hc = iota((n_shift * 8, GDN_HALO), 0), iota((n_shift * 8, GDN_HALO), 1)
    halo = hc == GDN_HALO + hr % 8 - (CONV_WIDTH - 1) + hr // 8
    ri, ci = iota((GDN_ROWS, 2 * GDN_ROWS), 0), iota((GDN_ROWS, 2 * GDN_ROWS), 1) % GDN_ROWS
    cum = (ri // GDN_CHUNK == ci // GDN_CHUNK) & (ci <= ri)
    er, ec = iota((2 * LANES, 2 * npair * LANES), 0) % LANES, iota((2 * LANES, 2 * npair * LANES), 1)
    expand = er == (ec // LANES % 2) * N_HEADS_A + 2 * (ec // (2 * LANES)) + ec % LANES // HEAD_DIM
    same_head = iota((2 * LANES, 2 * LANES), 0) // HEAD_DIM == iota((2 * LANES, 2 * LANES), 1) // HEAD_DIM
    bd4 = jnp.where(same_head, jnp.where(iota((2 * LANES, 2 * LANES), 1) < LANES, float(HEAD_DIM), 1.0), 0.0)
    return [m.astype(BF16) for m in (shift, halo, cum, expand, bd4)]


def _gdn_kernel(pa_ref, ba_ref, cw_ref, alog_ref, dtb_ref, gnw_ref, shift_ref, halo_ref, cum_ref, expand_ref,
                bd4_ref, o_ref, qn_s, kn_s, v_s, be_s, gc_s, qt_s, o0_s, nf_s, kw_s, gl_s, zg_s):
    seq = pa_ref.shape[1]
    n_chunks = seq // GDN_CHUNK
    npair = N_HEADS_A // PAIR
    lane = lax.broadcasted_iota(jnp.int32, (1, LANES), 1)
    head_lo = lane < HEAD_DIM
    ones_bd = _ones_bd()
    bd_mask = (lax.broadcasted_iota(jnp.int32, (LANES, LANES), 0) // HEAD_DIM
               == lax.broadcasted_iota(jnp.int32, (LANES, LANES), 1) // HEAD_DIM)

    ii = lax.broadcasted_iota(jnp.int32, (GDN_CHUNK, LANES), 0)
    jl = lax.broadcasted_iota(jnp.int32, (GDN_CHUNK, LANES), 1) % HEAD_DIM
    incl = ii >= jl
    strict = ii > jl
    eye_pair = ii == jl
    eye_f = eye_pair.astype(F32)

    def qkv_cols(p):
        return [slice(t * A_WIDTH + p * LANES, t * A_WIDTH + (p + 1) * LANES) for t in range(3)]

    def prep_taps(rows, p):
        return _dot(shift_ref[...], jnp.concatenate([pa_ref[0, rows, c] for c in qkv_cols(p)], axis=1))

    def prep_gates(i, rows):
        r0 = i * GDN_ROWS
        halo_rows = pl.ds(pl.multiple_of(jnp.maximum(r0 - GDN_HALO, 0), GDN_HALO), GDN_HALO)
        halo = pa_ref[0, halo_rows, 0:3 * A_WIDTH]
        halo = jnp.where(i > 0, halo, jnp.zeros_like(halo))
        edge = _dot(halo_ref[...], halo)

        ba = ba_ref[0, rows, :]
        g = -jnp.exp(alog_ref[...]) * _softplus(ba + dtb_ref[...])
        g = jnp.where((lane >= N_HEADS_A) & (lane < 2 * N_HEADS_A), g, 0.0)
        hi, lo = _split_bf16(g)
        gc = _dot(cum_ref[...], jnp.concatenate([hi, lo], axis=0))
        nar = jnp.where(lane < N_HEADS_A, _sigmoid(ba), gc)
        hi, lo = _split_bf16(nar)
        return edge, _dot(jnp.concatenate([hi, lo], axis=1), expand_ref[...])

    def prep_conv(rows, taps, edge, wide):
        def conv_silu(p, t):
            c = qkv_cols(p)[t]
            w = cw_ref[:, c]
            y = w[CONV_WIDTH - 1:CONV_WIDTH, :] * pa_ref[0, rows, c].astype(F32)
            y0 = 0.0
            for kk in range(CONV_WIDTH - 1):
                y = y + w[kk:kk + 1, :] * taps[p][kk * GDN_ROWS:(kk + 1) * GDN_ROWS, t * LANES:(t + 1) * LANES]
                y0 = y0 + w[kk:kk + 1, :] * edge[kk * 8:(kk + 1) * 8, c]
            y = jnp.concatenate([y[:8, :] + y0, y[8:, :]], axis=0)
            return y * _sigmoid(y)

        ys = [[conv_silu(p, t) for t in range(3)] for p in range(npair)]
        ss = [_dot(jnp.concatenate([y[0] * y[0], y[1] * y[1]], axis=1).astype(BF16), bd4_ref[...]) for y in ys]
        for p in range(npair):
            qn_s[p, rows, :] = ys[p][0] * lax.rsqrt(ss[p][:, :LANES] + HEAD_DIM * RMS_EPS)
            kn_s[p, rows, :] = ys[p][1] * lax.rsqrt(ss[p][:, LANES:] + RMS_EPS)
            v_s[p, rows, :] = ys[p][2]
            be_s[p, rows, :] = wide[:, 2 * p * LANES:(2 * p + 1) * LANES]
            gc_s[p, rows, :] = wide[:, (2 * p + 1) * LANES:(2 * p + 2) * LANES]

    def bd(ms):
        return [_block_diag(m, head_lo) for m in ms]

    def chunk_group(p, i, _):
        cs = [i * GDN_GROUP + gi for gi in range(GDN_GROUP)]
        rows = [pl.ds(pl.multiple_of(c * GDN_CHUNK, GDN_CHUNK), GDN_CHUNK) for c in cs]
        qn = [qn_s[p, r, :] for r in rows]
        kn = [kn_s[p, r, :] for r in rows]
        v = [v_s[p, r, :] for r in rows]
        be = [be_s[p, r, :] for r in rows]
        gc = [gc_s[p, r, :] for r in rows]
        kb = [a * b for a, b in zip(kn, be)]
        ap = [_dot_nt(jnp.concatenate([a, b], axis=0).astype(BF16), m) for a, b, m in zip(kb, qn, bd(kn))]
        g_row = [jnp.sum(jnp.where(eye_pair, x, 0.0), axis=0, keepdims=True) for x in gc]
        decay = [jnp.exp(jnp.where(incl, x - y, NEG)) for x, y in zip(gc, g_row)]
        x = [jnp.where(strict, -a[:GDN_CHUNK] * d, 0.0) for a, d in zip(ap, decay)]
        pm = [a[GDN_CHUNK:] * d for a, d in zip(ap, decay)]
        tm = [eye_f + a for a in x]
        xr = [_dot(a.astype(BF16), m) for a, m in zip(x, bd(x))]
        for r in range(1, 6):
            rhs = bd(xr)
            if r < 5:
                y = [_dot(jnp.concatenate([a, b], axis=0).astype(BF16), m) for a, b, m in zip(xr, tm, rhs)]
                xr = [a[:GDN_CHUNK] for a in y]
                tm = [a + b[GDN_CHUNK:] for a, b in zip(tm, y)]
            else:
                tm = [a + _dot(a.astype(BF16), m) for a, m in zip(tm, rhs)]
        eg = [jnp.exp(a) for a in gc]
        vb = bd([a * b for a, b in zip(v, be)])
        kbg = bd([a * b for a, b in zip(kb, eg)])
        uw = [_dot(a.astype(BF16), jnp.concatenate([b, c], axis=1)) for a, b, c in zip(tm, vb, kbg)]
        u_bd = bd([a[:, :LANES] for a in uw])
        w_bd = bd([a[:, LANES:] for a in uw])
        puw = [_dot(a.astype(BF16), jnp.concatenate([b, c], axis=1)) for a, b, c in zip(pm, u_bd, w_bd)]
        g_last = [a[GDN_CHUNK - 1:GDN_CHUNK, :] for a in gc]
        kg = [a * jnp.exp(b - c) for a, b, c in zip(kn, g_last, gc)]
        kuw = [_dot_tn(a.astype(BF16), b.astype(BF16)) for a, b in zip(kg, uw)]
        for c, r, q, e, a, b, gl in zip(cs, rows, qn, eg, puw, kuw, g_last):
            qt_s[p, r, :] = (q * e - a[:, LANES:]).astype(BF16)
            o0_s[p, r, :] = a[:, :LANES]
            nf_s[p, c] = jnp.where(bd_mask, b[:, :LANES], 0.0)
            kw_s[p, c] = jnp.where(bd_mask, b[:, LANES:], 0.0).astype(BF16)
            gl_s[p, c] = jnp.broadcast_to(jnp.exp(gl), (8, LANES))
        return 0

    def per_pair(p, _):
        lax.fori_loop(0, n_chunks // GDN_GROUP, functools.partial(chunk_group, p), 0)
        return 0

    step = pl.program_id(0)
    last_step = pl.num_programs(0) - 1
    steps_per_rows = GDN_ROWS // GDN_CHUNK

    @pl.when(step == 0)
    def _():
        def zero(c, _):
            rows = pl.ds(pl.multiple_of(c * GDN_CHUNK, GDN_CHUNK), GDN_CHUNK)
            zg_s[rows, :] = jnp.zeros((GDN_CHUNK, A_WIDTH), BF16)
            for p in range(npair):
                qt_s[p, rows, :] = jnp.zeros((GDN_CHUNK, LANES), BF16)
                o0_s[p, rows, :] = jnp.zeros((GDN_CHUNK, LANES), F32)
                nf_s[p, c] = jnp.zeros((LANES, LANES), F32)
                kw_s[p, c] = jnp.zeros((LANES, LANES), BF16)
                gl_s[p, c] = jnp.zeros((8, LANES), F32)
            return 0

        lax.fori_loop(0, n_chunks, zero, 0)

    def rows_step(with_prep, i, states):
        rows = pl.ds(pl.multiple_of(i * GDN_ROWS, GDN_ROWS), GDN_ROWS)
        outs = [[] for _ in range(npair)]
        taps = []
        for s in range(steps_per_rows):
            c = i * steps_per_rows + s
            crow = pl.ds(pl.multiple_of(c * GDN_CHUNK, GDN_CHUNK), GDN_CHUNK)
            sb = [st.astype(BF16) for st in states]
            ks = [_dot(kw_s[p, c], sb[p]) for p in range(npair)]
            for p in range(npair):
                outs[p].append(_dot(qt_s[p, crow, :], sb[p]) + o0_s[p, crow, :])
            states = tuple(gl_s[p, c][0:1, :] * states[p] + nf_s[p, c] - ks[p] for p in range(npair))
            if with_prep and s < npair:
                taps.append(prep_taps(rows, s))
        if with_prep:
            edge, wide = prep_gates(i, rows)
        for p in range(npair):
            o = jnp.concatenate(outs[p], axis=0)
            ms = _head_sums(o * o, ones_bd) * (1.0 / HEAD_DIM)
            y = o * lax.rsqrt(ms + RMS_EPS) * gnw_ref[...]
            zg = zg_s[rows, p * LANES:(p + 1) * LANES].astype(F32)
            o_ref[0, rows, p * LANES:(p + 1) * LANES] = (y * (zg * _sigmoid(zg))).astype(BF16)
        if with_prep:
            prep_conv(rows, taps, edge, wide)
            zg_s[rows, :] = pa_ref[0, rows, 3 * A_WIDTH:4 * A_WIDTH]
        return states

    zero_states = tuple(jnp.zeros((LANES, LANES), F32) for _ in range(npair))

    @pl.when(step < last_step)
    def _():
        lax.fori_loop(0, seq // GDN_ROWS, functools.partial(rows_step, True), zero_states)
        lax.fori_loop(0, npair, per_pair, 0)

    @pl.when(step == last_step)
    def _():
        lax.fori_loop(0, seq // GDN_ROWS, functools.partial(rows_step, False), zero_states)


def _gdn(pa, ba, conv_w, a_log, dt_bias, gdn_norm_w):
    bsz, seq, _ = pa.shape
    npair = N_HEADS_A // PAIR
    n_chunks = seq // GDN_CHUNK
    pad = lambda vec: jnp.zeros((1, LANES), F32).at[0, N_HEADS_A:2 * N_HEADS_A].set(vec.astype(F32))
    gnw = jnp.tile(gdn_norm_w.astype(F32), PAIR)[None, :]
    fixed = lambda shape: pl.BlockSpec(shape, lambda b: (0,) * len(shape))
    per_pair_f32 = pltpu.VMEM((npair, seq, LANES), F32)
    consts = _gdn_constants()
    return pl.pallas_call(
        _gdn_kernel,
        grid=(bsz + 1,),
        in_specs=[pl.BlockSpec((1, seq, PA_COLS), lambda b: (jnp.minimum(b, bsz - 1), 0, 0)),
                  pl.BlockSpec((1, seq, BA_COLS), lambda b: (jnp.minimum(b, bsz - 1), 0, 0)),
                  fixed((CONV_WIDTH, 3 * A_WIDTH)), fixed((1, LANES)), fixed((1, LANES)), fixed((1, LANES))]
        + [fixed(m.shape) for m in consts],
        out_specs=pl.BlockSpec((1, seq, A_WIDTH), lambda b: (jnp.maximum(b - 1, 0), 0, 0)),
        out_shape=jax.ShapeDtypeStruct((bsz, seq, A_WIDTH), BF16),
        scratch_shapes=[
            per_pair_f32, per_pair_f32, per_pair_f32,
            per_pair_f32, per_pair_f32,
            pltpu.VMEM((npair, seq, LANES), BF16),
            per_pair_f32,
            pltpu.VMEM((npair, n_chunks, LANES, LANES), F32),
            pltpu.VMEM((npair, n_chunks, LANES, LANES), BF16),
            pltpu.VMEM((npair, n_chunks, 8, LANES), F32),
            pltpu.VMEM((seq, A_WIDTH), BF16),
        ],
        compiler_params=pltpu.CompilerParams(dimension_semantics=("arbitrary",), vmem_limit_bytes=VMEM_LIMIT),
        name="gated_delta",
    )(pa, ba, conv_w.astype(F32), pad(a_log), pad(dt_bias), gnw, *consts)


ROW_TILE = 512
OUT_ROW_TILE = 2048


def kernel(x, norm_w, w_in, conv_w, a_log, dt_bias, gdn_norm_w, q_norm_w, k_norm_w, w_out):
    bsz, seq, _ = x.shape
    assert w_in.shape[1:] == (D_MODEL, IN_COLS)
    shape3 = lambda t: t.reshape(bsz, seq, t.shape[-1])
    flat = lambda t: t.reshape(bsz * seq, t.shape[-1])
    x2d = flat(x)
    w_in = w_in.astype(F32)
    norm_w = norm_w.astype(F32)
    for layer in range(norm_w.shape[0]):
        pa, ba, pb, pc = _in_proj(x2d, norm_w[layer][None, :], w_in, layer, ROW_TILE)
        oa = _gdn(shape3(pa), shape3(ba), conv_w[layer], a_log[layer], dt_bias[layer], gdn_norm_w[layer])
        ob = _stick_breaking(shape3(pb))
        oc = _dilated(shape3(pc), q_norm_w[layer], k_norm_w[layer])
        x2d = _out_proj(x2d, flat(oa), flat(ob), flat(oc), w_out[layer].astype(BF16), OUT_ROW_TILE)
    return shape3(x2d)
```

```python
import functools

import jax
import jax.numpy as jnp
from jax import lax
from jax.experimental import pallas as pl
from jax.experimental.pallas import tpu as pltpu

F32 = jnp.float32
BF16 = jnp.bfloat16

D_MODEL = 1024
HEAD_DIM = 64
N_HEADS_A, N_HEADS_B, N_HEADS_C = 6, 4, 6
A_WIDTH, B_WIDTH, C_WIDTH = N_HEADS_A * HEAD_DIM, N_HEADS_B * HEAD_DIM, N_HEADS_C * HEAD_DIM
CONV_WIDTH = 4
GDN_CHUNK = 64
ROPE_DIM = HEAD_DIM // 4
ROPE_THETA = 500000.0
DILATIONS = (1, 4, 16)
RMS_EPS = 1e-6

LANES = 128
SUBLANES = 8
PAIR = LANES // HEAD_DIM
NEG = -1e30

PA_COLS = 4 * A_WIDTH
BA_COLS = LANES
PB_COLS = 4 * B_WIDTH
PC_COLS = 4 * C_WIDTH
IN_COLS = PA_COLS + 2 * N_HEADS_A + PB_COLS + PC_COLS
W1_COLS = PA_COLS + BA_COLS + PB_COLS + PC_COLS

VMEM_LIMIT = 56 * 1024 * 1024

LOG2E = 1.4426950408889634
QSCALE2 = HEAD_DIM ** -0.5 * LOG2E


def _dot(a, b):
    return jnp.dot(a, b, preferred_element_type=F32)


def _dot_nt(a, b):
    return lax.dot_general(a, b, (((1,), (1,)), ((), ())), preferred_element_type=F32)


def _dot_tn(a, b):
    return lax.dot_general(a, b, (((0,), (0,)), ((), ())), preferred_element_type=F32)


def _aligned(x, m):
    return x if isinstance(x, int) else pl.multiple_of(x, m)


def _sigmoid(x):
    return 1.0 / (1.0 + jnp.exp2(x * (-LOG2E)))


def _softplus(x):
    return jnp.maximum(x, 0.0) + jnp.log(1.0 + jnp.exp(-jnp.abs(x)))


def _split_bf16(x):
    hi = x.astype(BF16)
    lo = (x - hi.astype(F32)).astype(BF16)
    return hi, lo


def _head_lo(shape):
    return lax.broadcasted_iota(jnp.int32, shape, len(shape) - 1) < HEAD_DIM


def _block_diag(m, head_lo):
    z = jnp.zeros_like(m)
    return jnp.concatenate([jnp.where(head_lo, m, z), jnp.where(head_lo, z, m)], axis=0).astype(BF16)


def _ones_bd():
    r = lax.broadcasted_iota(jnp.int32, (LANES, LANES), 0) // HEAD_DIM
    c = lax.broadcasted_iota(jnp.int32, (LANES, LANES), 1) // HEAD_DIM
    return (r == c).astype(BF16)


W_ROWS = 128


def _in_proj_kernel(x_ref, nw_ref, w_ref, pa_ref, ba_ref, pb_ref, pc_ref, wb):
    @pl.when(pl.program_id(0) == 0)
    def _():
        def rows(i, _):
            r = pl.ds(pl.multiple_of(i * W_ROWS, W_ROWS), W_ROWS)
            wb[r, 0:PA_COLS + BA_COLS] = w_ref[0, r, 0:PA_COLS + BA_COLS].astype(BF16)
            wb[r, PA_COLS + BA_COLS:W1_COLS] = w_ref[0, r, PA_COLS + 2 * N_HEADS_A:IN_COLS].astype(BF16)
            return 0

        lax.fori_loop(0, D_MODEL // W_ROWS, rows, 0)

    x = x_ref[...]
    ms = jnp.mean(x * x, axis=-1, keepdims=True)
    h = (x * lax.rsqrt(ms + RMS_EPS) * nw_ref[...]).astype(BF16)
    c0 = 0
    pa_ref[...] = _dot(h, wb[:, c0:c0 + PA_COLS]).astype(BF16)
    c0 += PA_COLS
    ba_ref[...] = _dot(h, wb[:, c0:c0 + BA_COLS])
    c0 += BA_COLS
    col_b = lax.broadcasted_iota(jnp.int32, (1, PB_COLS), 1)
    pb_ref[...] = (_dot(h, wb[:, c0:c0 + PB_COLS]) * jnp.where(col_b < B_WIDTH, QSCALE2, 1.0)).astype(BF16)
    c0 += PB_COLS
    pc_ref[...] = _dot(h, wb[:, c0:c0 + PC_COLS]).astype(BF16)


def _in_proj(x2d, norm_w, w_in, layer, tm):
    n = x2d.shape[0]
    row = lambda i: (i, 0)
    return pl.pallas_call(
        _in_proj_kernel,
        grid=(n // tm,),
        in_specs=[pl.BlockSpec((tm, D_MODEL), row),
                  pl.BlockSpec((1, D_MODEL), lambda i: (0, 0)),
                  pl.BlockSpec((1, D_MODEL, IN_COLS), lambda i: (layer, 0, 0), pipeline_mode=pl.Buffered(1))],
        out_specs=[pl.BlockSpec((tm, PA_COLS), row), pl.BlockSpec((tm, BA_COLS), row),
                   pl.BlockSpec((tm, PB_COLS), row), pl.BlockSpec((tm, PC_COLS), row)],
        out_shape=[jax.ShapeDtypeStruct((n, PA_COLS), BF16), jax.ShapeDtypeStruct((n, BA_COLS), F32),
                   jax.ShapeDtypeStruct((n, PB_COLS), BF16), jax.ShapeDtypeStruct((n, PC_COLS), BF16)],
        scratch_shapes=[pltpu.VMEM((D_MODEL, W1_COLS), BF16)],
        compiler_params=pltpu.CompilerParams(dimension_semantics=("arbitrary",), vmem_limit_bytes=VMEM_LIMIT),
        name="in_proj",
    )(x2d, norm_w, w_in)


def _out_proj_kernel(x_ref, oa_ref, ob_ref, oc_ref, w_ref, o_ref):
    acc = _dot(oa_ref[...], w_ref[0:A_WIDTH, :])
    acc = acc + _dot(ob_ref[...], w_ref[A_WIDTH:A_WIDTH + B_WIDTH, :])
    acc = acc + _dot(oc_ref[...], w_ref[A_WIDTH + B_WIDTH:, :])
    o_ref[...] = x_ref[...] + acc


def _out_proj(x2d, oa, ob, oc, w_out, tm):
    n = x2d.shape[0]
    row = lambda i: (i, 0)
    fixed = lambda i: (0, 0)
    return pl.pallas_call(
        _out_proj_kernel,
        grid=(n // tm,),
        in_specs=[pl.BlockSpec((tm, D_MODEL), row), pl.BlockSpec((tm, A_WIDTH), row),
                  pl.BlockSpec((tm, B_WIDTH), row), pl.BlockSpec((tm, C_WIDTH), row),
                  pl.BlockSpec((D_MODEL, D_MODEL), fixed)],
        out_specs=pl.BlockSpec((tm, D_MODEL), row),
        out_shape=jax.ShapeDtypeStruct((n, D_MODEL), F32),
        compiler_params=pltpu.CompilerParams(dimension_semantics=("arbitrary",), vmem_limit_bytes=VMEM_LIMIT),
        name="out_proj",
    )(x2d, oa, ob, oc, w_out)


SB_BLOCK = 256


def _sb_kernel(q_ref, k_ref, v_ref, z_ref, o_ref):
    seq = q_ref.shape[1]
    nb = seq // SB_BLOCK
    head_lo = _head_lo((1, LANES))
    row = lax.broadcasted_iota(jnp.int32, (SB_BLOCK, SB_BLOCK), 0)
    col = lax.broadcasted_iota(jnp.int32, (SB_BLOCK, SB_BLOCK), 1)
    earlier = col < row
    neg_suffix = jnp.where(row > col, -1.0, 0.0).astype(BF16)

    def rows(b):
        return pl.ds(b * SB_BLOCK, SB_BLOCK)

    q_heads = {}

    def scores(i, kt):
        if i not in q_heads:
            q = q_ref[0, rows(i), :]
            zero = jnp.zeros_like(q)
            q_heads[i] = (jnp.where(head_lo, q, zero), jnp.where(head_lo, zero, q))
        k = k_ref[0, rows(kt), :]
        return [_dot_nt(qh, k) for qh in q_heads[i]]

    def log_weights(z2, diag):
        nlk = [jnp.maximum(x, 0.0) + jnp.log2(1.0 + jnp.exp2(-jnp.abs(x))) for x in z2]
        if diag:
            nlk = [jnp.where(earlier, x, 0.0) for x in nlk]
        later = [_dot(x.astype(BF16), neg_suffix) for x in nlk]
        pre = [(x - y) + w for x, y, w in zip(z2, nlk, later)]
        if diag:
            pre = [jnp.where(earlier, x, NEG) for x in pre]
        return pre, [w[:, 0:1] - y[:, 0:1] for y, w in zip(nlk, later)]

    def accumulate(pre, carry, acc, kt):
        v = v_ref[0, rows(kt), :]
        w = [jnp.exp2(x + c).astype(BF16) for x, c in zip(pre, carry)]
        return [a + _dot(x, v) for a, x in zip(acc, w)]

    tiles = [(i, kt) for i in range(nb) for kt in range(i, -1, -1)]
    z2, lw = {}, {}
    carry = acc = None
    for step in range(len(tiles) + 2):
        if step < len(tiles):
            z2[step] = scores(*tiles[step])
        if 0 <= step - 1 < len(tiles):
            i, kt = tiles[step - 1]
            lw[step - 1] = log_weights(z2.pop(step - 1), i == kt)
        if 0 <= step - 2 < len(tiles):
            i, kt = tiles[step - 2]
            pre, tot = lw.pop(step - 2)
            if i == kt:
                carry = [jnp.zeros((SB_BLOCK, 1), F32)] * PAIR
                acc = [jnp.zeros((SB_BLOCK, LANES), F32)] * PAIR
            acc = accumulate(pre, carry, acc, kt)
            carry = [c + t for c, t in zip(carry, tot)]
            if kt == 0:
                o = jnp.where(head_lo, acc[0], acc[1])
                zg = z_ref[0, rows(i), :].astype(F32)
                o_ref[0, rows(i), :] = (o * (zg * _sigmoid(zg))).astype(BF16)


def _stick_breaking(pb):
    bsz, seq, _ = pb.shape
    npair = N_HEADS_B // PAIR
    spec = lambda g: pl.BlockSpec((1, seq, LANES), lambda b, p, g=g: (b, 0, g * npair + p))
    return pl.pallas_call(
        _sb_kernel,
        grid=(bsz, npair),
        in_specs=[spec(0), spec(1), spec(2), spec(3)],
        out_specs=pl.BlockSpec((1, seq, LANES), lambda b, p: (b, 0, p)),
        out_shape=jax.ShapeDtypeStruct((bsz, seq, B_WIDTH), BF16),
        compiler_params=pltpu.CompilerParams(dimension_semantics=("arbitrary", "arbitrary"),
                                             vmem_limit_bytes=VMEM_LIMIT),
        name="stick_breaking",
    )(pb, pb, pb, pb)


DW_BLOCK = 128
DW_SLOT = 2
DW_PREP_ROWS = 256


def _dw_kernel(q_ref, k_ref, v_ref, z_ref, qw_ref, kw_ref, cos_ref, sin_ref, perm_ref, mean_ref, o_ref,
               qf, kf, vf, q4f, k4f, v4f, qn, kn, qg4, kg4, vg4, qg16, kg16, vg16,
               m1_md, m2_md, l_md, acc_md, m1_st, m2_st, l_st, acc_st):
    seq = q_ref.shape[1]
    head_lo = _head_lo((1, LANES))
    n_rows = DW_PREP_ROWS

    def prep(i, _):
        blocks = [pl.ds(pl.multiple_of((2 * i + b) * n_rows, n_rows), n_rows) for b in range(2)]
        xs = [(ref[0, rows, :].astype(F32), w_ref[...], rows)
              for rows in blocks for ref, w_ref in ((q_ref, qw_ref), (k_ref, kw_ref))]
        ms = [_dot((x * x).astype(BF16), mean_ref[...]) for x, _, _ in xs]
        ys = [x * lax.rsqrt(m + RMS_EPS) * w for (x, w, _), m in zip(xs, ms)]
        partner = [_dot(jnp.concatenate(_split_bf16(y), axis=1), perm_ref[...]) for y in ys]
        ys = [y * cos_ref[rows, :] + pr * sin_ref[rows, :] for y, pr, (_, _, rows) in zip(ys, partner, xs)]
        for b, rows in enumerate(blocks):
            q, k = ys[2 * b], ys[2 * b + 1]
            qf[rows, :] = q
            kf[rows, :] = k
            qn[rows, :] = q.astype(BF16)
            kn[rows, :] = k.astype(BF16)
            vf[rows, :] = v_ref[0, rows, :].astype(F32)
        return 0

    lax.fori_loop(0, seq // (2 * n_rows), prep, 0)

    qi = lax.broadcasted_iota(jnp.int32, (DW_BLOCK, DW_BLOCK), 0)
    kj = lax.broadcasted_iota(jnp.int32, (DW_BLOCK, DW_BLOCK), 1)
    own_ok = kj <= qi
    prev_ok = kj >= qi
    both_ok = jnp.concatenate([prev_ok, own_ok], axis=1)

    def pair_of(a, b):
        return jnp.where(head_lo, a, b)

    d_mid, d_far = DILATIONS[1], DILATIONS[2]
    len_mid, len_far = seq // d_mid, seq // d_far
    for src, mid_f, mid_b, far_b in ((qf, q4f, qg4, qg16), (kf, k4f, kg4, kg16), (vf, v4f, vg4, vg16)):
        for r in range(d_mid):
            x = src[pl.ds(r, len_mid, stride=d_mid), :]
            mid_f[pl.ds(r * len_mid, len_mid), :] = x
            mid_b[pl.ds(r * len_mid, len_mid), :] = x.astype(BF16)
        for r in range(d_far):
            x = mid_f[pl.ds((r % d_mid) * len_mid + r // d_mid, len_far, stride=d_far // d_mid), :]
            far_b[pl.ds(r * len_far, len_far), :] = x.astype(BF16)
    sources = {d_far: (qg16, kg16, lambda ks: vg16[ks, :]), d_mid: (qg4, kg4, lambda ks: vg4[ks, :]),
               1: (qn, kn, lambda ks: v_ref[0, ks, :])}

    state_mid, state_nat = (m1_md, m2_md, l_md, acc_md), (m1_st, m2_st, l_st, acc_st)

    def nat_rows(it):
        return pl.ds(it["start"], DW_BLOCK, stride=it["d"]) if it["d"] > 1 else pl.ds(it["start"], DW_BLOCK)

    def state_in(it):
        if it["d"] == d_mid:
            return state_mid, pl.ds(it["c0"], DW_BLOCK)
        return state_nat, nat_rows(it)

    def state_out(it):
        if it["d"] == d_far:
            r = it["start"]
            return state_mid, pl.ds((r % d_mid) * len_mid + r // d_mid, DW_BLOCK, stride=d_far // d_mid)
        return state_nat, nat_rows(it)

    def stage_scores(it):
        q_src, k_src, _ = sources[it["d"]]
        c0 = it["c0"]
        keys = pl.ds(c0 - DW_BLOCK, 2 * DW_BLOCK) if it["has_prev"] else pl.ds(c0, DW_BLOCK)
        ok = both_ok if it["has_prev"] else own_ok
        q = q_src[pl.ds(c0, DW_BLOCK), :]
        zero = jnp.zeros_like(q)
        k = k_src[keys, :]
        s = [jnp.where(ok, _dot_nt(qh, k), NEG) for qh in (jnp.where(head_lo, q, zero), jnp.where(head_lo, zero, q))]
        return dict(s=s, m_blk=[jnp.max(sh, axis=1, keepdims=True) for sh in s], keys=keys)

    def stage_softmax(it, st):
        width = 2 if it["has_prev"] else 1
        if it["d"] == d_far:
            m_h = [jnp.broadcast_to(m, (DW_BLOCK, LANES)) for m in st["m_blk"]]
            alpha = None
        else:
            (m1_in, m2_in, _, _), rows_in = state_in(it)
            m_old = (m1_in[rows_in, :], m2_in[rows_in, :])
            m_h = [jnp.maximum(o, b) for o, b in zip(m_old, st["m_blk"])]
            alpha = jnp.exp2(pair_of(m_old[0], m_old[1]) - pair_of(m_h[0], m_h[1]))
        p = [jnp.exp2(sh - jnp.concatenate([mh] * width, axis=1)).astype(BF16) for sh, mh in zip(st["s"], m_h)]
        return dict(p=p, m_h=m_h, alpha=alpha, keys=st["keys"])

    def stage_values(it, st):
        _, _, v_of = sources[it["d"]]
        v = v_of(st["keys"])
        vo = jnp.concatenate([v, jnp.ones_like(v)], axis=1)
        pvl = [_dot(ph, vo) for ph in st["p"]]
        pv = pair_of(pvl[0][:, :LANES], pvl[1][:, :LANES])
        l_new = pair_of(pvl[0][:, LANES:], pvl[1][:, LANES:])
        if st["alpha"] is not None:
            (_, _, l_in, acc_in), rows_in = state_in(it)
            l_new = st["alpha"] * l_in[rows_in, :] + l_new
            pv = st["alpha"] * acc_in[rows_in, :] + pv
        if it["d"] == 1:
            nat = nat_rows(it)
            zg = z_ref[0, nat, :].astype(F32)
            o_ref[0, nat, :] = (pv / l_new * (zg * _sigmoid(zg))).astype(BF16)
        else:
            (m1_out, m2_out, l_out, acc_out), rows_out = state_out(it)
            m1_out[rows_out, :] = st["m_h"][0]
            m2_out[rows_out, :] = st["m_h"][1]
            l_out[rows_out, :] = l_new
            acc_out[rows_out, :] = pv

    groups = [[dict(d=d_far, c0=r * len_far, start=r, has_prev=False) for r in range(d_far)],
              [dict(d=d_mid, c0=r * len_mid + j * DW_BLOCK, start=r + j * DW_BLOCK * d_mid, has_prev=j > 0)
               for j in range(len_mid // DW_BLOCK) for r in range(d_mid)],
              [dict(d=1, c0=j * DW_BLOCK, start=j * DW_BLOCK, has_prev=j > 0) for j in range(seq // DW_BLOCK)]]
    slots = []
    for g in groups:
        slots += [g[i:i + DW_SLOT] for i in range(0, len(g), DW_SLOT)] + [[]]
    slots += [[]]
    scored, soft = {}, {}
    for step, slot in enumerate(slots):
        for n, it in enumerate(slot):
            scored[step, n] = stage_scores(it)
        if step >= 1:
            for n, it in enumerate(slots[step - 1]):
                soft[step - 1, n] = stage_softmax(it, scored.pop((step - 1, n)))
        if step >= 2:
            for n, it in enumerate(slots[step - 2]):
                stage_values(it, soft.pop((step - 2, n)))


def _rope_tables(seq):
    half = ROPE_DIM // 2
    inv_freq = ROPE_THETA ** (-jnp.arange(half, dtype=F32) / half)
    ang = jnp.arange(seq, dtype=jnp.int32).astype(F32)[:, None] * inv_freq[None, :]
    cos, sin = jnp.cos(ang), jnp.sin(ang)
    ones = jnp.ones((seq, HEAD_DIM - ROPE_DIM), F32)
    zeros_t = jnp.zeros((seq, HEAD_DIM - ROPE_DIM), F32)
    cos_h = jnp.concatenate([cos, cos, ones], axis=1)
    sin_h = jnp.concatenate([-sin, sin, zeros_t], axis=1)
    tile = lambda t: jnp.tile(t, (1, PAIR))
    src = lax.broadcasted_iota(jnp.int32, (LANES, LANES), 0)
    dst = lax.broadcasted_iota(jnp.int32, (LANES, LANES), 1)
    in_head = dst % HEAD_DIM
    want = jnp.where(in_head < half, dst + half, jnp.where(in_head < ROPE_DIM, dst - half, -1))
    perm = (src == want).astype(BF16)
    return tile(cos_h), tile(sin_h), jnp.concatenate([perm, perm], axis=0)


def _dilated(pc, q_norm_w, k_norm_w):
    bsz, seq, _ = pc.shape
    npair = N_HEADS_C // PAIR
    assert seq % (DW_BLOCK * DILATIONS[-1]) == 0 and DILATIONS[-1] % DILATIONS[1] == 0
    cos, sin, perm = _rope_tables(seq)
    qw = jnp.tile(q_norm_w.astype(F32), PAIR)[None, :] * QSCALE2
    kw = jnp.tile(k_norm_w.astype(F32), PAIR)[None, :]
    head_mean = _ones_bd() * jnp.asarray(1.0 / HEAD_DIM, BF16)
    spec = lambda g: pl.BlockSpec((1, seq, LANES), lambda b, p, g=g: (b, 0, g * npair + p))
    fixed = lambda shape: pl.BlockSpec(shape, lambda b, p: (0, 0))
    f32_rows, bf16_rows = pltpu.VMEM((seq, LANES), F32), pltpu.VMEM((seq, LANES), BF16)
    return pl.pallas_call(
        _dw_kernel,
        grid=(bsz, npair),
        in_specs=[spec(0), spec(1), spec(2), spec(3), fixed((1, LANES)), fixed((1, LANES)),
                  fixed((seq, LANES)), fixed((seq, LANES)), fixed((2 * LANES, LANES)), fixed((LANES, LANES))],
        out_specs=pl.BlockSpec((1, seq, LANES), lambda b, p: (b, 0, p)),
        out_shape=jax.ShapeDtypeStruct((bsz, seq, C_WIDTH), BF16),
        scratch_shapes=[f32_rows] * 6 + [bf16_rows] * 8 + [f32_rows] * 8,
        compiler_params=pltpu.CompilerParams(dimension_semantics=("arbitrary", "arbitrary"),
                                             vmem_limit_bytes=VMEM_LIMIT),
        name="dilated_window",
    )(pc, pc, pc, pc, qw, kw, cos, sin, perm, head_mean)


GDN_ROWS = 256
GDN_HALO = 16
GDN_GROUP = 32


def _gdn_constants():
    npair = N_HEADS_A // PAIR
    iota = lambda shape, dim: lax.broadcasted_iota(jnp.int32, shape, dim)
    n_shift = CONV_WIDTH - 1
    sr, sc = iota((n_shift * GDN_ROWS, GDN_ROWS), 0), iota((n_shift * GDN_ROWS, GDN_ROWS), 1)
    shift = sc == sr % GDN_ROWS - (CONV_WIDTH - 1) + sr // GDN_ROWS
    hr, hc = iota((n_shift * SUBLANES, GDN_HALO), 0), iota((n_shift * SUBLANES, GDN_HALO), 1)
    halo = hc == GDN_HALO + hr % SUBLANES - (CONV_WIDTH - 1) + hr // SUBLANES
    ri, ci = iota((GDN_ROWS, 2 * GDN_ROWS), 0), iota((GDN_ROWS, 2 * GDN_ROWS), 1) % GDN_ROWS
    cum = (ri // GDN_CHUNK == ci // GDN_CHUNK) & (ci <= ri)
    er, ec = iota((2 * LANES, 2 * npair * LANES), 0) % LANES, iota((2 * LANES, 2 * npair * LANES), 1)
    expand = er == (ec // LANES % 2) * N_HEADS_A + 2 * (ec // (2 * LANES)) + ec % LANES // HEAD_DIM
    same_head = iota((2 * LANES, 2 * LANES), 0) // HEAD_DIM == iota((2 * LANES, 2 * LANES), 1) // HEAD_DIM
    bd4 = jnp.where(same_head, jnp.where(iota((2 * LANES, 2 * LANES), 1) < LANES, float(HEAD_DIM), 1.0), 0.0)
    head_mean = _ones_bd() * jnp.asarray(1.0 / HEAD_DIM, BF16)
    return [m.astype(BF16) for m in (shift, halo, cum, expand, bd4, head_mean)]


def _gdn_kernel(pa_ref, ba_ref, cw_ref, alog_ref, dtb_ref, gnw_ref, shift_ref, halo_ref, cum_ref, expand_ref,
                bd4_ref, mean_ref, o_ref, qn_s, kn_s, v_s, be_s, gc_s, qt_s, o0_s, nf_s, kw_s, gl_s, zg_s):
    seq = pa_ref.shape[1]
    n_chunks = seq // GDN_CHUNK
    npair = N_HEADS_A // PAIR
    lane = lax.broadcasted_iota(jnp.int32, (1, LANES), 1)
    head_lo = lane < HEAD_DIM
    bd_mask = (lax.broadcasted_iota(jnp.int32, (LANES, LANES), 0) // HEAD_DIM
               == lax.broadcasted_iota(jnp.int32, (LANES, LANES), 1) // HEAD_DIM)

    ii = lax.broadcasted_iota(jnp.int32, (GDN_CHUNK, LANES), 0)
    jl = lax.broadcasted_iota(jnp.int32, (GDN_CHUNK, LANES), 1) % HEAD_DIM
    incl = ii >= jl
    strict = ii > jl
    eye_pair = ii == jl
    eye_f = eye_pair.astype(F32)

    def qkv_cols(p):
        return [slice(t * A_WIDTH + p * LANES, t * A_WIDTH + (p + 1) * LANES) for t in range(3)]

    def prep_taps(rows, p):
        return _dot(shift_ref[...], jnp.concatenate([pa_ref[0, rows, c] for c in qkv_cols(p)], axis=1))

    def prep_gates(i, rows):
        r0 = i * GDN_ROWS
        halo_rows = pl.ds(pl.multiple_of(jnp.maximum(r0 - GDN_HALO, 0), GDN_HALO), GDN_HALO)
        halo = pa_ref[0, halo_rows, 0:3 * A_WIDTH]
        halo = jnp.where(i > 0, halo, jnp.zeros_like(halo))
        edge = _dot(halo_ref[...], halo)

        ba = ba_ref[0, rows, :]
        g = -jnp.exp(alog_ref[...]) * _softplus(ba + dtb_ref[...])
        g = jnp.where((lane >= N_HEADS_A) & (lane < 2 * N_HEADS_A), g, 0.0)
        hi, lo = _split_bf16(g)
        gc = _dot(cum_ref[...], jnp.concatenate([hi, lo], axis=0))
        nar = jnp.where(lane < N_HEADS_A, _sigmoid(ba), gc)
        hi, lo = _split_bf16(nar)
        return edge, _dot(jnp.concatenate([hi, lo], axis=1), expand_ref[...])

    def prep_conv(rows, taps, edge, wide):
        def conv_silu(p, t):
            c = qkv_cols(p)[t]
            w = cw_ref[:, c]
            y = w[CONV_WIDTH - 1:CONV_WIDTH, :] * pa_ref[0, rows, c].astype(F32)
            y0 = 0.0
            for kk in range(CONV_WIDTH - 1):
                y = y + w[kk:kk + 1, :] * taps[p][kk * GDN_ROWS:(kk + 1) * GDN_ROWS, t * LANES:(t + 1) * LANES]
                y0 = y0 + w[kk:kk + 1, :] * edge[kk * SUBLANES:(kk + 1) * SUBLANES, c]
            y = jnp.concatenate([y[:SUBLANES, :] + y0, y[SUBLANES:, :]], axis=0)
            return y * _sigmoid(y)

        ys = [[conv_silu(p, t) for t in range(3)] for p in range(npair)]
        ss = [_dot(jnp.concatenate([y[0] * y[0], y[1] * y[1]], axis=1).astype(BF16), bd4_ref[...]) for y in ys]
        for p in range(npair):
            qn_s[p, rows, :] = ys[p][0] * lax.rsqrt(ss[p][:, :LANES] + HEAD_DIM * RMS_EPS)
            kn_s[p, rows, :] = ys[p][1] * lax.rsqrt(ss[p][:, LANES:] + RMS_EPS)
            v_s[p, rows, :] = ys[p][2]
            be_s[p, rows, :] = wide[:, 2 * p * LANES:(2 * p + 1) * LANES]
            gc_s[p, rows, :] = wide[:, (2 * p + 1) * LANES:(2 * p + 2) * LANES]

    def bd(ms):
        return [_block_diag(m, head_lo) for m in ms]

    def chunk_group(p, i, _):
        cs = [i * GDN_GROUP + gi for gi in range(GDN_GROUP)]
        rows = [pl.ds(pl.multiple_of(c * GDN_CHUNK, GDN_CHUNK), GDN_CHUNK) for c in cs]
        qn = [qn_s[p, r, :] for r in rows]
        kn = [kn_s[p, r, :] for r in rows]
        v = [v_s[p, r, :] for r in rows]
        be = [be_s[p, r, :] for r in rows]
        gc = [gc_s[p, r, :] for r in rows]
        kb = [a * b for a, b in zip(kn, be)]
        ap = [_dot_nt(jnp.concatenate([a, b], axis=0).astype(BF16), m) for a, b, m in zip(kb, qn, bd(kn))]
        g_row = [jnp.sum(jnp.where(eye_pair, x, 0.0), axis=0, keepdims=True) for x in gc]
        decay = [jnp.exp(jnp.where(incl, x - y, NEG)) for x, y in zip(gc, g_row)]
        x = [jnp.where(strict, -a[:GDN_CHUNK] * d, 0.0) for a, d in zip(ap, decay)]
        pm = [a[GDN_CHUNK:] * d for a, d in zip(ap, decay)]
        tm = [eye_f + a for a in x]
        xr = [_dot(a.astype(BF16), m) for a, m in zip(x, bd(x))]
        for r in range(1, 6):
            rhs = bd(xr)
            if r < 5:
                y = [_dot(jnp.concatenate([a, b], axis=0).astype(BF16), m) for a, b, m in zip(xr, tm, rhs)]
                xr = [a[:GDN_CHUNK] for a in y]
                tm = [a + b[GDN_CHUNK:] for a, b in zip(tm, y)]
            else:
                tm = [a + _dot(a.astype(BF16), m) for a, m in zip(tm, rhs)]
        eg = [jnp.exp(a) for a in gc]
        vb = bd([a * b for a, b in zip(v, be)])
        kbg = bd([a * b for a, b in zip(kb, eg)])
        uw = [_dot(a.astype(BF16), jnp.concatenate([b, c], axis=1)) for a, b, c in zip(tm, vb, kbg)]
        u_bd = bd([a[:, :LANES] for a in uw])
        w_bd = bd([a[:, LANES:] for a in uw])
        puw = [_dot(a.astype(BF16), jnp.concatenate([b, c], axis=1)) for a, b, c in zip(pm, u_bd, w_bd)]
        g_last = [a[GDN_CHUNK - 1:GDN_CHUNK, :] for a in gc]
        kg = [a * jnp.exp(b - c) for a, b, c in zip(kn, g_last, gc)]
        kuw = [_dot_tn(a.astype(BF16), b.astype(BF16)) for a, b in zip(kg, uw)]
        for c, r, q, e, a, b, gl in zip(cs, rows, qn, eg, puw, kuw, g_last):
            qt_s[p, r, :] = (q * e - a[:, LANES:]).astype(BF16)
            o0_s[p, r, :] = a[:, :LANES]
            nf_s[p, c] = jnp.where(bd_mask, b[:, :LANES], 0.0)
            kw_s[p, c] = jnp.where(bd_mask, b[:, LANES:], 0.0).astype(BF16)
            gl_s[p, c] = jnp.broadcast_to(jnp.exp(gl), (SUBLANES, LANES))
        return 0

    def per_pair(p, _):
        lax.fori_loop(0, n_chunks // GDN_GROUP, functools.partial(chunk_group, p), 0)
        return 0

    step = pl.program_id(0)
    last_step = pl.num_programs(0) - 1
    steps_per_rows = GDN_ROWS // GDN_CHUNK

    @pl.when(step == 0)
    def _():
        def zero(c, _):
            rows = pl.ds(pl.multiple_of(c * GDN_CHUNK, GDN_CHUNK), GDN_CHUNK)
            zg_s[rows, :] = jnp.zeros((GDN_CHUNK, A_WIDTH), BF16)
            for p in range(npair):
                qt_s[p, rows, :] = jnp.zeros((GDN_CHUNK, LANES), BF16)
                o0_s[p, rows, :] = jnp.zeros((GDN_CHUNK, LANES), F32)
                nf_s[p, c] = jnp.zeros((LANES, LANES), F32)
                kw_s[p, c] = jnp.zeros((LANES, LANES), BF16)
                gl_s[p, c] = jnp.zeros((SUBLANES, LANES), F32)
            return 0

        lax.fori_loop(0, n_chunks, zero, 0)

    def rows_step(with_prep, i, states):
        rows = pl.ds(pl.multiple_of(i * GDN_ROWS, GDN_ROWS), GDN_ROWS)
        outs = [[] for _ in range(npair)]
        taps = []
        for s in range(steps_per_rows):
            c = i * steps_per_rows + s
            crow = pl.ds(pl.multiple_of(c * GDN_CHUNK, GDN_CHUNK), GDN_CHUNK)
            sb = [st.astype(BF16) for st in states]
            ks = [_dot(kw_s[p, c], sb[p]) for p in range(npair)]
            for p in range(npair):
                outs[p].append(_dot(qt_s[p, crow, :], sb[p]) + o0_s[p, crow, :])
            states = tuple(gl_s[p, c][0:1, :] * states[p] + nf_s[p, c] - ks[p] for p in range(npair))
            if with_prep and s < npair:
                taps.append(prep_taps(rows, s))
        if with_prep:
            edge, wide = prep_gates(i, rows)
        for p in range(npair):
            o = jnp.concatenate(outs[p], axis=0)
            ms = _dot((o * o).astype(BF16), mean_ref[...])
            y = o * lax.rsqrt(ms + RMS_EPS) * gnw_ref[...]
            zg = zg_s[rows, p * LANES:(p + 1) * LANES].astype(F32)
            o_ref[0, rows, p * LANES:(p + 1) * LANES] = (y * (zg * _sigmoid(zg))).astype(BF16)
        if with_prep:
            prep_conv(rows, taps, edge, wide)
            zg_s[rows, :] = pa_ref[0, rows, 3 * A_WIDTH:4 * A_WIDTH]
        return states

    zero_states = tuple(jnp.zeros((LANES, LANES), F32) for _ in range(npair))

    @pl.when(step < last_step)
    def _():
        lax.fori_loop(0, seq // GDN_ROWS, functools.partial(rows_step, True), zero_states)
        lax.fori_loop(0, npair, per_pair, 0)

    @pl.when(step == last_step)
    def _():
        lax.fori_loop(0, seq // GDN_ROWS, functools.partial(rows_step, False), zero_states)


def _gdn(pa, ba, conv_w, a_log, dt_bias, gdn_norm_w):
    bsz, seq, _ = pa.shape
    npair = N_HEADS_A // PAIR
    n_chunks = seq // GDN_CHUNK
    pad = lambda vec: jnp.zeros((1, LANES), F32).at[0, N_HEADS_A:2 * N_HEADS_A].set(vec.astype(F32))
    gnw = jnp.tile(gdn_norm_w.astype(F32), PAIR)[None, :]
    fixed = lambda shape: pl.BlockSpec(shape, lambda b: (0,) * len(shape))
    per_pair_f32 = pltpu.VMEM((npair, seq, LANES), F32)
    consts = _gdn_constants()
    return pl.pallas_call(
        _gdn_kernel,
        grid=(bsz + 1,),
        in_specs=[pl.BlockSpec((1, seq, PA_COLS), lambda b: (jnp.minimum(b, bsz - 1), 0, 0)),
                  pl.BlockSpec((1, seq, BA_COLS), lambda b: (jnp.minimum(b, bsz - 1), 0, 0)),
                  fixed((CONV_WIDTH, 3 * A_WIDTH)), fixed((1, LANES)), fixed((1, LANES)), fixed((1, LANES))]
        + [fixed(m.shape) for m in consts],
        out_specs=pl.BlockSpec((1, seq, A_WIDTH), lambda b: (jnp.maximum(b - 1, 0), 0, 0)),
        out_shape=jax.ShapeDtypeStruct((bsz, seq, A_WIDTH), BF16),
        scratch_shapes=[
            per_pair_f32, per_pair_f32, per_pair_f32,
            per_pair_f32, per_pair_f32,
            pltpu.VMEM((npair, seq, LANES), BF16),
            per_pair_f32,
            pltpu.VMEM((npair, n_chunks, LANES, LANES), F32),
            pltpu.VMEM((npair, n_chunks, LANES, LANES), BF16),
            pltpu.VMEM((npair, n_chunks, SUBLANES, LANES), F32),
            pltpu.VMEM((seq, A_WIDTH), BF16),
        ],
        compiler_params=pltpu.CompilerParams(dimension_semantics=("arbitrary",), vmem_limit_bytes=VMEM_LIMIT),
        name="gated_delta",
    )(pa, ba, conv_w.astype(F32), pad(a_log), pad(dt_bias), gnw, *consts)


ROW_TILE = 512
OUT_ROW_TILE = 2048


def kernel(x, norm_w, w_in, conv_w, a_log, dt_bias, gdn_norm_w, q_norm_w, k_norm_w, w_out):
    bsz, seq, _ = x.shape
    assert w_in.shape[1:] == (D_MODEL, IN_COLS)
    shape3 = lambda t: t.reshape(bsz, seq, t.shape[-1])
    flat = lambda t: t.reshape(bsz * seq, t.shape[-1])
    x2d = flat(x)
    w_in = w_in.astype(F32)
    norm_w = norm_w.astype(F32)
    for layer in range(norm_w.shape[0]):
        pa, ba, pb, pc = _in_proj(x2d, norm_w[layer][None, :], w_in, layer, ROW_TILE)
        oa = _gdn(shape3(pa), shape3(ba), conv_w[layer], a_log[layer], dt_bias[layer], gdn_norm_w[layer])
        ob = _stick_breaking(shape3(pb))
        oc = _dilated(shape3(pc), q_norm_w[layer], k_norm_w[layer])
        x2d = _out_proj(x2d, flat(oa), flat(ob), flat(oc), w_out[layer].astype(BF16), OUT_ROW_TILE)
    return shape3(x2d)
```

```python
import functools

import jax
import jax.numpy as jnp
from jax import lax
from jax.experimental import pallas as pl
from jax.experimental.pallas import tpu as pltpu

F32 = jnp.float32
BF16 = jnp.bfloat16

D_MODEL = 1024
HEAD_DIM = 64
N_HEADS_A, N_HEADS_B, N_HEADS_C = 6, 4, 6
A_WIDTH, B_WIDTH, C_WIDTH = N_HEADS_A * HEAD_DIM, N_HEADS_B * HEAD_DIM, N_HEADS_C * HEAD_DIM
CONV_WIDTH = 4
GDN_CHUNK = 64
ROPE_DIM = HEAD_DIM // 4
ROPE_THETA = 500000.0
DILATIONS = (1, 4, 16)
RMS_EPS = 1e-6

LANES = 128
SUBLANES = 8
PAIR = LANES // HEAD_DIM
NEG = -1e30

PA_COLS = 4 * A_WIDTH
BA_COLS = LANES
PB_COLS = 4 * B_WIDTH
PC_COLS = 4 * C_WIDTH
IN_COLS = PA_COLS + 2 * N_HEADS_A + PB_COLS + PC_COLS
W1_COLS = PA_COLS + BA_COLS + PB_COLS + PC_COLS

VMEM_LIMIT = 56 * 1024 * 1024

LOG2E = 1.4426950408889634
QSCALE2 = HEAD_DIM ** -0.5 * LOG2E


def _dot(a, b):
    return jnp.dot(a, b, preferred_element_type=F32)


def _dot_nt(a, b):
    return lax.dot_general(a, b, (((1,), (1,)), ((), ())), preferred_element_type=F32)


def _dot_tn(a, b):
    return lax.dot_general(a, b, (((0,), (0,)), ((), ())), preferred_element_type=F32)


def _aligned(x, m):
    return x if isinstance(x, int) else pl.multiple_of(x, m)


def _sigmoid(x):
    return 1.0 / (1.0 + jnp.exp2(x * (-LOG2E)))


def _softplus(x):
    return jnp.maximum(x, 0.0) + jnp.log(1.0 + jnp.exp(-jnp.abs(x)))


def _split_bf16(x):
    hi = x.astype(BF16)
    lo = (x - hi.astype(F32)).astype(BF16)
    return hi, lo


def _head_lo(shape):
    return lax.broadcasted_iota(jnp.int32, shape, len(shape) - 1) < HEAD_DIM


def _block_diag(m, head_lo):
    z = jnp.zeros_like(m)
    return jnp.concatenate([jnp.where(head_lo, m, z), jnp.where(head_lo, z, m)], axis=0).astype(BF16)


def _ones_bd():
    r = lax.broadcasted_iota(jnp.int32, (LANES, LANES), 0) // HEAD_DIM
    c = lax.broadcasted_iota(jnp.int32, (LANES, LANES), 1) // HEAD_DIM
    return (r == c).astype(BF16)


W_ROWS = 128


def _in_proj_kernel(x_ref, nw_ref, w_ref, pa_ref, ba_ref, pb_ref, pc_ref, wb):
    @pl.when(pl.program_id(0) == 0)
    def _():
        def rows(i, _):
            r = pl.ds(pl.multiple_of(i * W_ROWS, W_ROWS), W_ROWS)
            wb[r, 0:PA_COLS + BA_COLS] = w_ref[0, r, 0:PA_COLS + BA_COLS].astype(BF16)
            wb[r, PA_COLS + BA_COLS:W1_COLS] = w_ref[0, r, PA_COLS + 2 * N_HEADS_A:IN_COLS].astype(BF16)
            return 0

        lax.fori_loop(0, D_MODEL // W_ROWS, rows, 0)

    x = x_ref[...]
    ms = jnp.mean(x * x, axis=-1, keepdims=True)
    h = (x * lax.rsqrt(ms + RMS_EPS) * nw_ref[...]).astype(BF16)
    c0 = 0
    pa_ref[...] = _dot(h, wb[:, c0:c0 + PA_COLS]).astype(BF16)
    c0 += PA_COLS
    ba_ref[...] = _dot(h, wb[:, c0:c0 + BA_COLS])
    c0 += BA_COLS
    col_b = lax.broadcasted_iota(jnp.int32, (1, PB_COLS), 1)
    pb_ref[...] = (_dot(h, wb[:, c0:c0 + PB_COLS]) * jnp.where(col_b < B_WIDTH, QSCALE2, 1.0)).astype(BF16)
    c0 += PB_COLS
    pc_ref[...] = _dot(h, wb[:, c0:c0 + PC_COLS]).astype(BF16)


def _in_proj(x2d, norm_w, w_in, layer, tm):
    n = x2d.shape[0]
    row = lambda i: (i, 0)
    return pl.pallas_call(
        _in_proj_kernel,
        grid=(n // tm,),
        in_specs=[pl.BlockSpec((tm, D_MODEL), row),
                  pl.BlockSpec((1, D_MODEL), lambda i: (0, 0)),
                  pl.BlockSpec((1, D_MODEL, IN_COLS), lambda i: (layer, 0, 0), pipeline_mode=pl.Buffered(1))],
        out_specs=[pl.BlockSpec((tm, PA_COLS), row), pl.BlockSpec((tm, BA_COLS), row),
                   pl.BlockSpec((tm, PB_COLS), row), pl.BlockSpec((tm, PC_COLS), row)],
        out_shape=[jax.ShapeDtypeStruct((n, PA_COLS), BF16), jax.ShapeDtypeStruct((n, BA_COLS), F32),
                   jax.ShapeDtypeStruct((n, PB_COLS), BF16), jax.ShapeDtypeStruct((n, PC_COLS), BF16)],
        scratch_shapes=[pltpu.VMEM((D_MODEL, W1_COLS), BF16)],
        compiler_params=pltpu.CompilerParams(dimension_semantics=("arbitrary",), vmem_limit_bytes=VMEM_LIMIT),
        name="in_proj",
    )(x2d, norm_w, w_in)


def _out_proj_kernel(x_ref, oa_ref, ob_ref, oc_ref, w_ref, o_ref):
    acc = _dot(oa_ref[...], w_ref[0:A_WIDTH, :])
    acc = acc + _dot(ob_ref[...], w_ref[A_WIDTH:A_WIDTH + B_WIDTH, :])
    acc = acc + _dot(oc_ref[...], w_ref[A_WIDTH + B_WIDTH:, :])
    o_ref[...] = x_ref[...] + acc


def _out_proj(x2d, oa, ob, oc, w_out, tm):
    n = x2d.shape[0]
    row = lambda i: (i, 0)
    fixed = lambda i: (0, 0)
    return pl.pallas_call(
        _out_proj_kernel,
        grid=(n // tm,),
        in_specs=[pl.BlockSpec((tm, D_MODEL), row), pl.BlockSpec((tm, A_WIDTH), row),
                  pl.BlockSpec((tm, B_WIDTH), row), pl.BlockSpec((tm, C_WIDTH), row),
                  pl.BlockSpec((D_MODEL, D_MODEL), fixed)],
        out_specs=pl.BlockSpec((tm, D_MODEL), row),
        out_shape=jax.ShapeDtypeStruct((n, D_MODEL), F32),
        compiler_params=pltpu.CompilerParams(dimension_semantics=("arbitrary",), vmem_limit_bytes=VMEM_LIMIT),
        name="out_proj",
    )(x2d, oa, ob, oc, w_out)


SB_BLOCK = 256


def _sb_kernel(q_ref, k_ref, v_ref, z_ref, o_ref):
    seq = q_ref.shape[1]
    nb = seq // SB_BLOCK
    head_lo = _head_lo((1, LANES))
    row = lax.broadcasted_iota(jnp.int32, (SB_BLOCK, SB_BLOCK), 0)
    col = lax.broadcasted_iota(jnp.int32, (SB_BLOCK, SB_BLOCK), 1)
    earlier = col < row
    neg_suffix = jnp.where(row > col, -1.0, 0.0).astype(BF16)

    def rows(b):
        return pl.ds(b * SB_BLOCK, SB_BLOCK)

    q_heads = {}

    def scores(i, kt):
        if i not in q_heads:
            q = q_ref[0, rows(i), :]
            zero = jnp.zeros_like(q)
            q_heads[i] = (jnp.where(head_lo, q, zero), jnp.where(head_lo, zero, q))
        k = k_ref[0, rows(kt), :]
        return [_dot_nt(qh, k) for qh in q_heads[i]]

    def log_weights(z2, diag):
        nlk = [jnp.maximum(x, 0.0) + jnp.log2(1.0 + jnp.exp2(-jnp.abs(x))) for x in z2]
        if diag:
            nlk = [jnp.where(earlier, x, 0.0) for x in nlk]
        later = [_dot(x.astype(BF16), neg_suffix) for x in nlk]
        pre = [(x - y) + w for x, y, w in zip(z2, nlk, later)]
        if diag:
            pre = [jnp.where(earlier, x, NEG) for x in pre]
        return pre, [w[:, 0:1] - y[:, 0:1] for y, w in zip(nlk, later)]

    def accumulate(pre, carry, acc, kt):
        v = v_ref[0, rows(kt), :]
        w = [jnp.exp2(x + c).astype(BF16) for x, c in zip(pre, carry)]
        return [a + _dot(x, v) for a, x in zip(acc, w)]

    tiles = [(i, kt) for i in range(nb) for kt in range(i, -1, -1)]
    z2, lw = {}, {}
    carry = acc = None
    for step in range(len(tiles) + 2):
        if step < len(tiles):
            z2[step] = scores(*tiles[step])
        if 0 <= step - 1 < len(tiles):
            i, kt = tiles[step - 1]
            lw[step - 1] = log_weights(z2.pop(step - 1), i == kt)
        if 0 <= step - 2 < len(tiles):
            i, kt = tiles[step - 2]
            pre, tot = lw.pop(step - 2)
            if i == kt:
                carry = [jnp.zeros((SB_BLOCK, 1), F32)] * PAIR
                acc = [jnp.zeros((SB_BLOCK, LANES), F32)] * PAIR
            acc = accumulate(pre, carry, acc, kt)
            carry = [c + t for c, t in zip(carry, tot)]
            if kt == 0:
                o = jnp.where(head_lo, acc[0], acc[1])
                zg = z_ref[0, rows(i), :].astype(F32)
                o_ref[0, rows(i), :] = (o * (zg * _sigmoid(zg))).astype(BF16)


def _stick_breaking(pb):
    bsz, seq, _ = pb.shape
    npair = N_HEADS_B // PAIR
    spec = lambda g: pl.BlockSpec((1, seq, LANES), lambda b, p, g=g: (b, 0, g * npair + p))
    return pl.pallas_call(
        _sb_kernel,
        grid=(bsz, npair),
        in_specs=[spec(0), spec(1), spec(2), spec(3)],
        out_specs=pl.BlockSpec((1, seq, LANES), lambda b, p: (b, 0, p)),
        out_shape=jax.ShapeDtypeStruct((bsz, seq, B_WIDTH), BF16),
        compiler_params=pltpu.CompilerParams(dimension_semantics=("arbitrary", "arbitrary"),
                                             vmem_limit_bytes=VMEM_LIMIT),
        name="stick_breaking",
    )(pb, pb, pb, pb)


DW_BLOCK = 128
DW_SLOT = 2
DW_PREP_ROWS = 256


def _dw_kernel(q_ref, k_ref, v_ref, z_ref, qw_ref, kw_ref, cos_ref, sin_ref, perm_ref, mean_ref, o_ref,
               qf, kf, vf, q4f, k4f, v4f, qn, kn, qg4, kg4, vg4, qg16, kg16, vg16,
               m1_md, m2_md, l_md, acc_md, m1_st, m2_st, l_st, acc_st):
    seq = q_ref.shape[1]
    head_lo = _head_lo((1, LANES))
    n_rows = DW_PREP_ROWS

    def prep(i, _):
        blocks = [pl.ds(pl.multiple_of((2 * i + b) * n_rows, n_rows), n_rows) for b in range(2)]
        xs = [(ref[0, rows, :].astype(F32), w_ref[...], rows)
              for rows in blocks for ref, w_ref in ((q_ref, qw_ref), (k_ref, kw_ref))]
        ms = [_dot((x * x).astype(BF16), mean_ref[...]) for x, _, _ in xs]
        ys = [x * lax.rsqrt(m + RMS_EPS) * w for (x, w, _), m in zip(xs, ms)]
        partner = [_dot(jnp.concatenate(_split_bf16(y), axis=1), perm_ref[...]) for y in ys]
        ys = [y * cos_ref[rows, :] + pr * sin_ref[rows, :] for y, pr, (_, _, rows) in zip(ys, partner, xs)]
        for b, rows in enumerate(blocks):
            q, k = ys[2 * b], ys[2 * b + 1]
            qf[rows, :] = q
            kf[rows, :] = k
            qn[rows, :] = q.astype(BF16)
            kn[rows, :] = k.astype(BF16)
            vf[rows, :] = v_ref[0, rows, :].astype(F32)
        return 0

    lax.fori_loop(0, seq // (2 * n_rows), prep, 0)

    qi = lax.broadcasted_iota(jnp.int32, (DW_BLOCK, DW_BLOCK), 0)
    kj = lax.broadcasted_iota(jnp.int32, (DW_BLOCK, DW_BLOCK), 1)
    own_ok = kj <= qi
    prev_ok = kj >= qi
    both_ok = jnp.concatenate([prev_ok, own_ok], axis=1)

    def pair_of(a, b):
        return jnp.where(head_lo, a, b)

    d_mid, d_far = DILATIONS[1], DILATIONS[2]
    len_mid, len_far = seq // d_mid, seq // d_far
    for src, mid_f, mid_b, far_b in ((qf, q4f, qg4, qg16), (kf, k4f, kg4, kg16), (vf, v4f, vg4, vg16)):
        for r in range(d_mid):
            x = src[pl.ds(r, len_mid, stride=d_mid), :]
            mid_f[pl.ds(r * len_mid, len_mid), :] = x
            mid_b[pl.ds(r * len_mid, len_mid), :] = x.astype(BF16)
        for r in range(d_far):
            x = mid_f[pl.ds((r % d_mid) * len_mid + r // d_mid, len_far, stride=d_far // d_mid), :]
            far_b[pl.ds(r * len_far, len_far), :] = x.astype(BF16)
    sources = {d_far: (qg16, kg16, lambda ks: vg16[ks, :]), d_mid: (qg4, kg4, lambda ks: vg4[ks, :]),
               1: (qn, kn, lambda ks: v_ref[0, ks, :])}

    state_mid, state_nat = (m1_md, m2_md, l_md, acc_md), (m1_st, m2_st, l_st, acc_st)

    def nat_rows(it):
        return pl.ds(it["start"], DW_BLOCK, stride=it["d"]) if it["d"] > 1 else pl.ds(it["start"], DW_BLOCK)

    def state_in(it):
        if it["d"] == d_mid:
            return state_mid, pl.ds(it["c0"], DW_BLOCK)
        return state_nat, nat_rows(it)

    def state_out(it):
        if it["d"] == d_far:
            r = it["start"]
            return state_mid, pl.ds((r % d_mid) * len_mid + r // d_mid, DW_BLOCK, stride=d_far // d_mid)
        return state_nat, nat_rows(it)

    def stage_scores(it):
        q_src, k_src, _ = sources[it["d"]]
        c0 = it["c0"]
        keys = pl.ds(c0 - DW_BLOCK, 2 * DW_BLOCK) if it["has_prev"] else pl.ds(c0, DW_BLOCK)
        ok = both_ok if it["has_prev"] else own_ok
        q = q_src[pl.ds(c0, DW_BLOCK), :]
        zero = jnp.zeros_like(q)
        k = k_src[keys, :]
        s = [jnp.where(ok, _dot_nt(qh, k), NEG) for qh in (jnp.where(head_lo, q, zero), jnp.where(head_lo, zero, q))]
        return dict(s=s, m_blk=[jnp.max(sh, axis=1, keepdims=True) for sh in s], keys=keys)

    def stage_softmax(it, st):
        width = 2 if it["has_prev"] else 1
        if it["d"] == d_far:
            m_h = [jnp.broadcast_to(m, (DW_BLOCK, LANES)) for m in st["m_blk"]]
            alpha = None
        else:
            (m1_in, m2_in, _, _), rows_in = state_in(it)
            m_old = (m1_in[rows_in, :], m2_in[rows_in, :])
            m_h = [jnp.maximum(o, b) for o, b in zip(m_old, st["m_blk"])]
            alpha = jnp.exp2(pair_of(m_old[0], m_old[1]) - pair_of(m_h[0], m_h[1]))
        p = [jnp.exp2(sh - jnp.concatenate([mh] * width, axis=1)).astype(BF16) for sh, mh in zip(st["s"], m_h)]
        return dict(p=p, m_h=m_h, alpha=alpha, keys=st["keys"])

    def stage_values(it, st):
        _, _, v_of = sources[it["d"]]
        v = v_of(st["keys"])
        vo = jnp.concatenate([v, jnp.ones_like(v)], axis=1)
        pvl = [_dot(ph, vo) for ph in st["p"]]
        pv = pair_of(pvl[0][:, :LANES], pvl[1][:, :LANES])
        l_new = pair_of(pvl[0][:, LANES:], pvl[1][:, LANES:])
        if st["alpha"] is not None:
            (_, _, l_in, acc_in), rows_in = state_in(it)
            l_new = st["alpha"] * l_in[rows_in, :] + l_new
            pv = st["alpha"] * acc_in[rows_in, :] + pv
        if it["d"] == 1:
            nat = nat_rows(it)
            zg = z_ref[0, nat, :].astype(F32)
            o_ref[0, nat, :] = (pv / l_new * (zg * _sigmoid(zg))).astype(BF16)
        else:
            (m1_out, m2_out, l_out, acc_out), rows_out = state_out(it)
            m1_out[rows_out, :] = st["m_h"][0]
            m2_out[rows_out, :] = st["m_h"][1]
            l_out[rows_out, :] = l_new
            acc_out[rows_out, :] = pv

    groups = [[dict(d=d_far, c0=r * len_far, start=r, has_prev=False) for r in range(d_far)],
              [dict(d=d_mid, c0=r * len_mid + j * DW_BLOCK, start=r + j * DW_BLOCK * d_mid, has_prev=j > 0)
               for j in range(len_mid // DW_BLOCK) for r in range(d_mid)],
              [dict(d=1, c0=j * DW_BLOCK, start=j * DW_BLOCK, has_prev=j > 0) for j in range(seq // DW_BLOCK)]]
    slots = []
    for g in groups:
        slots += [g[i:i + DW_SLOT] for i in range(0, len(g), DW_SLOT)] + [[]]
    slots += [[]]
    scored, soft = {}, {}
    for step, slot in enumerate(slots):
        for n, it in enumerate(slot):
            scored[step, n] = stage_scores(it)
        if step >= 1:
            for n, it in enumerate(slots[step - 1]):
                soft[step - 1, n] = stage_softmax(it, scored.pop((step - 1, n)))
        if step >= 2:
            for n, it in enumerate(slots[step - 2]):
                stage_values(it, soft.pop((step - 2, n)))


def _rope_tables(seq):
    half = ROPE_DIM // 2
    inv_freq = ROPE_THETA ** (-jnp.arange(half, dtype=F32) / half)
    ang = jnp.arange(seq, dtype=jnp.int32).astype(F32)[:, None] * inv_freq[None, :]
    cos, sin = jnp.cos(ang), jnp.sin(ang)
    ones = jnp.ones((seq, HEAD_DIM - ROPE_DIM), F32)
    zeros_t = jnp.zeros((seq, HEAD_DIM - ROPE_DIM), F32)
    cos_h = jnp.concatenate([cos, cos, ones], axis=1)
    sin_h = jnp.concatenate([-sin, sin, zeros_t], axis=1)
    tile = lambda t: jnp.tile(t, (1, PAIR))
    src = lax.broadcasted_iota(jnp.int32, (LANES, LANES), 0)
    dst = lax.broadcasted_iota(jnp.int32, (LANES, LANES), 1)
    in_head = dst % HEAD_DIM
    want = jnp.where(in_head < half, dst + half, jnp.where(in_head < ROPE_DIM, dst - half, -1))
    perm = (src == want).astype(BF16)
    return tile(cos_h), tile(sin_h), jnp.concatenate([perm, perm], axis=0)


def _dilated(pc, q_norm_w, k_norm_w):
    bsz, seq, _ = pc.shape
    npair = N_HEADS_C // PAIR
    assert seq % (DW_BLOCK * DILATIONS[-1]) == 0 and DILATIONS[-1] % DILATIONS[1] == 0
    cos, sin, perm = _rope_tables(seq)
    qw = jnp.tile(q_norm_w.astype(F32), PAIR)[None, :] * QSCALE2
    kw = jnp.tile(k_norm_w.astype(F32), PAIR)[None, :]
    head_mean = _ones_bd() * jnp.asarray(1.0 / HEAD_DIM, BF16)
    spec = lambda g: pl.BlockSpec((1, seq, LANES), lambda b, p, g=g: (b, 0, g * npair + p))
    fixed = lambda shape: pl.BlockSpec(shape, lambda b, p: (0, 0))
    f32_rows, bf16_rows = pltpu.VMEM((seq, LANES), F32), pltpu.VMEM((seq, LANES), BF16)
    return pl.pallas_call(
        _dw_kernel,
        grid=(bsz, npair),
        in_specs=[spec(0), spec(1), spec(2), spec(3), fixed((1, LANES)), fixed((1, LANES)),
                  fixed((seq, LANES)), fixed((seq, LANES)), fixed((2 * LANES, LANES)), fixed((LANES, LANES))],
        out_specs=pl.BlockSpec((1, seq, LANES), lambda b, p: (b, 0, p)),
        out_shape=jax.ShapeDtypeStruct((bsz, seq, C_WIDTH), BF16),
        scratch_shapes=[f32_rows] * 6 + [bf16_rows] * 8 + [f32_rows] * 8,
        compiler_params=pltpu.CompilerParams(dimension_semantics=("arbitrary", "arbitrary"),
                                             vmem_limit_bytes=VMEM_LIMIT),
        name="dilated_window",
    )(pc, pc, pc, pc, qw, kw, cos, sin, perm, head_mean)


GDN_ROWS = 256
GDN_HALO = 16
GDN_GROUP = 32


def _gdn_constants():
    npair = N_HEADS_A // PAIR
    iota = lambda shape, dim: lax.broadcasted_iota(jnp.int32, shape, dim)
    n_shift = CONV_WIDTH - 1
    sr, sc = iota((n_shift * GDN_ROWS, GDN_ROWS), 0), iota((n_shift * GDN_ROWS, GDN_ROWS), 1)
    shift = sc == sr % GDN_ROWS - (CONV_WIDTH - 1) + sr // GDN_ROWS
    hr, hc = iota((n_shift * SUBLANES, GDN_HALO), 0), iota((n_shift * SUBLANES, GDN_HALO), 1)
    halo = hc == GDN_HALO + hr % SUBLANES - (CONV_WIDTH - 1) + hr // SUBLANES
    ri, ci = iota((GDN_ROWS, 2 * GDN_ROWS), 0), iota((GDN_ROWS, 2 * GDN_ROWS), 1) % GDN_ROWS
    cum = (ri // GDN_CHUNK == ci // GDN_CHUNK) & (ci <= ri)
    er, ec = iota((2 * LANES, 2 * npair * LANES), 0) % LANES, iota((2 * LANES, 2 * npair * LANES), 1)
    expand = er == (ec // LANES % 2) * N_HEADS_A + 2 * (ec // (2 * LANES)) + ec % LANES // HEAD_DIM
    same_head = iota((2 * LANES, 2 * LANES), 0) // HEAD_DIM == iota((2 * LANES, 2 * LANES), 1) // HEAD_DIM
    bd4 = jnp.where(same_head, jnp.where(iota((2 * LANES, 2 * LANES), 1) < LANES, float(HEAD_DIM), 1.0), 0.0)
    head_mean = _ones_bd() * jnp.asarray(1.0 / HEAD_DIM, BF16)
    return [m.astype(BF16) for m in (shift, halo, cum, expand, bd4, head_mean)]


def _gdn_kernel(pa_ref, ba_ref, cw_ref, alog_ref, dtb_ref, gnw_ref, shift_ref, halo_ref, cum_ref, expand_ref,
                bd4_ref, mean_ref, o_ref, qn_s, kn_s, v_s, be_s, gc_s, qt_s, o0_s, nf_s, kw_s, gl_s, zg_s):
    seq = pa_ref.shape[1]
    n_chunks = seq // GDN_CHUNK
    npair = N_HEADS_A // PAIR
    lane = lax.broadcasted_iota(jnp.int32, (1, LANES), 1)
    head_lo = lane < HEAD_DIM
    bd_mask = (lax.broadcasted_iota(jnp.int32, (LANES, LANES), 0) // HEAD_DIM
               == lax.broadcasted_iota(jnp.int32, (LANES, LANES), 1) // HEAD_DIM)

    ii = lax.broadcasted_iota(jnp.int32, (GDN_CHUNK, LANES), 0)
    jl = lax.broadcasted_iota(jnp.int32, (GDN_CHUNK, LANES), 1) % HEAD_DIM
    incl = ii >= jl
    strict = ii > jl
    eye_pair = ii == jl
    eye_f = eye_pair.astype(F32)

    def qkv_cols(p):
        return [slice(t * A_WIDTH + p * LANES, t * A_WIDTH + (p + 1) * LANES) for t in range(3)]

    def prep_taps(rows, p):
        return _dot(shift_ref[...], jnp.concatenate([pa_ref[0, rows, c] for c in qkv_cols(p)], axis=1))

    def prep_gates(i, rows):
        r0 = i * GDN_ROWS
        halo_rows = pl.ds(pl.multiple_of(jnp.maximum(r0 - GDN_HALO, 0), GDN_HALO), GDN_HALO)
        halo = pa_ref[0, halo_rows, 0:3 * A_WIDTH]
        halo = jnp.where(i > 0, halo, jnp.zeros_like(halo))
        edge = _dot(halo_ref[...], halo)

        ba = ba_ref[0, rows, :]
        g = -jnp.exp(alog_ref[...]) * _softplus(ba + dtb_ref[...])
        g = jnp.where((lane >= N_HEADS_A) & (lane < 2 * N_HEADS_A), g, 0.0)
        hi, lo = _split_bf16(g)
        gc = _dot(cum_ref[...], jnp.concatenate([hi, lo], axis=0))
        nar = jnp.where(lane < N_HEADS_A, _sigmoid(ba), gc)
        hi, lo = _split_bf16(nar)
        return edge, _dot(jnp.concatenate([hi, lo], axis=1), expand_ref[...])

    def prep_conv(rows, taps, edge, wide):
        def conv_silu(p, t):
            c = qkv_cols(p)[t]
            w = cw_ref[:, c]
            y = w[CONV_WIDTH - 1:CONV_WIDTH, :] * pa_ref[0, rows, c].astype(F32)
            y0 = 0.0
            for kk in range(CONV_WIDTH - 1):
                y = y + w[kk:kk + 1, :] * taps[p][kk * GDN_ROWS:(kk + 1) * GDN_ROWS, t * LANES:(t + 1) * LANES]
                y0 = y0 + w[kk:kk + 1, :] * edge[kk * SUBLANES:(kk + 1) * SUBLANES, c]
            y = jnp.concatenate([y[:SUBLANES, :] + y0, y[SUBLANES:, :]], axis=0)
            return y * _sigmoid(y)

        ys = [[conv_silu(p, t) for t in range(3)] for p in range(npair)]
        ss = [_dot(jnp.concatenate([y[0] * y[0], y[1] * y[1]], axis=1).astype(BF16), bd4_ref[...]) for y in ys]
        for p in range(npair):
            qn_s[p, rows, :] = ys[p][0] * lax.rsqrt(ss[p][:, :LANES] + HEAD_DIM * RMS_EPS)
            kn_s[p, rows, :] = ys[p][1] * lax.rsqrt(ss[p][:, LANES:] + RMS_EPS)
            v_s[p, rows, :] = ys[p][2]
            be_s[p, rows, :] = wide[:, 2 * p * LANES:(2 * p + 1) * LANES]
            gc_s[p, rows, :] = wide[:, (2 * p + 1) * LANES:(2 * p + 2) * LANES]

    def bd(ms):
        return [_block_diag(m, head_lo) for m in ms]

    def chunk_group(p, i, _):
        cs = [i * GDN_GROUP + gi for gi in range(GDN_GROUP)]
        rows = [pl.ds(pl.multiple_of(c * GDN_CHUNK, GDN_CHUNK), GDN_CHUNK) for c in cs]
        qn = [qn_s[p, r, :] for r in rows]
        kn = [kn_s[p, r, :] for r in rows]
        v = [v_s[p, r, :] for r in rows]
        be = [be_s[p, r, :] for r in rows]
        gc = [gc_s[p, r, :] for r in rows]
        kb = [a * b for a, b in zip(kn, be)]
        ap = [_dot_nt(jnp.concatenate([a, b], axis=0).astype(BF16), m) for a, b, m in zip(kb, qn, bd(kn))]
        g_row = [jnp.sum(jnp.where(eye_pair, x, 0.0), axis=0, keepdims=True) for x in gc]
        decay = [jnp.exp(jnp.where(incl, x - y, NEG)) for x, y in zip(gc, g_row)]
        x = [jnp.where(strict, -a[:GDN_CHUNK] * d, 0.0) for a, d in zip(ap, decay)]
        pm = [a[GDN_CHUNK:] * d for a, d in zip(ap, decay)]
        tm = [eye_f + a for a in x]
        xr = [_dot(a.astype(BF16), m) for a, m in zip(x, bd(x))]
        for r in range(1, 6):
            rhs = bd(xr)
            if r < 5:
                y = [_dot(jnp.concatenate([a, b], axis=0).astype(BF16), m) for a, b, m in zip(xr, tm, rhs)]
                xr = [a[:GDN_CHUNK] for a in y]
                tm = [a + b[GDN_CHUNK:] for a, b in zip(tm, y)]
            else:
                tm = [a + _dot(a.astype(BF16), m) for a, m in zip(tm, rhs)]
        eg = [jnp.exp(a) for a in gc]
        vb = bd([a * b for a, b in zip(v, be)])
        kbg = bd([a * b for a, b in zip(kb, eg)])
        uw = [_dot(a.astype(BF16), jnp.concatenate([b, c], axis=1)) for a, b, c in zip(tm, vb, kbg)]
        u_bd = bd([a[:, :LANES] for a in uw])
        w_bd = bd([a[:, LANES:] for a in uw])
        puw = [_dot(a.astype(BF16), jnp.concatenate([b, c], axis=1)) for a, b, c in zip(pm, u_bd, w_bd)]
        g_last = [a[GDN_CHUNK - 1:GDN_CHUNK, :] for a in gc]
        kg = [a * jnp.exp(b - c) for a, b, c in zip(kn, g_last, gc)]
        kuw = [_dot_tn(a.astype(BF16), b.astype(BF16)) for a, b in zip(kg, uw)]
        for c, r, q, e, a, b, gl in zip(cs, rows, qn, eg, puw, kuw, g_last):
            qt_s[p, r, :] = (q * e - a[:, LANES:]).astype(BF16)
            o0_s[p, r, :] = a[:, :LANES]
            nf_s[p, c] = jnp.where(bd_mask, b[:, :LANES], 0.0)
            kw_s[p, c] = jnp.where(bd_mask, b[:, LANES:], 0.0).astype(BF16)
            gl_s[p, c] = jnp.broadcast_to(jnp.exp(gl), (SUBLANES, LANES))
        return 0

    def per_pair(p, _):
        lax.fori_loop(0, n_chunks // GDN_GROUP, functools.partial(chunk_group, p), 0)
        return 0

    step = pl.program_id(0)
    last_step = pl.num_programs(0) - 1
    steps_per_rows = GDN_ROWS // GDN_CHUNK

    @pl.when(step == 0)
    def _():
        def zero(c, _):
            rows = pl.ds(pl.multiple_of(c * GDN_CHUNK, GDN_CHUNK), GDN_CHUNK)
            zg_s[rows, :] = jnp.zeros((GDN_CHUNK, A_WIDTH), BF16)
            for p in range(npair):
                qt_s[p, rows, :] = jnp.zeros((GDN_CHUNK, LANES), BF16)
                o0_s[p, rows, :] = jnp.zeros((GDN_CHUNK, LANES), F32)
                nf_s[p, c] = jnp.zeros((LANES, LANES), F32)
                kw_s[p, c] = jnp.zeros((LANES, LANES), BF16)
                gl_s[p, c] = jnp.zeros((SUBLANES, LANES), F32)
            return 0

        lax.fori_loop(0, n_chunks, zero, 0)

    def rows_step(with_prep, i, states):
        rows = pl.ds(pl.multiple_of(i * GDN_ROWS, GDN_ROWS), GDN_ROWS)
        outs = [[] for _ in range(npair)]
        taps = []
        for s in range(steps_per_rows):
            c = i * steps_per_rows + s
            crow = pl.ds(pl.multiple_of(c * GDN_CHUNK, GDN_CHUNK), GDN_CHUNK)
            sb = [st.astype(BF16) for st in states]
            ks = [_dot(kw_s[p, c], sb[p]) for p in range(npair)]
            for p in range(npair):
                outs[p].append(_dot(qt_s[p, crow, :], sb[p]) + o0_s[p, crow, :])
            states = tuple(gl_s[p, c][0:1, :] * states[p] + nf_s[p, c] - ks[p] for p in range(npair))
            if with_prep and s < npair:
                taps.append(prep_taps(rows, s))
        if with_prep:
            edge, wide = prep_gates(i, rows)
        for p in range(npair):
            o = jnp.concatenate(outs[p], axis=0)
            ms = _dot((o * o).astype(BF16), mean_ref[...])
            y = o * lax.rsqrt(ms + RMS_EPS) * gnw_ref[...]
            zg = zg_s[rows, p * LANES:(p + 1) * LANES].astype(F32)
            o_ref[0, rows, p * LANES:(p + 1) * LANES] = (y * (zg * _sigmoid(zg))).astype(BF16)
        if with_prep:
            prep_conv(rows, taps, edge, wide)
            zg_s[rows, :] = pa_ref[0, rows, 3 * A_WIDTH:4 * A_WIDTH]
        return states

    zero_states = tuple(jnp.zeros((LANES, LANES), F32) for _ in range(npair))

    @pl.when(step < last_step)
    def _():
        lax.fori_loop(0, seq // GDN_ROWS, functools.partial(rows_step, True), zero_states)
        lax.fori_loop(0, npair, per_pair, 0)

    @pl.when(step == last_step)
    def _():
        lax.fori_loop(0, seq // GDN_ROWS, functools.partial(rows_step, False), zero_states)


def _gdn(pa, ba, conv_w, a_log, dt_bias, gdn_norm_w):
    bsz, seq, _ = pa.shape
    npair = N_HEADS_A // PAIR
    n_chunks = seq // GDN_CHUNK
    pad = lambda vec: jnp.zeros((1, LANES), F32).at[0, N_HEADS_A:2 * N_HEADS_A].set(vec.astype(F32))
    gnw = jnp.tile(gdn_norm_w.astype(F32), PAIR)[None, :]
    fixed = lambda shape: pl.BlockSpec(shape, lambda b: (0,) * len(shape))
    per_pair_f32 = pltpu.VMEM((npair, seq, LANES), F32)
    consts = _gdn_constants()
    return pl.pallas_call(
        _gdn_kernel,
        grid=(bsz + 1,),
        in_specs=[pl.BlockSpec((1, seq, PA_COLS), lambda b: (jnp.minimum(b, bsz - 1), 0, 0)),
                  pl.BlockSpec((1, seq, BA_COLS), lambda b: (jnp.minimum(b, bsz - 1), 0, 0)),
                  fixed((CONV_WIDTH, 3 * A_WIDTH)), fixed((1, LANES)), fixed((1, LANES)), fixed((1, LANES))]
        + [fixed(m.shape) for m in consts],
        out_specs=pl.BlockSpec((1, seq, A_WIDTH), lambda b: (jnp.maximum(b - 1, 0), 0, 0)),
        out_shape=jax.ShapeDtypeStruct((bsz, seq, A_WIDTH), BF16),
        scratch_shapes=[
            per_pair_f32, per_pair_f32, per_pair_f32,
            per_pair_f32, per_pair_f32,
            pltpu.VMEM((npair, seq, LANES), BF16),
            per_pair_f32,
            pltpu.VMEM((npair, n_chunks, LANES, LANES), F32),
            pltpu.VMEM((npair, n_chunks, LANES, LANES), BF16),
            pltpu.VMEM((npair, n_chunks, SUBLANES, LANES), F32),
            pltpu.VMEM((seq, A_WIDTH), BF16),
        ],
        compiler_params=pltpu.CompilerParams(dimension_semantics=("arbitrary",), vmem_limit_bytes=VMEM_LIMIT),
        name="gated_delta",
    )(pa, ba, conv_w.astype(F32), pad(a_log), pad(dt_bias), gnw, *consts)


ROW_TILE = 1024
OUT_ROW_TILE = 2048


def kernel(x, norm_w, w_in, conv_w, a_log, dt_bias, gdn_norm_w, q_norm_w, k_norm_w, w_out):
    bsz, seq, _ = x.shape
    assert w_in.shape[1:] == (D_MODEL, IN_COLS)
    shape3 = lambda t: t.reshape(bsz, seq, t.shape[-1])
    flat = lambda t: t.reshape(bsz * seq, t.shape[-1])
    x2d = flat(x)
    w_in = w_in.astype(F32)
    norm_w = norm_w.astype(F32)
    for layer in range(norm_w.shape[0]):
        pa, ba, pb, pc = _in_proj(x2d, norm_w[layer][None, :], w_in, layer, ROW_TILE)
        oa = _gdn(shape3(pa), shape3(ba), conv_w[layer], a_log[layer], dt_bias[layer], gdn_norm_w[layer])
        ob = _stick_breaking(shape3(pb))
        oc = _dilated(shape3(pc), q_norm_w[layer], k_norm_w[layer])
        x2d = _out_proj(x2d, flat(oa), flat(ob), flat(oc), w_out[layer].astype(BF16), OUT_ROW_TILE)
    return shape3(x2d)
```
